```python
import math
import jax, jax.numpy as jnp
from jax import lax
import numpy as np

D_MODEL = 2048
BATCH = 2
SEQ = 4096
DEPTH = 2

HEAD_DIM = 128
N_HEADS_TOTAL = D_MODEL // HEAD_DIM
N_HEADS_SB = N_HEADS_TOTAL // 4
N_HEADS_DIFF = N_HEADS_TOTAL // 4
N_HEADS_DIL = N_HEADS_TOTAL - N_HEADS_SB - N_HEADS_DIFF
DIFF_QK_DIM = HEAD_DIM // 2
W_SB = N_HEADS_SB * HEAD_DIM
W_DIFF = N_HEADS_DIFF * HEAD_DIM
W_DIL = N_HEADS_DIL * HEAD_DIM
MIX_WIDTH = W_SB + W_DIFF + W_DIL
IN_PROJ_WIDTH = 3 * MIX_WIDTH
D_FF = ((8 * D_MODEL // 3 + 255) // 256) * 256
ROPE_THETA = 500000.0
ROPE_FRACTION = 0.25
DILATED_PATTERNS = ((128, 1), (512, 4), (2048, 16))
BLOCK_Q = 128
NORM_EPS = 1e-6

kernel_name = "hybrid_sb_diff_dilated_parallel_heads"


def rmsnorm(x, g):
    x32 = x.astype(jnp.float32)
    y = x32 * lax.rsqrt(jnp.mean(x32 * x32, axis=-1, keepdims=True) + NORM_EPS)
    return y.astype(x.dtype) * g


def partial_rope(x, positions):
    rot = int(x.shape[-1] * ROPE_FRACTION)
    half = rot // 2
    inv_freq = ROPE_THETA ** (-jnp.arange(half, dtype=jnp.float32) / half)
    ang = positions.astype(jnp.float32)[:, None, :, None] * inv_freq
    cos, sin = jnp.cos(ang), jnp.sin(ang)
    xr = x[..., :rot].astype(jnp.float32)
    x1, x2 = xr[..., :half], xr[..., half:]
    rotated = jnp.concatenate([x1 * cos - x2 * sin, x2 * cos + x1 * sin], axis=-1)
    return jnp.concatenate([rotated.astype(x.dtype), x[..., rot:]], axis=-1)


def to_heads(t, n_heads, dh):
    b, s, _ = t.shape
    return t.reshape(b, s, n_heads, dh).transpose(0, 2, 1, 3)


def merge_heads(t):
    b, h, s, dh = t.shape
    return t.transpose(0, 2, 1, 3).reshape(b, s, h * dh)


def query_blocks(q):
    *lead, s, dh = q.shape
    nb = s // BLOCK_Q
    qb = q.reshape(*lead, nb, BLOCK_Q, dh)
    return jnp.moveaxis(qb, -3, 0), nb


def unblock(o):
    o = jnp.moveaxis(o, 0, -3)
    *lead, nb, bq, dh = o.shape
    return o.reshape(*lead, nb * bq, dh)


def stick_breaking_attention(q, k, v):
    s_len, dh = q.shape[-2], q.shape[-1]
    scale = dh ** -0.5
    qb, nb = query_blocks(q)
    key_pos = jnp.arange(s_len)

    def one_block(args):
        blk, qblk = args
        q_pos = blk * BLOCK_Q + jnp.arange(BLOCK_Q)
        z = jnp.einsum('bhqd,bhkd->bhqk', qblk, k, preferred_element_type=jnp.float32) * scale
        strict = key_pos[None, :] < q_pos[:, None]
        neg_log_keep = jnp.where(strict, jax.nn.softplus(z), 0.0)
        between = lax.cumsum(neg_log_keep, axis=3, reverse=True) - neg_log_keep
        a = jnp.where(strict, jnp.exp(jax.nn.log_sigmoid(z) - between), 0.0)
        return jnp.einsum('bhqk,bhkd->bhqd', a.astype(v.dtype), v)

    return unblock(lax.map(one_block, (jnp.arange(nb), qb)))


def differential_attention(q2, k2, v, lam):
    s_len, dqk = q2.shape[-2], q2.shape[-1]
    scale = dqk ** -0.5
    qb, nb = query_blocks(q2)
    key_pos = jnp.arange(s_len)

    def one_block(args):
        blk, qblk = args
        q_pos = blk * BLOCK_Q + jnp.arange(BLOCK_Q)
        sc = jnp.einsum('cbhqd,cbhkd->cbhqk', qblk, k2, preferred_element_type=jnp.float32) * scale
        causal = key_pos[None, :] <= q_pos[:, None]
        p = jax.nn.softmax(jnp.where(causal, sc, -jnp.inf), axis=-1)
        attn = p[0] - lam * p[1]
        return jnp.einsum('bhqk,bhkd->bhqd', attn.astype(v.dtype), v)

    return unblock(lax.map(one_block, (jnp.arange(nb), qb)))


def dilated_attention(q, k, v):
    dh = q.shape[-1]
    scale = dh ** -0.5
    qb, nb = query_blocks(q)

    def one_block(args):
        blk, qblk = args
        q_pos = blk * BLOCK_Q + jnp.arange(BLOCK_Q)
        outs, lses = [], []
        for window, dilation in DILATED_PATTERNS:
            n_keys = window // dilation + 1
            idx = q_pos[:, None] - dilation * jnp.arange(n_keys)[None, :]
            valid = idx >= 0
            idx_c = jnp.maximum(idx, 0)
            kg = jnp.take(k, idx_c, axis=2)
            vg = jnp.take(v, idx_c, axis=2)
            sc = jnp.einsum('bhqd,bhqmd->bhqm', qblk, kg, preferred_element_type=jnp.float32) * scale
            sc = jnp.where(valid, sc, -jnp.inf)
            m = jnp.max(sc, axis=-1, keepdims=True)
            p = jnp.exp(sc - m)
            l = jnp.sum(p, axis=-1, keepdims=True)
            outs.append(jnp.einsum('bhqm,bhqmd->bhqd', (p / l).astype(v.dtype), vg))
            lses.append(m + jnp.log(l))
        w = jax.nn.softmax(jnp.concatenate(lses, axis=-1), axis=-1)
        o = jnp.einsum('pbhqd,bhqp->bhqd', jnp.stack(outs).astype(jnp.float32), w)
        return o.astype(v.dtype)

    return unblock(lax.map(one_block, (jnp.arange(nb), qb)))


def mixing_sublayer(h, positions, layer_idx, w_in, lambda_q1, lambda_k1, lambda_q2, lambda_k2,
                    g_sb_out, g_diff_out, g_dil_out, w_out):
    b, s, _ = h.shape
    proj = h @ w_in
    o0 = 0
    sb_q, sb_k, sb_v = (proj[..., o0 + i * W_SB:o0 + (i + 1) * W_SB] for i in range(3))
    o0 += 3 * W_SB
    df_q, df_k, df_v = (proj[..., o0 + i * W_DIFF:o0 + (i + 1) * W_DIFF] for i in range(3))
    o0 += 3 * W_DIFF
    dl_q, dl_k, dl_v = (proj[..., o0 + i * W_DIL:o0 + (i + 1) * W_DIL] for i in range(3))

    o_sb = stick_breaking_attention(to_heads(sb_q, N_HEADS_SB, HEAD_DIM),
                                    to_heads(sb_k, N_HEADS_SB, HEAD_DIM),
                                    to_heads(sb_v, N_HEADS_SB, HEAD_DIM))
    o_sb = rmsnorm(o_sb, g_sb_out)

    def two_component(t):
        return t.reshape(b, s, N_HEADS_DIFF, 2, DIFF_QK_DIM).transpose(3, 0, 2, 1, 4)
    dq = partial_rope(two_component(df_q), positions)
    dk = partial_rope(two_component(df_k), positions)
    lambda_init = 0.8 - 0.6 * math.exp(-0.3 * layer_idx)
    lam = (jnp.exp(jnp.sum(lambda_q1.astype(jnp.float32) * lambda_k1.astype(jnp.float32)))
           - jnp.exp(jnp.sum(lambda_q2.astype(jnp.float32) * lambda_k2.astype(jnp.float32)))
           + lambda_init)
    o_df = differential_attention(dq, dk, to_heads(df_v, N_HEADS_DIFF, HEAD_DIM), lam)
    o_df = rmsnorm(o_df, g_diff_out) * (1.0 - lambda_init)

    cq = partial_rope(to_heads(dl_q, N_HEADS_DIL, HEAD_DIM), positions)
    ck = partial_rope(to_heads(dl_k, N_HEADS_DIL, HEAD_DIM), positions)
    o_dl = dilated_attention(cq, ck, to_heads(dl_v, N_HEADS_DIL, HEAD_DIM))
    o_dl = rmsnorm(o_dl, g_dil_out)

    mixed = jnp.concatenate([merge_heads(o_sb), merge_heads(o_df), merge_heads(o_dl)], axis=-1)
    return mixed @ w_out


def swiglu_ffn(h, w_gate, w_up, w_down):
    return (jax.nn.silu(h @ w_gate) * (h @ w_up)) @ w_down


def setup_inputs(seed: int = 0) -> dict:
    key = jax.random.key(seed)
    ks = jax.random.split(key, 16)
    f32 = jnp.float32

    def nrm(k, shape, scale):
        return jax.random.normal(k, shape, f32) * scale

    def gain(k, shape):
        return 1.0 + 0.02 * jax.random.normal(k, shape, f32)

    x = jax.random.normal(ks[0], (BATCH, SEQ, D_MODEL), f32)
    positions = jnp.broadcast_to(jnp.arange(SEQ, dtype=jnp.int32)[None, :], (BATCH, SEQ))
    return {
        "x": x,
        "positions": positions,
        "norm_mix_g": gain(ks[1], (DEPTH, D_MODEL)),
        "w_in": nrm(ks[2], (DEPTH, D_MODEL, IN_PROJ_WIDTH), D_MODEL ** -0.5),
        "lambda_q1": nrm(ks[3], (DEPTH, DIFF_QK_DIM), 0.1),
        "lambda_k1": nrm(ks[4], (DEPTH, DIFF_QK_DIM), 0.1),
        "lambda_q2": nrm(ks[5], (DEPTH, DIFF_QK_DIM), 0.1),
        "lambda_k2": nrm(ks[6], (DEPTH, DIFF_QK_DIM), 0.1),
        "g_sb_out": gain(ks[7], (DEPTH, HEAD_DIM)),
        "g_diff_out": gain(ks[8], (DEPTH, HEAD_DIM)),
        "g_dil_out": gain(ks[9], (DEPTH, HEAD_DIM)),
        "w_out": nrm(ks[10], (DEPTH, MIX_WIDTH, D_MODEL), MIX_WIDTH ** -0.5),
        "norm_ffn_g": gain(ks[11], (DEPTH, D_MODEL)),
        "w_gate": nrm(ks[12], (DEPTH, D_MODEL, D_FF), D_MODEL ** -0.5),
        "w_up": nrm(ks[13], (DEPTH, D_MODEL, D_FF), D_MODEL ** -0.5),
        "w_down": nrm(ks[14], (DEPTH, D_FF, D_MODEL), D_FF ** -0.5),
        "norm_final_g": gain(ks[15], (D_MODEL,)),
    }


def reference(x, positions, norm_mix_g, w_in, lambda_q1, lambda_k1, lambda_q2, lambda_k2,
              g_sb_out, g_diff_out, g_dil_out, w_out, norm_ffn_g, w_gate, w_up, w_down,
              norm_final_g):
    for layer in range(DEPTH):
        h = rmsnorm(x, norm_mix_g[layer])
        x = x + mixing_sublayer(h, positions, layer, w_in[layer],
                                lambda_q1[layer], lambda_k1[layer], lambda_q2[layer], lambda_k2[layer],
                                g_sb_out[layer], g_diff_out[layer], g_dil_out[layer], w_out[layer])
        h = rmsnorm(x, norm_ffn_g[layer])
        x = x + swiglu_ffn(h, w_gate[layer], w_up[layer], w_down[layer])
    return rmsnorm(x, norm_final_g)
```

```python
import functools
import math

import jax
import jax.numpy as jnp
from jax import lax
from jax.experimental import pallas as pl
from jax.experimental.pallas import tpu as pltpu

F32 = jnp.float32
BF16 = jnp.bfloat16

HEAD_DIM = 128
N_SB = 4
N_DIFF = 4
N_DIL = 8
W_SB = N_SB * HEAD_DIM
W_DIFF = N_DIFF * HEAD_DIM
W_DIL = N_DIL * HEAD_DIM
DIFF_QK = HEAD_DIM // 2
ROPE_THETA = 500000.0
ROPE_FRACTION = 0.25
DILATIONS = (1, 4, 16)
DIL_SPAN = 128
NORM_EPS = 1e-6
LANES = 128
VMEM_LIMIT = 56 * 1024 * 1024

NEG_INF = float("-inf")


def _params(*sem):
    return pltpu.CompilerParams(dimension_semantics=sem, vmem_limit_bytes=VMEM_LIMIT)


def _dot(a, b):
    return jnp.dot(a, b, preferred_element_type=F32)


def _dot_nt(a, b):
    return lax.dot_general(a, b, (((1,), (1,)), ((), ())), preferred_element_type=F32)


def _rms(x, g):
    return x * lax.rsqrt(jnp.mean(x * x, axis=-1, keepdims=True) + NORM_EPS) * g


def _rmsnorm_kernel(x_ref, g_ref, o_ref):
    o_ref[...] = _rms(x_ref[...], g_ref[...]).astype(o_ref.dtype)


def rmsnorm(x, g, out_dtype, tm=256):
    m, d = x.shape
    return pl.pallas_call(
        _rmsnorm_kernel,
        grid=(m // tm,),
        in_specs=[pl.BlockSpec((tm, d), lambda i: (i, 0)),
                  pl.BlockSpec((1, d), lambda i: (0, 0))],
        out_specs=pl.BlockSpec((tm, d), lambda i: (i, 0)),
        out_shape=jax.ShapeDtypeStruct((m, d), out_dtype),
        compiler_params=_params("parallel"),
        name="rmsnorm",
    )(x, g.reshape(1, d))


def _rope_lane_consts():
    lane = jnp.arange(LANES)
    rows = []
    for chunk in (DIFF_QK, HEAD_DIM):
        rot = int(chunk * ROPE_FRACTION)
        half = rot // 2
        inv_freq = ROPE_THETA ** (-jnp.arange(half, dtype=F32) / half)
        pos = lane % chunk
        freq = jnp.where(pos < rot, inv_freq[pos % half], 0.0)
        lo = jnp.where(pos < half, -1.0, 0.0)
        hi = jnp.where((pos >= half) & (pos < rot), 1.0, 0.0)
        rows += [freq, lo, hi]
    rows += [jnp.zeros((LANES,)), jnp.zeros((LANES,))]
    return jnp.stack(rows).astype(F32)


def _rope_table_kernel(pos_ref, c_ref, o_ref):
    p = pos_ref[...].astype(F32)
    for k in range(2):
        ang = p * c_ref[3 * k:3 * k + 1, :]
        s = jnp.sin(ang)
        o_ref[3 * k] = jnp.cos(ang)
        o_ref[3 * k + 1] = s * c_ref[3 * k + 1:3 * k + 2, :]
        o_ref[3 * k + 2] = s * c_ref[3 * k + 2:3 * k + 3, :]


def rope_tables(positions, tm=512):
    m = positions.shape[0]
    return pl.pallas_call(
        _rope_table_kernel,
        grid=(m // tm,),
        in_specs=[pl.BlockSpec((tm, 1), lambda i: (i, 0)),
                  pl.BlockSpec((8, LANES), lambda i: (0, 0))],
        out_specs=pl.BlockSpec((6, tm, LANES), lambda i: (0, i, 0)),
        out_shape=jax.ShapeDtypeStruct((6, m, LANES), F32),
        compiler_params=_params("parallel"),
        name="rope_tables",
    )(positions, _rope_lane_consts())


ROPE_NONE, ROPE_DIFF, ROPE_DIL = 0, 1, 2
_ROPE_SHIFT = {ROPE_DIFF: int(DIFF_QK * ROPE_FRACTION) // 2, ROPE_DIL: int(HEAD_DIM * ROPE_FRACTION) // 2}


def _proj_kernel(a_ref, w_ref, tab_ref, o_ref, *, kinds, tn):
    j = pl.program_id(1)
    acc = _dot(a_ref[...], w_ref[...])

    def is_kind(kind):
        hit = None
        for idx, k in enumerate(kinds):
            if k == kind:
                hit = (j == idx) if hit is None else (hit | (j == idx))
        return hit

    for kind in sorted(set(kinds)):
        def write(kind=kind):
            if kind == ROPE_NONE:
                o_ref[...] = acc.astype(o_ref.dtype)
                return
            base = 3 * (kind - 1)
            shift = _ROPE_SHIFT[kind]
            c, lo, hi = tab_ref[base], tab_ref[base + 1], tab_ref[base + 2]
            for q in range(tn // LANES):
                xk = acc[:, q * LANES:(q + 1) * LANES]
                rot = (xk * c + pltpu.roll(xk, LANES - shift, 1) * lo
                       + pltpu.roll(xk, shift, 1) * hi)
                o_ref[:, q * LANES:(q + 1) * LANES] = rot.astype(o_ref.dtype)
        if len(set(kinds)) == 1:
            write()
        else:
            pl.when(is_kind(kind))(write)


def proj(a, w, tabs, kinds, out_dtype, tm=512, tn=512):
    m, k = a.shape
    n = w.shape[1]
    assert len(kinds) == n // tn
    return pl.pallas_call(
        functools.partial(_proj_kernel, kinds=tuple(kinds), tn=tn),
        grid=(m // tm, n // tn),
        in_specs=[pl.BlockSpec((tm, k), lambda i, j: (i, 0)),
                  pl.BlockSpec((k, tn), lambda i, j: (0, j)),
                  pl.BlockSpec((6, tm, LANES), lambda i, j: (0, i, 0))],
        out_specs=pl.BlockSpec((tm, tn), lambda i, j: (i, j)),
        out_shape=jax.ShapeDtypeStruct((m, n), out_dtype),
        compiler_params=_params("parallel", "arbitrary"),
        name="proj",
    )(a, w, tabs)


def _matmul_res_kernel(a_ref, w_ref, r_ref, o_ref):
    o_ref[...] = r_ref[...] + _dot(a_ref[...], w_ref[...])


def matmul_residual(a, w, res, tm=512, tn=512):
    m, k = a.shape
    n = w.shape[1]
    return pl.pallas_call(
        _matmul_res_kernel,
        grid=(m // tm, n // tn),
        in_specs=[pl.BlockSpec((tm, k), lambda i, j: (i, 0)),
                  pl.BlockSpec((k, tn), lambda i, j: (0, j)),
                  pl.BlockSpec((tm, tn), lambda i, j: (i, j))],
        out_specs=pl.BlockSpec((tm, tn), lambda i, j: (i, j)),
        out_shape=jax.ShapeDtypeStruct((m, n), F32),
        compiler_params=_params("parallel", "arbitrary"),
        name="out_proj",
    )(a, w, res)


def _ffn_kernel(h_ref, wg_ref, wu_ref, wd_ref, x_ref, o_ref, acc_ref):
    j = pl.program_id(1)

    @pl.when(j == 0)
    def _():
        acc_ref[...] = jnp.zeros_like(acc_ref)

    h = h_ref[...]
    gate = _dot(h, wg_ref[...])
    up = _dot(h, wu_ref[...])
    act = (gate * jax.nn.sigmoid(gate)) * up
    acc_ref[...] += _dot(act.astype(BF16), wd_ref[...])

    @pl.when(j == pl.num_programs(1) - 1)
    def _():
        o_ref[...] = x_ref[...] + acc_ref[...]


def ffn(h, wg, wu, wd, x, tm=512, tf=512):
    m, d = h.shape
    f = wg.shape[1]
    return pl.pallas_call(
        _ffn_kernel,
        grid=(m // tm, f // tf),
        in_specs=[pl.BlockSpec((tm, d), lambda i, j: (i, 0)),
                  pl.BlockSpec((d, tf), lambda i, j: (0, j)),
                  pl.BlockSpec((d, tf), lambda i, j: (0, j)),
                  pl.BlockSpec((tf, d), lambda i, j: (j, 0)),
                  pl.BlockSpec((tm, d), lambda i, j: (i, 0))],
        out_specs=pl.BlockSpec((tm, d), lambda i, j: (i, 0)),
        out_shape=jax.ShapeDtypeStruct((m, d), F32),
        scratch_shapes=[pltpu.VMEM((tm, d), F32)],
        compiler_params=_params("parallel", "arbitrary"),
        name="ffn",
    )(h, wg, wu, wd, x)


def _sb_kernel(q_ref, k_ref, v_ref, g_ref, o_ref, *, blk):
    i = pl.program_id(2)
    scale = HEAD_DIM ** -0.5
    q = (q_ref[...].astype(F32) * scale).astype(BF16)
    row = lax.broadcasted_iota(jnp.int32, (blk, blk), 0)
    col = lax.broadcasted_iota(jnp.int32, (blk, blk), 1)
    later = (row > col).astype(BF16)
    strict = col < row

    def block(j, carry, acc, mask):
        start = pl.multiple_of(j * blk, blk)
        k = k_ref[pl.ds(start, blk), :]
        v = v_ref[pl.ds(start, blk), :]
        z = _dot_nt(q, k)
        sp = jnp.maximum(z, 0.0) + jnp.log1p(jnp.exp(-jnp.abs(z)))
        spm = jnp.where(mask, sp, 0.0) if mask is not None else sp
        hi = spm.astype(BF16)
        lo = (spm - hi.astype(F32)).astype(BF16)
        between = _dot(hi, later) + _dot(lo, later) + carry
        loga = z - sp - between
        if mask is not None:
            loga = jnp.where(mask, loga, NEG_INF)
        a = jnp.exp(loga)
        acc = acc + _dot(a.astype(BF16), v)
        carry = carry + jnp.sum(spm, axis=-1, keepdims=True)
        return carry, acc

    carry = jnp.zeros((blk, 1), F32)
    acc = jnp.zeros((blk, HEAD_DIM), F32)
    carry, acc = block(i, carry, acc, strict)

    def body(t, state):
        return block(i - 1 - t, state[0], state[1], None)

    carry, acc = lax.fori_loop(0, i, body, (carry, acc))
    o_ref[...] = _rms(acc, g_ref[...]).astype(o_ref.dtype)


def sb_attention(pa, g, blk=256):
    b, s, _ = pa.shape
    return pl.pallas_call(
        functools.partial(_sb_kernel, blk=blk),
        grid=(b, N_SB, s // blk),
        in_specs=[pl.BlockSpec((None, blk, HEAD_DIM), lambda b_, h, i: (b_, i, h)),
                  pl.BlockSpec((None, s, HEAD_DIM), lambda b_, h, i: (b_, 0, N_SB + h)),
                  pl.BlockSpec((None, s, HEAD_DIM), lambda b_, h, i: (b_, 0, 2 * N_SB + h)),
                  pl.BlockSpec((1, HEAD_DIM), lambda b_, h, i: (0, 0))],
        out_specs=pl.BlockSpec((None, blk, HEAD_DIM), lambda b_, h, i: (b_, i, h)),
        out_shape=jax.ShapeDtypeStruct((b, s, W_SB), BF16),
        compiler_params=_params("parallel", "parallel", "arbitrary"),
        name="sb_attention",
    )(pa, pa, pa, g.reshape(1, HEAD_DIM))


def _diff_kernel(q_ref, k_ref, v_ref, lam_ref, g_ref, o_ref, *, blk, lambda_init):
    i = pl.program_id(2)
    scale = DIFF_QK ** -0.5
    q = q_ref[...].astype(F32) * scale
    lane = lax.broadcasted_iota(jnp.int32, (blk, HEAD_DIM), 1)
    q1 = jnp.where(lane < DIFF_QK, q, 0.0).astype(BF16)
    q2 = jnp.where(lane >= DIFF_QK, q, 0.0).astype(BF16)
    qs = jnp.concatenate([q1, q2], axis=0)
    row = lax.broadcasted_iota(jnp.int32, (2 * blk, blk), 0)
    col = lax.broadcasted_iota(jnp.int32, (2 * blk, blk), 1)
    causal = col <= jnp.where(row >= blk, row - blk, row)

    def block(j, m, l, acc, mask):
        start = pl.multiple_of(j * blk, blk)
        k = k_ref[pl.ds(start, blk), :]
        v = v_ref[pl.ds(start, blk), :]
        s = _dot_nt(qs, k)
        if mask is not None:
            s = jnp.where(mask, s, NEG_INF)
        m_new = jnp.maximum(m, jnp.max(s, axis=-1, keepdims=True))
        alpha = jnp.exp(m - m_new)
        p = jnp.exp(s - m_new)
        l = alpha * l + jnp.sum(p, axis=-1, keepdims=True)
        acc = alpha * acc + _dot(p.astype(BF16), v)
        return m_new, l, acc

    m = jnp.full((2 * blk, 1), NEG_INF, F32)
    l = jnp.zeros((2 * blk, 1), F32)
    acc = jnp.zeros((2 * blk, HEAD_DIM), F32)
    m, l, acc = block(i, m, l, acc, causal)

    def body(t, state):
        return block(i - 1 - t, *state, None)

    m, l, acc = lax.fori_loop(0, i, body, (m, l, acc))
    o = acc / l
    lq = lam_ref[...]
    lam = (jnp.exp(jnp.sum(lq[0:1] * lq[1:2], axis=-1, keepdims=True))
           - jnp.exp(jnp.sum(lq[2:3] * lq[3:4], axis=-1, keepdims=True)) + lambda_init)
    out = o[:blk] - lam * o[blk:]
    o_ref[...] = (_rms(out, g_ref[...]) * (1.0 - lambda_init)).astype(o_ref.dtype)


def diff_attention(pa, lam_vecs, g, lambda_init, blk=256):
    b, s, _ = pa.shape
    c0 = 3 * N_SB
    return pl.pallas_call(
        functools.partial(_diff_kernel, blk=blk, lambda_init=lambda_init),
        grid=(b, N_DIFF, s // blk),
        in_specs=[pl.BlockSpec((None, blk, HEAD_DIM), lambda b_, h, i: (b_, i, c0 + h)),
                  pl.BlockSpec((None, s, HEAD_DIM), lambda b_, h, i: (b_, 0, c0 + N_DIFF + h)),
                  pl.BlockSpec((None, s, HEAD_DIM), lambda b_, h, i: (b_, 0, c0 + 2 * N_DIFF + h)),
                  pl.BlockSpec((4, DIFF_QK), lambda b_, h, i: (0, 0)),
                  pl.BlockSpec((1, HEAD_DIM), lambda b_, h, i: (0, 0))],
        out_specs=pl.BlockSpec((None, blk, HEAD_DIM), lambda b_, h, i: (b_, i, h)),
        out_shape=jax.ShapeDtypeStruct((b, s, W_DIFF), BF16),
        compiler_params=_params("parallel", "parallel", "arbitrary"),
        name="diff_attention",
    )(pa, pa, pa, lam_vecs, g.reshape(1, HEAD_DIM))


def _dil_kernel(q_ref, k_ref, v_ref, g_ref, o_ref, kp_ref, vp_ref, op_ref, lse_ref, *, seq):
    blk = DIL_SPAN
    scale = HEAD_DIM ** -0.5
    n_blocks = seq // blk
    kp_ref[0:blk, :] = jnp.zeros((blk, HEAD_DIM), BF16)
    vp_ref[0:blk, :] = jnp.zeros((blk, HEAD_DIM), BF16)
    ii = lax.broadcasted_iota(jnp.int32, (blk, 2 * blk), 0)
    jj = lax.broadcasted_iota(jnp.int32, (blk, 2 * blk), 1)
    dist = blk + ii - jj
    band = (dist >= 0) & (dist <= DIL_SPAN)

    for p, r in enumerate(DILATIONS):
        per_seq = n_blocks // r

        def body(n, _, p=p, r=r, per_seq=per_seq):
            c = n // per_seq
            i = n % per_seq
            src = c + r * blk * i
            rows = pl.ds(src, blk, stride=r) if r > 1 else pl.ds(pl.multiple_of(src, blk), blk)
            dst = pl.multiple_of(blk * (i + 1), blk)
            kp_ref[pl.ds(dst, blk), :] = k_ref[rows, :].astype(BF16)
            vp_ref[pl.ds(dst, blk), :] = v_ref[rows, :].astype(BF16)
            q = (q_ref[rows, :] * scale).astype(BF16)
            win = pl.ds(pl.multiple_of(blk * i, blk), 2 * blk)
            s = _dot_nt(q, kp_ref[win, :])
            first_key = jnp.where(i == 0, blk, 0)
            s = jnp.where(band & (jj >= first_key), s, NEG_INF)
            m = jnp.max(s, axis=-1, keepdims=True)
            e = jnp.exp(s - m)
            l = jnp.sum(e, axis=-1, keepdims=True)
            o = _dot(e.astype(BF16), vp_ref[win, :]) / l
            op_ref[p, rows, :] = o
            lse_ref[p, rows, :] = jnp.broadcast_to(m + jnp.log(l), (blk, HEAD_DIM))
            return 0

        lax.fori_loop(0, n_blocks, body, 0)

    g = g_ref[...]

    def merge(n, _):
        rows = pl.ds(pl.multiple_of(n * blk, blk), blk)
        lses = [lse_ref[p, rows, :] for p in range(len(DILATIONS))]
        top = functools.reduce(jnp.maximum, lses)
        ws = [jnp.exp(x - top) for x in lses]
        den = functools.reduce(lambda a, b_: a + b_, ws)
        num = functools.reduce(lambda a, b_: a + b_,
                               [op_ref[p, rows, :] * (ws[p] / den) for p in range(len(DILATIONS))])
        o_ref[rows, :] = _rms(num, g).astype(o_ref.dtype)
        return 0

    lax.fori_loop(0, n_blocks, merge, 0)


def dil_attention(pd, g):
    b, s, _ = pd.shape
    n_pat = len(DILATIONS)
    return pl.pallas_call(
        functools.partial(_dil_kernel, seq=s),
        grid=(b, N_DIL),
        in_specs=[pl.BlockSpec((None, s, HEAD_DIM), lambda b_, h: (b_, 0, h)),
                  pl.BlockSpec((None, s, HEAD_DIM), lambda b_, h: (b_, 0, N_DIL + h)),
                  pl.BlockSpec((None, s, HEAD_DIM), lambda b_, h: (b_, 0, 2 * N_DIL + h)),
                  pl.BlockSpec((1, HEAD_DIM), lambda b_, h: (0, 0))],
        out_specs=pl.BlockSpec((None, s, HEAD_DIM), lambda b_, h: (b_, 0, h)),
        out_shape=jax.ShapeDtypeStruct((b, s, W_DIL), BF16),
        scratch_shapes=[pltpu.VMEM((s + DIL_SPAN, HEAD_DIM), BF16),
                        pltpu.VMEM((s + DIL_SPAN, HEAD_DIM), BF16),
                        pltpu.VMEM((n_pat, s, HEAD_DIM), F32),
                        pltpu.VMEM((n_pat, s, HEAD_DIM), F32)],
        compiler_params=_params("parallel", "arbitrary"),
        name="dil_attention",
    )(pd, pd, pd, g.reshape(1, HEAD_DIM))


_KINDS_A = (ROPE_NONE,) * 3 + (ROPE_DIFF,) * 2 + (ROPE_NONE,)
_KINDS_D = (ROPE_DIL,) * 4 + (ROPE_NONE,) * 2


def mixing(h, tabs, batch, layer, w_in, lam_vecs, g_sb, g_diff, g_dil):
    m, _ = h.shape
    seq = m // batch
    split = 3 * (W_SB + W_DIFF)
    pa = proj(h, w_in[:, :split].astype(BF16), tabs, _KINDS_A, BF16)
    pd = proj(h, w_in[:, split:].astype(BF16), tabs, _KINDS_D, F32)
    pa = pa.reshape(batch, seq, split)
    pd = pd.reshape(batch, seq, 3 * W_DIL)
    lambda_init = 0.8 - 0.6 * math.exp(-0.3 * layer)
    o_sb = sb_attention(pa, g_sb)
    o_df = diff_attention(pa, lam_vecs, g_diff, lambda_init)
    o_dl = dil_attention(pd, g_dil)
    return jnp.concatenate([o_sb, o_df, o_dl], axis=-1).reshape(m, W_SB + W_DIFF + W_DIL)


def kernel(x, positions, norm_mix_g, w_in, lambda_q1, lambda_k1, lambda_q2, lambda_k2, g_sb_out, g_diff_out, g_dil_out, w_out, norm_ffn_g, w_gate, w_up, w_down, norm_final_g):
    batch, seq, d = x.shape
    m = batch * seq
    depth = w_in.shape[0]
    xf = x.reshape(m, d)
    tabs = rope_tables(positions.reshape(m, 1))
    for layer in range(depth):
        h = rmsnorm(xf, norm_mix_g[layer], BF16)
        lam_vecs = jnp.stack([lambda_q1[layer], lambda_k1[layer],
                              lambda_q2[layer], lambda_k2[layer]]).astype(F32)
        mixed = mixing(h, tabs, batch, layer, w_in[layer], lam_vecs,
                       g_sb_out[layer], g_diff_out[layer], g_dil_out[layer])
        xf = matmul_residual(mixed, w_out[layer].astype(BF16), xf)
        h = rmsnorm(xf, norm_ffn_g[layer], BF16)
        xf = ffn(h, w_gate[layer].astype(BF16), w_up[layer].astype(BF16),
                 w_down[layer].astype(BF16), xf)
    return rmsnorm(xf, norm_final_g, F32).reshape(batch, seq, d)
```

```python
import functools
import math

import jax
import jax.numpy as jnp
from jax import lax
from jax.experimental import pallas as pl
from jax.experimental.pallas import tpu as pltpu

F32 = jnp.float32
BF16 = jnp.bfloat16

HEAD_DIM = 128
N_SB = 4
N_DIFF = 4
N_DIL = 8
W_SB = N_SB * HEAD_DIM
W_DIFF = N_DIFF * HEAD_DIM
W_DIL = N_DIL * HEAD_DIM
DIFF_QK = HEAD_DIM // 2
ROPE_THETA = 500000.0
ROPE_FRACTION = 0.25
DILATIONS = (1, 4, 16)
DIL_SPAN = 128
NORM_EPS = 1e-6
LANES = 128
VMEM_LIMIT = 56 * 1024 * 1024

NEG_INF = float("-inf")
LOG2_E = math.log2(math.e)


def _params(*sem):
    return pltpu.CompilerParams(dimension_semantics=sem, vmem_limit_bytes=VMEM_LIMIT)


def _dot(a, b):
    return jnp.dot(a, b, preferred_element_type=F32)


def _dot_nt(a, b):
    return lax.dot_general(a, b, (((1,), (1,)), ((), ())), preferred_element_type=F32)


def _tile(x, n):
    return x if n == 1 else jnp.concatenate([x] * n, axis=1)


def _rms(x, g):
    return x * lax.rsqrt(jnp.mean(x * x, axis=-1, keepdims=True) + NORM_EPS) * g


def _rmsnorm_kernel(x_ref, g_ref, o_ref):
    o_ref[...] = _rms(x_ref[...], g_ref[...]).astype(o_ref.dtype)


def rmsnorm(x, g, out_dtype, tm=256):
    m, d = x.shape
    return pl.pallas_call(
        _rmsnorm_kernel,
        grid=(m // tm,),
        in_specs=[pl.BlockSpec((tm, d), lambda i: (i, 0)),
                  pl.BlockSpec((1, d), lambda i: (0, 0))],
        out_specs=pl.BlockSpec((tm, d), lambda i: (i, 0)),
        out_shape=jax.ShapeDtypeStruct((m, d), out_dtype),
        compiler_params=_params("parallel"),
        name="rmsnorm",
    )(x, g.reshape(1, d))


def _rope_lane_consts():
    lane = jnp.arange(LANES)
    rows = []
    for chunk in (DIFF_QK, HEAD_DIM):
        rot = int(chunk * ROPE_FRACTION)
        half = rot // 2
        inv_freq = ROPE_THETA ** (-jnp.arange(half, dtype=F32) / half)
        pos = lane % chunk
        freq = jnp.where(pos < rot, inv_freq[pos % half], 0.0)
        lo = jnp.where(pos < half, -1.0, 0.0)
        hi = jnp.where((pos >= half) & (pos < rot), 1.0, 0.0)
        rows += [freq, lo, hi]
    rows += [jnp.zeros((LANES,)), jnp.zeros((LANES,))]
    return jnp.stack(rows).astype(F32)


def _rope_table_kernel(pos_ref, c_ref, o_ref):
    p = pos_ref[...].astype(F32)
    for k in range(2):
        ang = p * c_ref[3 * k:3 * k + 1, :]
        s = jnp.sin(ang)
        o_ref[3 * k] = jnp.cos(ang)
        o_ref[3 * k + 1] = s * c_ref[3 * k + 1:3 * k + 2, :]
        o_ref[3 * k + 2] = s * c_ref[3 * k + 2:3 * k + 3, :]


def rope_tables(positions, tm=512):
    m = positions.shape[0]
    return pl.pallas_call(
        _rope_table_kernel,
        grid=(m // tm,),
        in_specs=[pl.BlockSpec((tm, 1), lambda i: (i, 0)),
                  pl.BlockSpec((8, LANES), lambda i: (0, 0))],
        out_specs=pl.BlockSpec((6, tm, LANES), lambda i: (0, i, 0)),
        out_shape=jax.ShapeDtypeStruct((6, m, LANES), F32),
        compiler_params=_params("parallel"),
        name="rope_tables",
    )(positions, _rope_lane_consts())


ROPE_NONE, ROPE_DIFF, ROPE_DIL = 0, 1, 2
_ROPE_SHIFT = {ROPE_DIFF: int(DIFF_QK * ROPE_FRACTION) // 2, ROPE_DIL: int(HEAD_DIM * ROPE_FRACTION) // 2}


def _proj_kernel(a_ref, w_ref, tab_ref, o_ref, *, kinds, tn):
    j = pl.program_id(1)
    acc = _dot(a_ref[...], w_ref[...])

    def is_kind(kind):
        hit = None
        for idx, k in enumerate(kinds):
            if k == kind:
                hit = (j == idx) if hit is None else (hit | (j == idx))
        return hit

    for kind in sorted(set(kinds)):
        def write(kind=kind):
            if kind == ROPE_NONE:
                o_ref[...] = acc.astype(o_ref.dtype)
                return
            base = 3 * (kind - 1)
            shift = _ROPE_SHIFT[kind]
            c, lo, hi = tab_ref[base], tab_ref[base + 1], tab_ref[base + 2]
            for q in range(tn // LANES):
                xk = acc[:, q * LANES:(q + 1) * LANES]
                rot = (xk * c + pltpu.roll(xk, LANES - shift, 1) * lo
                       + pltpu.roll(xk, shift, 1) * hi)
                o_ref[:, q * LANES:(q + 1) * LANES] = rot.astype(o_ref.dtype)
        if len(set(kinds)) == 1:
            write()
        else:
            pl.when(is_kind(kind))(write)


def proj(a, w, tabs, kinds, out_dtype, tm=512, tn=512):
    m, k = a.shape
    n = w.shape[1]
    assert len(kinds) == n // tn
    return pl.pallas_call(
        functools.partial(_proj_kernel, kinds=tuple(kinds), tn=tn),
        grid=(m // tm, n // tn),
        in_specs=[pl.BlockSpec((tm, k), lambda i, j: (i, 0)),
                  pl.BlockSpec((k, tn), lambda i, j: (0, j)),
                  pl.BlockSpec((6, tm, LANES), lambda i, j: (0, i, 0))],
        out_specs=pl.BlockSpec((tm, tn), lambda i, j: (i, j)),
        out_shape=jax.ShapeDtypeStruct((m, n), out_dtype),
        compiler_params=_params("parallel", "arbitrary"),
        name="proj",
    )(a, w, tabs)


def _matmul_res_kernel(a_ref, w_ref, r_ref, o_ref):
    o_ref[...] = r_ref[...] + _dot(a_ref[...], w_ref[...])


def matmul_residual(a, w, res, tm=512, tn=512):
    m, k = a.shape
    n = w.shape[1]
    return pl.pallas_call(
        _matmul_res_kernel,
        grid=(m // tm, n // tn),
        in_specs=[pl.BlockSpec((tm, k), lambda i, j: (i, 0)),
                  pl.BlockSpec((k, tn), lambda i, j: (0, j)),
                  pl.BlockSpec((tm, tn), lambda i, j: (i, j))],
        out_specs=pl.BlockSpec((tm, tn), lambda i, j: (i, j)),
        out_shape=jax.ShapeDtypeStruct((m, n), F32),
        compiler_params=_params("parallel", "arbitrary"),
        name="out_proj",
    )(a, w, res)


def _ffn_kernel(h_ref, wg_ref, wu_ref, wd_ref, x_ref, o_ref, acc_ref):
    j = pl.program_id(1)

    @pl.when(j == 0)
    def _():
        acc_ref[...] = jnp.zeros_like(acc_ref)

    h = h_ref[...]
    gate = _dot(h, wg_ref[...])
    up = _dot(h, wu_ref[...])
    act = (gate * jax.nn.sigmoid(gate)) * up
    acc_ref[...] += _dot(act.astype(BF16), wd_ref[...])

    @pl.when(j == pl.num_programs(1) - 1)
    def _():
        o_ref[...] = x_ref[...] + acc_ref[...]


def ffn(h, wg, wu, wd, x, tm=512, tf=512):
    m, d = h.shape
    f = wg.shape[1]
    return pl.pallas_call(
        _ffn_kernel,
        grid=(m // tm, f // tf),
        in_specs=[pl.BlockSpec((tm, d), lambda i, j: (i, 0)),
                  pl.BlockSpec((d, tf), lambda i, j: (0, j)),
                  pl.BlockSpec((d, tf), lambda i, j: (0, j)),
                  pl.BlockSpec((tf, d), lambda i, j: (j, 0)),
                  pl.BlockSpec((tm, d), lambda i, j: (i, 0))],
        out_specs=pl.BlockSpec((tm, d), lambda i, j: (i, 0)),
        out_shape=jax.ShapeDtypeStruct((m, d), F32),
        scratch_shapes=[pltpu.VMEM((tm, d), F32)],
        compiler_params=_params("parallel", "arbitrary"),
        name="ffn",
    )(h, wg, wu, wd, x)


def _causal_sweep(i, n_chain, step, keys_per_iter=1):
    assert n_chain % keys_per_iter == 0
    base = n_chain * i
    for d in range(n_chain - 1, -1, -1):
        _interleave([step(c, base + d, c == d) for c in range(d, n_chain)])

    def body(t, _):
        first = base - 1 - t * keys_per_iter
        _interleave([step(c, first - u, False) for u in range(keys_per_iter) for c in range(n_chain)])
        return 0

    lax.fori_loop(0, (n_chain // keys_per_iter) * i, body, 0)


def _interleave(steps):
    steps = list(steps)
    while steps:
        alive = []
        for g in steps:
            try:
                next(g)
                alive.append(g)
            except StopIteration:
                pass
        steps = alive


def _from_key_matrix(blk):
    idx = jnp.arange(blk)
    tri = (idx[:, None] >= idx[None, :]).astype(BF16)
    return jnp.concatenate([tri, tri], axis=0)


def _sb_kernel(q_ref, k_ref, v_ref, later_ref, g_ref, o_ref, qs_ref, acc_ref, carry_ref, *,
               blk, n_chain, keys_per_iter):
    i = pl.program_id(2)
    scale = HEAD_DIM ** -0.5
    qs_ref[...] = (q_ref[...].astype(F32) * scale).astype(BF16)
    acc_ref[...] = jnp.zeros_like(acc_ref)
    carry_ref[...] = jnp.zeros_like(carry_ref)
    row = lax.broadcasted_iota(jnp.int32, (blk, blk), 0)
    col = lax.broadcasted_iota(jnp.int32, (blk, blk), 1)
    strict = col < row

    def step(c, j, diag):
        rows = slice(c * blk, (c + 1) * blk)
        start = pl.multiple_of(j * blk, blk)
        k = k_ref[pl.ds(start, blk), :]
        v = v_ref[pl.ds(start, blk), :]
        z = _dot_nt(qs_ref[rows, :], k)
        yield
        sp = jnp.maximum(z, 0.0) + jnp.log(1.0 + jnp.exp2(jnp.abs(z) * -LOG2_E))
        spm = jnp.where(strict, sp, 0.0) if diag else sp
        hi = spm.astype(BF16)
        lo = (spm - hi.astype(F32)).astype(BF16)
        from_key = _dot(jnp.concatenate([hi, lo], axis=1), later_ref[...])
        yield
        loga = z - (from_key + _tile(carry_ref[rows, :], blk // LANES))
        if diag:
            loga = jnp.where(strict, loga, NEG_INF)
        pv = _dot(jnp.exp(loga).astype(BF16), v)
        carry_ref[rows, :] += from_key[:, 0:1]
        yield
        acc_ref[rows, :] += pv

    _causal_sweep(i, n_chain, step, keys_per_iter)
    o_ref[...] = _rms(acc_ref[...], g_ref[...]).astype(o_ref.dtype)


def sb_attention(pa, g, blk=256, n_chain=4, keys_per_iter=2):
    b, s, _ = pa.shape
    bq = blk * n_chain
    return pl.pallas_call(
        functools.partial(_sb_kernel, blk=blk, n_chain=n_chain, keys_per_iter=keys_per_iter),
        grid=(b, N_SB, s // bq),
        in_specs=[pl.BlockSpec((None, bq, HEAD_DIM), lambda b_, h, i: (b_, i, h)),
                  pl.BlockSpec((None, s, HEAD_DIM), lambda b_, h, i: (b_, 0, N_SB + h)),
                  pl.BlockSpec((None, s, HEAD_DIM), lambda b_, h, i: (b_, 0, 2 * N_SB + h)),
                  pl.BlockSpec((2 * blk, blk), lambda b_, h, i: (0, 0)),
                  pl.BlockSpec((1, HEAD_DIM), lambda b_, h, i: (0, 0))],
        out_specs=pl.BlockSpec((None, bq, HEAD_DIM), lambda b_, h, i: (b_, i, h)),
        out_shape=jax.ShapeDtypeStruct((b, s, W_SB), BF16),
        scratch_shapes=[pltpu.VMEM((bq, HEAD_DIM), BF16),
                        pltpu.VMEM((bq, HEAD_DIM), F32),
                        pltpu.VMEM((bq, LANES), F32)],
        compiler_params=_params("parallel", "parallel", "arbitrary"),
        name="sb_attention",
    )(pa, pa, pa, _from_key_matrix(blk), g.reshape(1, HEAD_DIM))


def _diff_kernel(q_ref, k_ref, v_ref, lam_ref, g_ref, o_ref, qs_ref, m_ref, l_ref, acc_ref, *,
                 blk, n_chain, lambda_init):
    i = pl.program_id(2)
    scale = DIFF_QK ** -0.5
    q = q_ref[...].astype(F32) * scale
    lc = lax.broadcasted_iota(jnp.int32, (blk, HEAD_DIM), 1)
    for c in range(n_chain):
        qc = q[c * blk:(c + 1) * blk]
        qs_ref[(2 * c) * blk:(2 * c + 1) * blk, :] = jnp.where(lc < DIFF_QK, qc, 0.0).astype(BF16)
        qs_ref[(2 * c + 1) * blk:(2 * c + 2) * blk, :] = jnp.where(lc >= DIFF_QK, qc, 0.0).astype(BF16)
    m_ref[...] = jnp.full(m_ref.shape, NEG_INF, F32)
    l_ref[...] = jnp.zeros_like(l_ref)
    acc_ref[...] = jnp.zeros_like(acc_ref)
    row = lax.broadcasted_iota(jnp.int32, (blk, blk), 0)
    col = lax.broadcasted_iota(jnp.int32, (blk, blk), 1)
    causal = col <= row

    def step(c, j, diag):
        start = pl.multiple_of(j * blk, blk)
        k = k_ref[pl.ds(start, blk), :]
        v = v_ref[pl.ds(start, blk), :]
        rows = [slice((2 * c + e) * blk, (2 * c + e + 1) * blk) for e in range(2)]
        s = [_dot_nt(qs_ref[r, :], k) for r in rows]
        yield
        alpha, pv = [], []
        for e, r in enumerate(rows):
            se = jnp.where(causal, s[e], NEG_INF) if diag else s[e]
            m_prev = m_ref[r, :]
            m_new = jnp.maximum(m_prev, jnp.max(se, axis=-1, keepdims=True))
            alpha.append(jnp.exp(m_prev - m_new))
            p = jnp.exp(se - _tile(m_new, blk // LANES))
            l_ref[r, :] = alpha[e] * l_ref[r, :] + jnp.sum(p, axis=-1, keepdims=True)
            m_ref[r, :] = m_new
            pv.append(_dot(p.astype(BF16), v))
        yield
        for e, r in enumerate(rows):
            acc_ref[r, :] = alpha[e] * acc_ref[r, :] + pv[e]

    _causal_sweep(i, n_chain, step)
    lq = lam_ref[...]
    lam = (jnp.exp(jnp.sum(lq[0:1] * lq[1:2], axis=-1, keepdims=True))
           - jnp.exp(jnp.sum(lq[2:3] * lq[3:4], axis=-1, keepdims=True)) + lambda_init)
    g = g_ref[...]
    for c in range(n_chain):
        o = [acc_ref[(2 * c + e) * blk:(2 * c + e + 1) * blk, :] / l_ref[(2 * c + e) * blk:(2 * c + e + 1) * blk, :]
             for e in range(2)]
        out = o[0] - lam * o[1]
        o_ref[c * blk:(c + 1) * blk, :] = (_rms(out, g) * (1.0 - lambda_init)).astype(o_ref.dtype)


def diff_attention(pa, lam_vecs, g, lambda_init, blk=256, n_chain=4):
    b, s, _ = pa.shape
    c0 = 3 * N_SB
    bq = blk * n_chain
    return pl.pallas_call(
        functools.partial(_diff_kernel, blk=blk, n_chain=n_chain, lambda_init=lambda_init),
        grid=(b, N_DIFF, s // bq),
        in_specs=[pl.BlockSpec((None, bq, HEAD_DIM), lambda b_, h, i: (b_, i, c0 + h)),
                  pl.BlockSpec((None, s, HEAD_DIM), lambda b_, h, i: (b_, 0, c0 + N_DIFF + h)),
                  pl.BlockSpec((None, s, HEAD_DIM), lambda b_, h, i: (b_, 0, c0 + 2 * N_DIFF + h)),
                  pl.BlockSpec((4, DIFF_QK), lambda b_, h, i: (0, 0)),
                  pl.BlockSpec((1, HEAD_DIM), lambda b_, h, i: (0, 0))],
        out_specs=pl.BlockSpec((None, bq, HEAD_DIM), lambda b_, h, i: (b_, i, h)),
        out_shape=jax.ShapeDtypeStruct((b, s, W_DIFF), BF16),
        scratch_shapes=[pltpu.VMEM((2 * bq, HEAD_DIM), BF16),
                        pltpu.VMEM((2 * bq, LANES), F32),
                        pltpu.VMEM((2 * bq, LANES), F32),
                        pltpu.VMEM((2 * bq, HEAD_DIM), F32)],
        compiler_params=_params("parallel", "parallel", "arbitrary"),
        name="diff_attention",
    )(pa, pa, pa, lam_vecs, g.reshape(1, HEAD_DIM))


def _dil_kernel(q_ref, k_ref, v_ref, g_ref, o_ref, qd_ref, kd_ref, vd_ref, op_ref, lse_ref, *,
                seq, unroll):
    blk = DIL_SPAN
    scale = HEAD_DIM ** -0.5
    n_blocks = seq // blk
    kd_ref[0:blk, :] = jnp.zeros((blk, HEAD_DIM), BF16)
    vd_ref[0:blk, :] = jnp.zeros((blk, HEAD_DIM), BF16)
    ii = lax.broadcasted_iota(jnp.int32, (blk, 2 * blk), 0)
    jj = lax.broadcasted_iota(jnp.int32, (blk, 2 * blk), 1)
    dist = blk + ii - jj

    for p, r in enumerate(DILATIONS):
        per_seq = n_blocks // r
        shift = per_seq.bit_length() - 1

        def natural_rows(n, r=r, per_seq=per_seq, shift=shift):
            c = lax.shift_right_logical(n, shift)
            i = n & (per_seq - 1)
            src = c + r * blk * i
            if r == 1:
                return i, pl.ds(pl.multiple_of(src, blk), blk)
            return i, pl.ds(src, blk, stride=r)

        def gather(n, _, natural_rows=natural_rows):
            _, rows = natural_rows(n)
            dst = pl.ds(pl.multiple_of(blk * (n + 1), blk), blk)
            kd_ref[dst, :] = k_ref[rows, :].astype(BF16)
            vd_ref[dst, :] = v_ref[rows, :].astype(BF16)
            qd_ref[pl.ds(pl.multiple_of(blk * n, blk), blk), :] = (q_ref[rows, :] * scale).astype(BF16)
            return 0

        lax.fori_loop(0, n_blocks, gather, 0, unroll=unroll)

        def attend(n, _, p=p, natural_rows=natural_rows):
            i, rows = natural_rows(n)
            q = qd_ref[pl.ds(pl.multiple_of(blk * n, blk), blk), :]
            win = pl.ds(pl.multiple_of(blk * n, blk), 2 * blk)
            s = _dot_nt(q, kd_ref[win, :])
            reach = jnp.minimum(ii + jnp.where(i == 0, 0, DIL_SPAN), DIL_SPAN)
            s = jnp.where((dist >= 0) & (dist <= reach), s, NEG_INF)
            m = jnp.max(s, axis=-1, keepdims=True)
            e = jnp.exp(s - m)
            l = jnp.sum(e, axis=-1, keepdims=True)
            o = _dot(e.astype(BF16), vd_ref[win, :]) / l
            op_ref[p, rows, :] = o
            lse_ref[p, rows, :] = jnp.broadcast_to(m + jnp.log(l), (blk, HEAD_DIM))
            return 0

        lax.fori_loop(0, n_blocks, attend, 0, unroll=unroll)

    g = g_ref[...]

    def merge(n, _):
        rows = pl.ds(pl.multiple_of(n * blk, blk), blk)
        lses = [lse_ref[p, rows, :] for p in range(len(DILATIONS))]
        top = functools.reduce(jnp.maximum, lses)
        ws = [jnp.exp(x - top) for x in lses]
        den = functools.reduce(lambda a, b_: a + b_, ws)
        num = functools.reduce(lambda a, b_: a + b_,
                               [op_ref[p, rows, :] * (ws[p] / den) for p in range(len(DILATIONS))])
        o_ref[rows, :] = _rms(num, g).astype(o_ref.dtype)
        return 0

    lax.fori_loop(0, n_blocks, merge, 0, unroll=unroll)


def dil_attention(pd, g, unroll=4):
    b, s, _ = pd.shape
    n_pat = len(DILATIONS)
    return pl.pallas_call(
        functools.partial(_dil_kernel, seq=s, unroll=unroll),
        grid=(b, N_DIL),
        in_specs=[pl.BlockSpec((None, s, HEAD_DIM), lambda b_, h: (b_, 0, h)),
                  pl.BlockSpec((None, s, HEAD_DIM), lambda b_, h: (b_, 0, N_DIL + h)),
                  pl.BlockSpec((None, s, HEAD_DIM), lambda b_, h: (b_, 0, 2 * N_DIL + h)),
                  pl.BlockSpec((1, HEAD_DIM), lambda b_, h: (0, 0))],
        out_specs=pl.BlockSpec((None, s, HEAD_DIM), lambda b_, h: (b_, 0, h)),
        out_shape=jax.ShapeDtypeStruct((b, s, W_DIL), BF16),
        scratch_shapes=[pltpu.VMEM((s, HEAD_DIM), BF16),
                        pltpu.VMEM((s + DIL_SPAN, HEAD_DIM), BF16),
                        pltpu.VMEM((s + DIL_SPAN, HEAD_DIM), BF16),
                        pltpu.VMEM((n_pat, s, HEAD_DIM), F32),
                        pltpu.VMEM((n_pat, s, HEAD_DIM), F32)],
        compiler_params=_params("parallel", "arbitrary"),
        name="dil_attention",
    )(pd, pd, pd, g.reshape(1, HEAD_DIM))


_KINDS_A = (ROPE_NONE,) * 3 + (ROPE_DIFF,) * 2 + (ROPE_NONE,)
_KINDS_D = (ROPE_DIL,) * 4 + (ROPE_NONE,) * 2


def mixing(h, tabs, batch, layer, w_in, lam_vecs, g_sb, g_diff, g_dil):
    m, _ = h.shape
    seq = m // batch
    split = 3 * (W_SB + W_DIFF)
    pa = proj(h, w_in[:, :split].astype(BF16), tabs, _KINDS_A, BF16)
    pd = proj(h, w_in[:, split:].astype(BF16), tabs, _KINDS_D, F32)
    pa = pa.reshape(batch, seq, split)
    pd = pd.reshape(batch, seq, 3 * W_DIL)
    lambda_init = 0.8 - 0.6 * math.exp(-0.3 * layer)
    o_sb = sb_attention(pa, g_sb)
    o_df = diff_attention(pa, lam_vecs, g_diff, lambda_init)
    o_dl = dil_attention(pd, g_dil)
    return jnp.concatenate([o_sb, o_df, o_dl], axis=-1).reshape(m, W_SB + W_DIFF + W_DIL)


def kernel(x, positions, norm_mix_g, w_in, lambda_q1, lambda_k1, lambda_q2, lambda_k2, g_sb_out, g_diff_out, g_dil_out, w_out, norm_ffn_g, w_gate, w_up, w_down, norm_final_g):
    batch, seq, d = x.shape
    m = batch * seq
    depth = w_in.shape[0]
    xf = x.reshape(m, d)
    tabs = rope_tables(positions.reshape(m, 1))
    for layer in range(depth):
        h = rmsnorm(xf, norm_mix_g[layer], BF16)
        lam_vecs = jnp.stack([lambda_q1[layer], lambda_k1[layer],
                              lambda_q2[layer], lambda_k2[layer]]).astype(F32)
        mixed = mixing(h, tabs, batch, layer, w_in[layer], lam_vecs,
                       g_sb_out[layer], g_diff_out[layer], g_dil_out[layer])
        xf = matmul_residual(mixed, w_out[layer].astype(BF16), xf)
        h = rmsnorm(xf, norm_ffn_g[layer], BF16)
        xf = ffn(h, w_gate[layer].astype(BF16), w_up[layer].astype(BF16),
                 w_down[layer].astype(BF16), xf)
    return rmsnorm(xf, norm_final_g, F32).reshape(batch, seq, d)
```

```python
import functools
import math

import jax
import jax.numpy as jnp
from jax import lax
from jax.experimental import pallas as pl
from jax.experimental.pallas import tpu as pltpu

F32 = jnp.float32
BF16 = jnp.bfloat16

HEAD_DIM = 128
N_SB = 4
N_DIFF = 4
N_DIL = 8
W_SB = N_SB * HEAD_DIM
W_DIFF = N_DIFF * HEAD_DIM
W_DIL = N_DIL * HEAD_DIM
DIFF_QK = HEAD_DIM // 2
ROPE_THETA = 500000.0
ROPE_FRACTION = 0.25
DILATIONS = (1, 4, 16)
DIL_SPAN = 128
NORM_EPS = 1e-6
LANES = 128
VMEM_LIMIT = 56 * 1024 * 1024

NEG_INF = float("-inf")
LOG2_E = math.log2(math.e)


def _params(*sem):
    return pltpu.CompilerParams(dimension_semantics=sem, vmem_limit_bytes=VMEM_LIMIT)


def _dot(a, b):
    return jnp.dot(a, b, preferred_element_type=F32)


def _dot_nt(a, b):
    return lax.dot_general(a, b, (((1,), (1,)), ((), ())), preferred_element_type=F32)


def _tile(x, n):
    return x if n == 1 else jnp.concatenate([x] * n, axis=1)


def _rms(x, g):
    return x * lax.rsqrt(jnp.mean(x * x, axis=-1, keepdims=True) + NORM_EPS) * g


def _rope_lane_consts():
    lane = jnp.arange(LANES)
    rows = []
    for chunk in (DIFF_QK, HEAD_DIM):
        rot = int(chunk * ROPE_FRACTION)
        half = rot // 2
        inv_freq = ROPE_THETA ** (-jnp.arange(half, dtype=F32) / half)
        pos = lane % chunk
        freq = jnp.where(pos < rot, inv_freq[pos % half], 0.0)
        lo = jnp.where(pos < half, -1.0, 0.0)
        hi = jnp.where((pos >= half) & (pos < rot), 1.0, 0.0)
        rows += [freq, lo, hi]
    rows += [jnp.zeros((LANES,)), jnp.zeros((LANES,))]
    return jnp.stack(rows).astype(F32)


def _rope_table_kernel(pos_ref, c_ref, o_ref):
    p = pos_ref[...].astype(F32)
    for k in range(2):
        ang = p * c_ref[3 * k:3 * k + 1, :]
        s = jnp.sin(ang)
        o_ref[3 * k] = jnp.cos(ang)
        o_ref[3 * k + 1] = s * c_ref[3 * k + 1:3 * k + 2, :]
        o_ref[3 * k + 2] = s * c_ref[3 * k + 2:3 * k + 3, :]


def rope_tables(positions, tm=512):
    m = positions.shape[0]
    return pl.pallas_call(
        _rope_table_kernel,
        grid=(m // tm,),
        in_specs=[pl.BlockSpec((tm, 1), lambda i: (i, 0)),
                  pl.BlockSpec((8, LANES), lambda i: (0, 0))],
        out_specs=pl.BlockSpec((6, tm, LANES), lambda i: (0, i, 0)),
        out_shape=jax.ShapeDtypeStruct((6, m, LANES), F32),
        compiler_params=_params("parallel"),
        name="rope_tables",
    )(positions, _rope_lane_consts())


ROPE_NONE, ROPE_DIFF, ROPE_DIL = 0, 1, 2
_ROPE_SHIFT = {ROPE_DIFF: int(DIFF_QK * ROPE_FRACTION) // 2, ROPE_DIL: int(HEAD_DIM * ROPE_FRACTION) // 2}
W_A = 3 * (W_SB + W_DIFF)
PROJ_TN = 512
_PROJ_KINDS = (ROPE_NONE,) * 3 + (ROPE_DIFF,) * 2 + (ROPE_NONE,) + (ROPE_DIL,) * 4 + (ROPE_NONE,) * 2


def _resident(shape):
    return pl.BlockSpec(shape, lambda *_: (0,) * len(shape), pipeline_mode=pl.Buffered(1))


def _proj_kernel(x_ref, g_ref, w_ref, tab_ref, oa_ref, od_ref, h_ref):
    h_ref[...] = _rms(x_ref[...], g_ref[...]).astype(BF16)
    tn = PROJ_TN
    for idx, kind in enumerate(_PROJ_KINDS):
        acc = _dot(h_ref[...], w_ref[:, idx * tn:(idx + 1) * tn])
        o_ref, col = (oa_ref, idx * tn) if idx * tn < W_A else (od_ref, idx * tn - W_A)
        if kind == ROPE_NONE:
            o_ref[:, col:col + tn] = acc.astype(o_ref.dtype)
            continue
        base = 3 * (kind - 1)
        shift = _ROPE_SHIFT[kind]
        c, lo, hi = tab_ref[base], tab_ref[base + 1], tab_ref[base + 2]
        for q in range(tn // LANES):
            xk = acc[:, q * LANES:(q + 1) * LANES]
            rot = xk * c + pltpu.roll(xk, LANES - shift, 1) * lo + pltpu.roll(xk, shift, 1) * hi
            o_ref[:, col + q * LANES:col + (q + 1) * LANES] = rot.astype(o_ref.dtype)


def in_proj(x, g, w, tabs, tm=256):
    m, d = x.shape
    n = w.shape[1]
    assert n == len(_PROJ_KINDS) * PROJ_TN
    return pl.pallas_call(
        _proj_kernel,
        grid=(m // tm,),
        in_specs=[pl.BlockSpec((tm, d), lambda i: (i, 0)),
                  _resident((1, d)),
                  _resident((d, n)),
                  pl.BlockSpec((6, tm, LANES), lambda i: (0, i, 0))],
        out_specs=[pl.BlockSpec((tm, W_A), lambda i: (i, 0)),
                   pl.BlockSpec((tm, n - W_A), lambda i: (i, 0))],
        out_shape=[jax.ShapeDtypeStruct((m, W_A), BF16),
                   jax.ShapeDtypeStruct((m, n - W_A), F32)],
        scratch_shapes=[pltpu.VMEM((tm, d), BF16)],
        compiler_params=_params("parallel"),
        name="in_proj",
    )(x, g.reshape(1, d), w, tabs)


def _out_proj_kernel(sb_ref, df_ref, dl_ref, w_ref, x_ref, g_ref, xo_ref, ho_ref):
    mixed = (_dot(sb_ref[...], w_ref[0:W_SB, :])
             + _dot(df_ref[...], w_ref[W_SB:W_SB + W_DIFF, :])
             + _dot(dl_ref[...], w_ref[W_SB + W_DIFF:, :]))
    xn = x_ref[...] + mixed
    xo_ref[...] = xn
    ho_ref[...] = _rms(xn, g_ref[...]).astype(BF16)


def out_proj(o_sb, o_df, o_dl, w, x, g, tm=256):
    m, d = x.shape
    return pl.pallas_call(
        _out_proj_kernel,
        grid=(m // tm,),
        in_specs=[pl.BlockSpec((tm, W_SB), lambda i: (i, 0)),
                  pl.BlockSpec((tm, W_DIFF), lambda i: (i, 0)),
                  pl.BlockSpec((tm, W_DIL), lambda i: (i, 0)),
                  _resident(w.shape),
                  pl.BlockSpec((tm, d), lambda i: (i, 0)),
                  _resident((1, d))],
        out_specs=[pl.BlockSpec((tm, d), lambda i: (i, 0)),
                   pl.BlockSpec((tm, d), lambda i: (i, 0))],
        out_shape=[jax.ShapeDtypeStruct((m, d), F32),
                   jax.ShapeDtypeStruct((m, d), BF16)],
        compiler_params=_params("parallel"),
        name="out_proj",
    )(o_sb, o_df, o_dl, w, x, g.reshape(1, d))


def _ffn_kernel(*refs, final_norm):
    if final_norm:
        h_ref, wg_ref, wu_ref, wd_ref, x_ref, gf_ref, o_ref, acc_ref = refs
    else:
        h_ref, wg_ref, wu_ref, wd_ref, x_ref, o_ref, acc_ref = refs
    j = pl.program_id(1)

    @pl.when(j == 0)
    def _():
        acc_ref[...] = jnp.zeros_like(acc_ref)

    h = h_ref[...]
    gate = _dot(h, wg_ref[...])
    up = _dot(h, wu_ref[...])
    act = (gate * jax.nn.sigmoid(gate)) * up
    acc_ref[...] += _dot(act.astype(BF16), wd_ref[...])

    @pl.when(j == pl.num_programs(1) - 1)
    def _():
        xn = x_ref[...] + acc_ref[...]
        o_ref[...] = _rms(xn, gf_ref[...]) if final_norm else xn


def ffn(h, wg, wu, wd, x, final_g=None, tm=512, tf=512):
    m, d = h.shape
    f = wg.shape[1]
    final_norm = final_g is not None
    in_specs = [pl.BlockSpec((tm, d), lambda i, j: (i, 0)),
                pl.BlockSpec((d, tf), lambda i, j: (0, j)),
                pl.BlockSpec((d, tf), lambda i, j: (0, j)),
                pl.BlockSpec((tf, d), lambda i, j: (j, 0)),
                pl.BlockSpec((tm, d), lambda i, j: (i, 0))]
    args = [h, wg, wu, wd, x]
    if final_norm:
        in_specs.append(pl.BlockSpec((1, d), lambda i, j: (0, 0)))
        args.append(final_g.reshape(1, d))
    return pl.pallas_call(
        functools.partial(_ffn_kernel, final_norm=final_norm),
        grid=(m // tm, f // tf),
        in_specs=in_specs,
        out_specs=pl.BlockSpec((tm, d), lambda i, j: (i, 0)),
        out_shape=jax.ShapeDtypeStruct((m, d), F32),
        scratch_shapes=[pltpu.VMEM((tm, d), F32)],
        compiler_params=_params("parallel", "arbitrary"),
        name="ffn",
    )(*args)


def _causal_sweep(i, n_chain, step, keys_per_iter=1):
    assert n_chain % keys_per_iter == 0
    base = n_chain * i
    for d in range(n_chain - 1, -1, -1):
        _interleave([step(c, base + d, c == d) for c in range(d, n_chain)])

    def body(t, _):
        first = base - 1 - t * keys_per_iter
        _interleave([step(c, first - u, False) for u in range(keys_per_iter) for c in range(n_chain)])
        return 0

    lax.fori_loop(0, (n_chain // keys_per_iter) * i, body, 0)


def _interleave(steps):
    steps = list(steps)
    while steps:
        alive = []
        for g in steps:
            try:
                next(g)
                alive.append(g)
            except StopIteration:
                pass
        steps = alive


def _from_key_matrix(blk):
    idx = jnp.arange(blk)
    tri = (idx[:, None] >= idx[None, :]).astype(BF16)
    return jnp.concatenate([tri, tri], axis=0)


def _sb_kernel(q_ref, k_ref, v_ref, later_ref, g_ref, o_ref, qs_ref, acc_ref, carry_ref, *,
               blk, n_chain, keys_per_iter):
    i = pl.program_id(2)
    scale = HEAD_DIM ** -0.5
    qs_ref[...] = (q_ref[...].astype(F32) * scale).astype(BF16)
    acc_ref[...] = jnp.zeros_like(acc_ref)
    carry_ref[...] = jnp.zeros_like(carry_ref)
    row = lax.broadcasted_iota(jnp.int32, (blk, blk), 0)
    col = lax.broadcasted_iota(jnp.int32, (blk, blk), 1)
    strict = col < row

    def step(c, j, diag):
        rows = slice(c * blk, (c + 1) * blk)
        start = pl.multiple_of(j * blk, blk)
        k = k_ref[pl.ds(start, blk), :]
        v = v_ref[pl.ds(start, blk), :]
        z = _dot_nt(qs_ref[rows, :], k)
        yield
        sp = jnp.maximum(z, 0.0) + jnp.log(1.0 + jnp.exp2(jnp.abs(z) * -LOG2_E))
        spm = jnp.where(strict, sp, 0.0) if diag else sp
        hi = spm.astype(BF16)
        lo = (spm - hi.astype(F32)).astype(BF16)
        from_key = _dot(jnp.concatenate([hi, lo], axis=1), later_ref[...])
        yield
        loga = z - (from_key + _tile(carry_ref[rows, :], blk // LANES))
        if diag:
            loga = jnp.where(strict, loga, NEG_INF)
        pv = _dot(jnp.exp(loga).astype(BF16), v)
        carry_ref[rows, :] += from_key[:, 0:1]
        yield
        acc_ref[rows, :] += pv

    _causal_sweep(i, n_chain, step, keys_per_iter)
    o_ref[...] = _rms(acc_ref[...], g_ref[...]).astype(o_ref.dtype)


def sb_attention(pa, g, blk=256, n_chain=4, keys_per_iter=2):
    b, s, _ = pa.shape
    bq = blk * n_chain
    return pl.pallas_call(
        functools.partial(_sb_kernel, blk=blk, n_chain=n_chain, keys_per_iter=keys_per_iter),
        grid=(b, N_SB, s // bq),
        in_specs=[pl.BlockSpec((None, bq, HEAD_DIM), lambda b_, h, i: (b_, i, h)),
                  pl.BlockSpec((None, s, HEAD_DIM), lambda b_, h, i: (b_, 0, N_SB + h)),
                  pl.BlockSpec((None, s, HEAD_DIM), lambda b_, h, i: (b_, 0, 2 * N_SB + h)),
                  pl.BlockSpec((2 * blk, blk), lambda b_, h, i: (0, 0)),
                  pl.BlockSpec((1, HEAD_DIM), lambda b_, h, i: (0, 0))],
        out_specs=pl.BlockSpec((None, bq, HEAD_DIM), lambda b_, h, i: (b_, i, h)),
        out_shape=jax.ShapeDtypeStruct((b, s, W_SB), BF16),
        scratch_shapes=[pltpu.VMEM((bq, HEAD_DIM), BF16),
                        pltpu.VMEM((bq, HEAD_DIM), F32),
                        pltpu.VMEM((bq, LANES), F32)],
        compiler_params=_params("parallel", "parallel", "arbitrary"),
        name="sb_attention",
    )(pa, pa, pa, _from_key_matrix(blk), g.reshape(1, HEAD_DIM))


def _diff_kernel(q_ref, k_ref, v_ref, lam_ref, g_ref, o_ref, qs_ref, m_ref, l_ref, acc_ref, *,
                 blk, n_chain, lambda_init):
    i = pl.program_id(2)
    scale = DIFF_QK ** -0.5
    q = q_ref[...].astype(F32) * scale
    lc = lax.broadcasted_iota(jnp.int32, (blk, HEAD_DIM), 1)
    for c in range(n_chain):
        qc = q[c * blk:(c + 1) * blk]
        qs_ref[(2 * c) * blk:(2 * c + 1) * blk, :] = jnp.where(lc < DIFF_QK, qc, 0.0).astype(BF16)
        qs_ref[(2 * c + 1) * blk:(2 * c + 2) * blk, :] = jnp.where(lc >= DIFF_QK, qc, 0.0).astype(BF16)
    m_ref[...] = jnp.full(m_ref.shape, NEG_INF, F32)
    l_ref[...] = jnp.zeros_like(l_ref)
    acc_ref[...] = jnp.zeros_like(acc_ref)
    row = lax.broadcasted_iota(jnp.int32, (blk, blk), 0)
    col = lax.broadcasted_iota(jnp.int32, (blk, blk), 1)
    causal = col <= row

    def step(c, j, diag):
        start = pl.multiple_of(j * blk, blk)
        k = k_ref[pl.ds(start, blk), :]
        v = v_ref[pl.ds(start, blk), :]
        rows = [slice((2 * c + e) * blk, (2 * c + e + 1) * blk) for e in range(2)]
        s = [_dot_nt(qs_ref[r, :], k) for r in rows]
        yield
        alpha, pv = [], []
        for e, r in enumerate(rows):
            se = jnp.where(causal, s[e], NEG_INF) if diag else s[e]
            m_prev = m_ref[r, :]
            m_new = jnp.maximum(m_prev, jnp.max(se, axis=-1, keepdims=True))
            alpha.append(jnp.exp(m_prev - m_new))
            p = jnp.exp(se - _tile(m_new, blk // LANES))
            l_ref[r, :] = alpha[e] * l_ref[r, :] + jnp.sum(p, axis=-1, keepdims=True)
            m_ref[r, :] = m_new
            pv.append(_dot(p.astype(BF16), v))
        yield
        for e, r in enumerate(rows):
            acc_ref[r, :] = alpha[e] * acc_ref[r, :] + pv[e]

    _causal_sweep(i, n_chain, step)
    lq = lam_ref[...]
    lam = (jnp.exp(jnp.sum(lq[0:1] * lq[1:2], axis=-1, keepdims=True))
           - jnp.exp(jnp.sum(lq[2:3] * lq[3:4], axis=-1, keepdims=True)) + lambda_init)
    g = g_ref[...]
    for c in range(n_chain):
        o = [acc_ref[(2 * c + e) * blk:(2 * c + e + 1) * blk, :] / l_ref[(2 * c + e) * blk:(2 * c + e + 1) * blk, :]
             for e in range(2)]
        out = o[0] - lam * o[1]
        o_ref[c * blk:(c + 1) * blk, :] = (_rms(out, g) * (1.0 - lambda_init)).astype(o_ref.dtype)


def diff_attention(pa, lam_vecs, g, lambda_init, blk=256, n_chain=4):
    b, s, _ = pa.shape
    c0 = 3 * N_SB
    bq = blk * n_chain
    return pl.pallas_call(
        functools.partial(_diff_kernel, blk=blk, n_chain=n_chain, lambda_init=lambda_init),
        grid=(b, N_DIFF, s // bq),
        in_specs=[pl.BlockSpec((None, bq, HEAD_DIM), lambda b_, h, i: (b_, i, c0 + h)),
                  pl.BlockSpec((None, s, HEAD_DIM), lambda b_, h, i: (b_, 0, c0 + N_DIFF + h)),
                  pl.BlockSpec((None, s, HEAD_DIM), lambda b_, h, i: (b_, 0, c0 + 2 * N_DIFF + h)),
                  pl.BlockSpec((4, DIFF_QK), lambda b_, h, i: (0, 0)),
                  pl.BlockSpec((1, HEAD_DIM), lambda b_, h, i: (0, 0))],
        out_specs=pl.BlockSpec((None, bq, HEAD_DIM), lambda b_, h, i: (b_, i, h)),
        out_shape=jax.ShapeDtypeStruct((b, s, W_DIFF), BF16),
        scratch_shapes=[pltpu.VMEM((2 * bq, HEAD_DIM), BF16),
                        pltpu.VMEM((2 * bq, LANES), F32),
                        pltpu.VMEM((2 * bq, LANES), F32),
                        pltpu.VMEM((2 * bq, HEAD_DIM), F32)],
        compiler_params=_params("parallel", "parallel", "arbitrary"),
        name="diff_attention",
    )(pa, pa, pa, lam_vecs, g.reshape(1, HEAD_DIM))


def _dil_kernel(q_ref, k_ref, v_ref, g_ref, o_ref, qd_ref, kd_ref, vd_ref, op_ref, lse_ref, *,
                seq, unroll):
    blk = DIL_SPAN
    scale = HEAD_DIM ** -0.5
    n_blocks = seq // blk
    kd_ref[0:blk, :] = jnp.zeros((blk, HEAD_DIM), BF16)
    vd_ref[0:blk, :] = jnp.zeros((blk, HEAD_DIM), BF16)
    ii = lax.broadcasted_iota(jnp.int32, (blk, 2 * blk), 0)
    jj = lax.broadcasted_iota(jnp.int32, (blk, 2 * blk), 1)
    dist = blk + ii - jj

    for p, r in enumerate(DILATIONS):
        per_seq = n_blocks // r
        shift = per_seq.bit_length() - 1

        def natural_rows(n, r=r, per_seq=per_seq, shift=shift):
            c = lax.shift_right_logical(n, shift)
            i = n & (per_seq - 1)
            src = c + r * blk * i
            if r == 1:
                return i, pl.ds(pl.multiple_of(src, blk), blk)
            return i, pl.ds(src, blk, stride=r)

        def gather(n, _, natural_rows=natural_rows):
            _, rows = natural_rows(n)
            dst = pl.ds(pl.multiple_of(blk * (n + 1), blk), blk)
            kd_ref[dst, :] = k_ref[rows, :].astype(BF16)
            vd_ref[dst, :] = v_ref[rows, :].astype(BF16)
            qd_ref[pl.ds(pl.multiple_of(blk * n, blk), blk), :] = (q_ref[rows, :] * scale).astype(BF16)
            return 0

        lax.fori_loop(0, n_blocks, gather, 0, unroll=unroll)

        def attend(n, p=p, natural_rows=natural_rows):
            i, rows = natural_rows(n)
            q = qd_ref[pl.ds(pl.multiple_of(blk * n, blk), blk), :]
            win = pl.ds(pl.multiple_of(blk * n, blk), 2 * blk)
            s = _dot_nt(q, kd_ref[win, :])
            yield
            reach = jnp.minimum(ii + jnp.where(i == 0, 0, DIL_SPAN), DIL_SPAN)
            s = jnp.where((dist >= 0) & (dist <= reach), s, NEG_INF)
            m = jnp.max(s, axis=-1, keepdims=True)
            e = jnp.exp(s - m)
            l = jnp.sum(e, axis=-1, keepdims=True)
            pv = _dot(e.astype(BF16), vd_ref[win, :])
            yield
            op_ref[p, rows, :] = pv / l
            lse_ref[p, rows, :] = jnp.broadcast_to(m + jnp.log(l), (blk, HEAD_DIM))

        def attend_group(t, _, attend=attend):
            _interleave([attend(t * unroll + u) for u in range(unroll)])
            return 0

        lax.fori_loop(0, n_blocks // unroll, attend_group, 0)

    g = g_ref[...]

    def merge(n, _):
        rows = pl.ds(pl.multiple_of(n * blk, blk), blk)
        lses = [lse_ref[p, rows, :] for p in range(len(DILATIONS))]
        top = functools.reduce(jnp.maximum, lses)
        ws = [jnp.exp(x - top) for x in lses]
        den = functools.reduce(lambda a, b_: a + b_, ws)
        num = functools.reduce(lambda a, b_: a + b_,
                               [op_ref[p, rows, :] * (ws[p] / den) for p in range(len(DILATIONS))])
        o_ref[rows, :] = _rms(num, g).astype(o_ref.dtype)
        return 0

    lax.fori_loop(0, n_blocks, merge, 0, unroll=unroll)


def dil_attention(pd, g, unroll=4):
    b, s, _ = pd.shape
    n_pat = len(DILATIONS)
    return pl.pallas_call(
        functools.partial(_dil_kernel, seq=s, unroll=unroll),
        grid=(b, N_DIL),
        in_specs=[pl.BlockSpec((None, s, HEAD_DIM), lambda b_, h: (b_, 0, h)),
                  pl.BlockSpec((None, s, HEAD_DIM), lambda b_, h: (b_, 0, N_DIL + h)),
                  pl.BlockSpec((None, s, HEAD_DIM), lambda b_, h: (b_, 0, 2 * N_DIL + h)),
                  pl.BlockSpec((1, HEAD_DIM), lambda b_, h: (0, 0))],
        out_specs=pl.BlockSpec((None, s, HEAD_DIM), lambda b_, h: (b_, 0, h)),
        out_shape=jax.ShapeDtypeStruct((b, s, W_DIL), BF16),
        scratch_shapes=[pltpu.VMEM((s, HEAD_DIM), BF16),
                        pltpu.VMEM((s + DIL_SPAN, HEAD_DIM), BF16),
                        pltpu.VMEM((s + DIL_SPAN, HEAD_DIM), BF16),
                        pltpu.VMEM((n_pat, s, HEAD_DIM), F32),
                        pltpu.VMEM((n_pat, s, HEAD_DIM), F32)],
        compiler_params=_params("parallel", "arbitrary"),
        name="dil_attention",
    )(pd, pd, pd, g.reshape(1, HEAD_DIM))


def kernel(x, positions, norm_mix_g, w_in, lambda_q1, lambda_k1, lambda_q2, lambda_k2, g_sb_out, g_diff_out, g_dil_out, w_out, norm_ffn_g, w_gate, w_up, w_down, norm_final_g):
    batch, seq, d = x.shape
    m = batch * seq
    depth = w_in.shape[0]
    xf = x.reshape(m, d)
    tabs = rope_tables(positions.reshape(m, 1))
    for layer in range(depth):
        pa, pd = in_proj(xf, norm_mix_g[layer], w_in[layer].astype(BF16), tabs)
        pa = pa.reshape(batch, seq, W_A)
        pd = pd.reshape(batch, seq, 3 * W_DIL)
        lambda_init = 0.8 - 0.6 * math.exp(-0.3 * layer)
        lam_vecs = jnp.stack([lambda_q1[layer], lambda_k1[layer],
                              lambda_q2[layer], lambda_k2[layer]]).astype(F32)
        o_sb = sb_attention(pa, g_sb_out[layer]).reshape(m, W_SB)
        o_df = diff_attention(pa, lam_vecs, g_diff_out[layer], lambda_init).reshape(m, W_DIFF)
        o_dl = dil_attention(pd, g_dil_out[layer]).reshape(m, W_DIL)
        xf, h = out_proj(o_sb, o_df, o_dl, w_out[layer].astype(BF16), xf, norm_ffn_g[layer])
        xf = ffn(h, w_gate[layer].astype(BF16), w_up[layer].astype(BF16), w_down[layer].astype(BF16),
                 xf, final_g=norm_final_g if layer == depth - 1 else None)
    return xf.reshape(batch, seq, d)
```

```python
import functools
import math

import jax
import jax.numpy as jnp
from jax import lax
from jax.experimental import pallas as pl
from jax.experimental.pallas import tpu as pltpu

F32 = jnp.float32
BF16 = jnp.bfloat16

HEAD_DIM = 128
N_SB = 4
N_DIFF = 4
N_DIL = 8
W_SB = N_SB * HEAD_DIM
W_DIFF = N_DIFF * HEAD_DIM
W_DIL = N_DIL * HEAD_DIM
DIFF_QK = HEAD_DIM // 2
ROPE_THETA = 500000.0
ROPE_FRACTION = 0.25
DILATIONS = (1, 4, 16)
DIL_SPAN = 128
NORM_EPS = 1e-6
LANES = 128
VMEM_LIMIT = 56 * 1024 * 1024

NEG_INF = float("-inf")
LOG2_E = math.log2(math.e)
SB_DEAD_MASS = 106.0


def _params(*sem):
    return pltpu.CompilerParams(dimension_semantics=sem, vmem_limit_bytes=VMEM_LIMIT)


def _dot(a, b):
    return jnp.dot(a, b, preferred_element_type=F32)


def _dot_nt(a, b):
    return lax.dot_general(a, b, (((1,), (1,)), ((), ())), preferred_element_type=F32)


def _tile(x, n):
    return x if n == 1 else jnp.concatenate([x] * n, axis=1)


def _rms(x, g):
    return x * lax.rsqrt(jnp.mean(x * x, axis=-1, keepdims=True) + NORM_EPS) * g


def _cast_kernel(x_ref, o_ref):
    o_ref[...] = x_ref[...].astype(o_ref.dtype)


def cast_bf16(w, rows=256):
    l, r, c = w.shape
    return pl.pallas_call(
        _cast_kernel,
        grid=(l * r // rows,),
        in_specs=[pl.BlockSpec((rows, c), lambda i: (i, 0))],
        out_specs=pl.BlockSpec((rows, c), lambda i: (i, 0)),
        out_shape=jax.ShapeDtypeStruct((l * r, c), BF16),
        compiler_params=_params("parallel"),
        name="cast_bf16",
    )(w.reshape(l * r, c)).reshape(l, r, c)


def _rope_lane_consts():
    lane = jnp.arange(LANES)
    rows = []
    for chunk in (DIFF_QK, HEAD_DIM):
        rot = int(chunk * ROPE_FRACTION)
        half = rot // 2
        inv_freq = ROPE_THETA ** (-jnp.arange(half, dtype=F32) / half)
        pos = lane % chunk
        freq = jnp.where(pos < rot, inv_freq[pos % half], 0.0)
        lo = jnp.where(pos < half, -1.0, 0.0)
        hi = jnp.where((pos >= half) & (pos < rot), 1.0, 0.0)
        rows += [freq, lo, hi]
    rows += [jnp.zeros((LANES,)), jnp.zeros((LANES,))]
    return jnp.stack(rows).astype(F32)


def _rope_table_kernel(pos_ref, c_ref, o_ref):
    p = pos_ref[...].astype(F32)
    for k in range(2):
        ang = p * c_ref[3 * k:3 * k + 1, :]
        s = jnp.sin(ang)
        o_ref[3 * k] = jnp.cos(ang)
        o_ref[3 * k + 1] = s * c_ref[3 * k + 1:3 * k + 2, :]
        o_ref[3 * k + 2] = s * c_ref[3 * k + 2:3 * k + 3, :]


def rope_tables(positions, tm=512):
    m = positions.shape[0]
    return pl.pallas_call(
        _rope_table_kernel,
        grid=(m // tm,),
        in_specs=[pl.BlockSpec((tm, 1), lambda i: (i, 0)),
                  pl.BlockSpec((8, LANES), lambda i: (0, 0))],
        out_specs=pl.BlockSpec((6, tm, LANES), lambda i: (0, i, 0)),
        out_shape=jax.ShapeDtypeStruct((6, m, LANES), F32),
        compiler_params=_params("parallel"),
        name="rope_tables",
    )(positions, _rope_lane_consts())


ROPE_NONE, ROPE_DIFF, ROPE_DIL = 0, 1, 2
_ROPE_SHIFT = {ROPE_DIFF: int(DIFF_QK * ROPE_FRACTION) // 2, ROPE_DIL: int(HEAD_DIM * ROPE_FRACTION) // 2}
W_A = 3 * (W_SB + W_DIFF)
PROJ_TN = 512
_PROJ_KINDS = (ROPE_NONE,) * 3 + (ROPE_DIFF,) * 2 + (ROPE_NONE,) + (ROPE_DIL,) * 4 + (ROPE_NONE,) * 2


def _resident(shape, layer=None):
    if layer is None:
        return pl.BlockSpec(shape, lambda *_: (0,) * len(shape), pipeline_mode=pl.Buffered(1))
    return pl.BlockSpec((None,) + tuple(shape), lambda *_: (layer,) + (0,) * len(shape),
                        pipeline_mode=pl.Buffered(1))


def _proj_kernel(x_ref, g_ref, w_ref, tab_ref, oa_ref, od_ref, h_ref):
    h_ref[...] = _rms(x_ref[...], g_ref[...]).astype(BF16)
    tn = PROJ_TN
    for idx, kind in enumerate(_PROJ_KINDS):
        acc = _dot(h_ref[...], w_ref[:, idx * tn:(idx + 1) * tn])
        o_ref, col = (oa_ref, idx * tn) if idx * tn < W_A else (od_ref, idx * tn - W_A)
        if kind == ROPE_NONE:
            o_ref[:, col:col + tn] = acc.astype(o_ref.dtype)
            continue
        base = 3 * (kind - 1)
        shift = _ROPE_SHIFT[kind]
        c, lo, hi = tab_ref[base], tab_ref[base + 1], tab_ref[base + 2]
        for q in range(tn // LANES):
            xk = acc[:, q * LANES:(q + 1) * LANES]
            rot = xk * c + pltpu.roll(xk, LANES - shift, 1) * lo + pltpu.roll(xk, shift, 1) * hi
            o_ref[:, col + q * LANES:col + (q + 1) * LANES] = rot.astype(o_ref.dtype)


def in_proj(x, g, w, layer, tabs, tm=256):
    m, d = x.shape
    n = w.shape[2]
    assert n == len(_PROJ_KINDS) * PROJ_TN
    return pl.pallas_call(
        _proj_kernel,
        grid=(m // tm,),
        in_specs=[pl.BlockSpec((tm, d), lambda i: (i, 0)),
                  _resident((1, d)),
                  _resident((d, n), layer),
                  pl.BlockSpec((6, tm, LANES), lambda i: (0, i, 0))],
        out_specs=[pl.BlockSpec((tm, W_A), lambda i: (i, 0)),
                   pl.BlockSpec((tm, n - W_A), lambda i: (i, 0))],
        out_shape=[jax.ShapeDtypeStruct((m, W_A), BF16),
                   jax.ShapeDtypeStruct((m, n - W_A), F32)],
        scratch_shapes=[pltpu.VMEM((tm, d), BF16)],
        compiler_params=_params("parallel"),
        name="in_proj",
    )(x, g.reshape(1, d), w, tabs)


def _out_proj_kernel(sb_ref, df_ref, dl_ref, w_ref, x_ref, g_ref, xo_ref, ho_ref):
    mixed = (_dot(sb_ref[...], w_ref[0:W_SB, :])
             + _dot(df_ref[...], w_ref[W_SB:W_SB + W_DIFF, :])
             + _dot(dl_ref[...], w_ref[W_SB + W_DIFF:, :]))
    xn = x_ref[...] + mixed
    xo_ref[...] = xn
    ho_ref[...] = _rms(xn, g_ref[...]).astype(BF16)


def out_proj(o_sb, o_df, o_dl, w, layer, x, g, tm=256):
    m, d = x.shape
    return pl.pallas_call(
        _out_proj_kernel,
        grid=(m // tm,),
        in_specs=[pl.BlockSpec((tm, W_SB), lambda i: (i, 0)),
                  pl.BlockSpec((tm, W_DIFF), lambda i: (i, 0)),
                  pl.BlockSpec((tm, W_DIL), lambda i: (i, 0)),
                  _resident(w.shape[1:], layer),
                  pl.BlockSpec((tm, d), lambda i: (i, 0)),
                  _resident((1, d))],
        out_specs=[pl.BlockSpec((tm, d), lambda i: (i, 0)),
                   pl.BlockSpec((tm, d), lambda i: (i, 0))],
        out_shape=[jax.ShapeDtypeStruct((m, d), F32),
                   jax.ShapeDtypeStruct((m, d), BF16)],
        compiler_params=_params("parallel"),
        name="out_proj",
    )(o_sb, o_df, o_dl, w, x, g.reshape(1, d))


def _ffn_kernel(*refs, final_norm):
    if final_norm:
        h_ref, wg_ref, wu_ref, wd_ref, x_ref, gf_ref, o_ref, acc_ref = refs
    else:
        h_ref, wg_ref, wu_ref, wd_ref, x_ref, o_ref, acc_ref = refs
    j = pl.program_id(1)

    @pl.when(j == 0)
    def _():
        acc_ref[...] = jnp.zeros_like(acc_ref)

    h = h_ref[...]
    gate = _dot(h, wg_ref[...])
    up = _dot(h, wu_ref[...])
    act = (gate * jax.nn.sigmoid(gate)) * up
    acc_ref[...] += _dot(act.astype(BF16), wd_ref[...])

    @pl.when(j == pl.num_programs(1) - 1)
    def _():
        xn = x_ref[...] + acc_ref[...]
        o_ref[...] = _rms(xn, gf_ref[...]) if final_norm else xn


def ffn(h, wg, wu, wd, layer, x, final_g=None, tm=512, tf=512):
    m, d = h.shape
    f = wg.shape[2]
    final_norm = final_g is not None
    in_specs = [pl.BlockSpec((tm, d), lambda i, j: (i, 0)),
                pl.BlockSpec((None, d, tf), lambda i, j: (layer, 0, j)),
                pl.BlockSpec((None, d, tf), lambda i, j: (layer, 0, j)),
                pl.BlockSpec((None, tf, d), lambda i, j: (layer, j, 0)),
                pl.BlockSpec((tm, d), lambda i, j: (i, 0))]
    args = [h, wg, wu, wd, x]
    if final_norm:
        in_specs.append(pl.BlockSpec((1, d), lambda i, j: (0, 0)))
        args.append(final_g.reshape(1, d))
    return pl.pallas_call(
        functools.partial(_ffn_kernel, final_norm=final_norm),
        grid=(m // tm, f // tf),
        in_specs=in_specs,
        out_specs=pl.BlockSpec((tm, d), lambda i, j: (i, 0)),
        out_shape=jax.ShapeDtypeStruct((m, d), F32),
        scratch_shapes=[pltpu.VMEM((tm, d), F32)],
        compiler_params=_params("parallel", "arbitrary"),
        name="ffn",
    )(*args)


def _causal_sweep(i, n_chain, step, keys_per_iter=1, exhausted=None):
    assert n_chain % keys_per_iter == 0
    base = n_chain * i
    for d in range(n_chain - 1, -1, -1):
        _interleave([step(c, base + d, c == d) for c in range(d, n_chain)])

    def body(t, _):
        first = base - 1 - t * keys_per_iter
        _interleave([step(c, first - u, False) for u in range(keys_per_iter) for c in range(n_chain)])
        return 0

    trips = (n_chain // keys_per_iter) * i
    if exhausted is None:
        lax.fori_loop(0, trips, body, 0)
        return

    def more(state):
        t, done = state
        return (t < trips) & jnp.logical_not(done)

    def advance(state):
        body(state[0], 0)
        return state[0] + 1, exhausted()

    lax.while_loop(more, advance, (0, exhausted()))


def _interleave(steps):
    steps = list(steps)
    while steps:
        alive = []
        for g in steps:
            try:
                next(g)
                alive.append(g)
            except StopIteration:
                pass
        steps = alive


def _from_key_matrix(blk):
    idx = jnp.arange(blk)
    tri = (idx[:, None] >= idx[None, :]).astype(BF16)
    return jnp.concatenate([tri, tri], axis=0)


def _sb_kernel(q_ref, k_ref, v_ref, later_ref, g_ref, o_ref, qs_ref, acc_ref, carry_ref, *,
               blk, n_chain, keys_per_iter):
    i = pl.program_id(2)
    scale = HEAD_DIM ** -0.5
    qs_ref[...] = (q_ref[...].astype(F32) * scale).astype(BF16)
    acc_ref[...] = jnp.zeros_like(acc_ref)
    carry_ref[...] = jnp.zeros_like(carry_ref)
    row = lax.broadcasted_iota(jnp.int32, (blk, blk), 0)
    col = lax.broadcasted_iota(jnp.int32, (blk, blk), 1)
    strict = col < row

    def step(c, j, diag):
        rows = slice(c * blk, (c + 1) * blk)
        start = pl.multiple_of(j * blk, blk)
        k = k_ref[pl.ds(start, blk), :]
        v = v_ref[pl.ds(start, blk), :]
        z = _dot_nt(qs_ref[rows, :], k)
        yield
        sp = jnp.maximum(z, 0.0) + jnp.log(1.0 + jnp.exp2(jnp.abs(z) * -LOG2_E))
        spm = jnp.where(strict, sp, 0.0) if diag else sp
        hi = spm.astype(BF16)
        lo = (spm - hi.astype(F32)).astype(BF16)
        from_key = _dot(jnp.concatenate([hi, lo], axis=1), later_ref[...])
        yield
        loga = z - (from_key + _tile(carry_ref[rows, :], blk // LANES))
        if diag:
            loga = jnp.where(strict, loga, NEG_INF)
        pv = _dot(jnp.exp(loga).astype(BF16), v)
        carry_ref[rows, :] += from_key[:, 0:1]
        yield
        acc_ref[rows, :] += pv

    _causal_sweep(i, n_chain, step, keys_per_iter,
                  exhausted=lambda: jnp.min(carry_ref[...]) >= SB_DEAD_MASS)
    o_ref[...] = _rms(acc_ref[...], g_ref[...]).astype(o_ref.dtype)


def sb_attention(pa, g, blk=256, n_chain=4, keys_per_iter=2):
    b, s, _ = pa.shape
    bq = blk * n_chain
    return pl.pallas_call(
        functools.partial(_sb_kernel, blk=blk, n_chain=n_chain, keys_per_iter=keys_per_iter),
        grid=(b, N_SB, s // bq),
        in_specs=[pl.BlockSpec((None, bq, HEAD_DIM), lambda b_, h, i: (b_, i, h)),
                  pl.BlockSpec((None, s, HEAD_DIM), lambda b_, h, i: (b_, 0, N_SB + h)),
                  pl.BlockSpec((None, s, HEAD_DIM), lambda b_, h, i: (b_, 0, 2 * N_SB + h)),
                  pl.BlockSpec((2 * blk, blk), lambda b_, h, i: (0, 0)),
                  pl.BlockSpec((1, HEAD_DIM), lambda b_, h, i: (0, 0))],
        out_specs=pl.BlockSpec((None, bq, HEAD_DIM), lambda b_, h, i: (b_, i, h)),
        out_shape=jax.ShapeDtypeStruct((b, s, W_SB), BF16),
        scratch_shapes=[pltpu.VMEM((bq, HEAD_DIM), BF16),
                        pltpu.VMEM((bq, HEAD_DIM), F32),
                        pltpu.VMEM((bq, LANES), F32)],
        compiler_params=_params("parallel", "parallel", "arbitrary"),
        name="sb_attention",
    )(pa, pa, pa, _from_key_matrix(blk), g.reshape(1, HEAD_DIM))


def _diff_kernel(q_ref, k_ref, v_ref, lam_ref, g_ref, o_ref, qs_ref, vx_ref, m_ref, acc_ref, *,
                 blk, n_chain, lambda_init):
    i = pl.program_id(2)
    seq = v_ref.shape[0]

    @pl.when(i == 0)
    def _():
        vx_ref[:, 0:HEAD_DIM] = v_ref[...]
        vx_ref[:, HEAD_DIM:] = jnp.ones((seq, HEAD_DIM), BF16)

    scale = DIFF_QK ** -0.5
    q = q_ref[...].astype(F32) * scale
    lc = lax.broadcasted_iota(jnp.int32, (blk, HEAD_DIM), 1)
    for c in range(n_chain):
        qc = q[c * blk:(c + 1) * blk]
        qs_ref[(2 * c) * blk:(2 * c + 1) * blk, :] = jnp.where(lc < DIFF_QK, qc, 0.0).astype(BF16)
        qs_ref[(2 * c + 1) * blk:(2 * c + 2) * blk, :] = jnp.where(lc >= DIFF_QK, qc, 0.0).astype(BF16)
    m_ref[...] = jnp.full(m_ref.shape, NEG_INF, F32)
    acc_ref[...] = jnp.zeros_like(acc_ref)
    row = lax.broadcasted_iota(jnp.int32, (blk, blk), 0)
    col = lax.broadcasted_iota(jnp.int32, (blk, blk), 1)
    causal = col <= row

    def step(c, j, diag):
        start = pl.multiple_of(j * blk, blk)
        k = k_ref[pl.ds(start, blk), :]
        vx = vx_ref[pl.ds(start, blk), :]
        rows = [slice((2 * c + e) * blk, (2 * c + e + 1) * blk) for e in range(2)]
        s = [_dot_nt(qs_ref[r, :], k) for r in rows]
        yield
        alpha, pv = [], []
        for e, r in enumerate(rows):
            se = jnp.where(causal, s[e], NEG_INF) if diag else s[e]
            m_prev = m_ref[r, :]
            m_new = jnp.maximum(m_prev, jnp.max(se, axis=-1, keepdims=True))
            alpha.append(jnp.exp(m_prev - m_new))
            p = jnp.exp((se - _tile(m_new, blk // LANES)).astype(BF16))
            m_ref[r, :] = m_new
            pv.append(_dot(p, vx))
        yield
        for e, r in enumerate(rows):
            acc_ref[r, :] = _tile(alpha[e], 2) * acc_ref[r, :] + pv[e]

    _causal_sweep(i, n_chain, step)
    lq = lam_ref[...]
    lam = (jnp.exp(jnp.sum(lq[0:1] * lq[1:2], axis=-1, keepdims=True))
           - jnp.exp(jnp.sum(lq[2:3] * lq[3:4], axis=-1, keepdims=True)) + lambda_init)
    g = g_ref[...]
    for c in range(n_chain):
        o = []
        for e in range(2):
            acc = acc_ref[(2 * c + e) * blk:(2 * c + e + 1) * blk, :]
            o.append(acc[:, :HEAD_DIM] / acc[:, HEAD_DIM:])
        out = o[0] - lam * o[1]
        o_ref[c * blk:(c + 1) * blk, :] = (_rms(out, g) * (1.0 - lambda_init)).astype(o_ref.dtype)


def diff_attention(pa, lam_vecs, g, lambda_init, blk=256, n_chain=4):
    b, s, _ = pa.shape
    c0 = 3 * N_SB
    bq = blk * n_chain
    return pl.pallas_call(
        functools.partial(_diff_kernel, blk=blk, n_chain=n_chain, lambda_init=lambda_init),
        grid=(b, N_DIFF, s // bq),
        in_specs=[pl.BlockSpec((None, bq, HEAD_DIM), lambda b_, h, i: (b_, i, c0 + h)),
                  pl.BlockSpec((None, s, HEAD_DIM), lambda b_, h, i: (b_, 0, c0 + N_DIFF + h)),
                  pl.BlockSpec((None, s, HEAD_DIM), lambda b_, h, i: (b_, 0, c0 + 2 * N_DIFF + h)),
                  pl.BlockSpec((4, DIFF_QK), lambda b_, h, i: (0, 0)),
                  pl.BlockSpec((1, HEAD_DIM), lambda b_, h, i: (0, 0))],
        out_specs=pl.BlockSpec((None, bq, HEAD_DIM), lambda b_, h, i: (b_, i, h)),
        out_shape=jax.ShapeDtypeStruct((b, s, W_DIFF), BF16),
        scratch_shapes=[pltpu.VMEM((2 * bq, HEAD_DIM), BF16),
                        pltpu.VMEM((s, 2 * HEAD_DIM), BF16),
                        pltpu.VMEM((2 * bq, LANES), F32),
                        pltpu.VMEM((2 * bq, 2 * HEAD_DIM), F32)],
        compiler_params=_params("parallel", "parallel", "arbitrary"),
        name="diff_attention",
    )(pa, pa, pa, lam_vecs, g.reshape(1, HEAD_DIM))


def _dil_kernel(q_ref, k_ref, v_ref, g_ref, o_ref, qd_ref, kd_ref, vd_ref, q4_ref, k4_ref, v4_ref,
                bias_ref, op_ref, lse_ref, *, seq, unroll):
    blk = DIL_SPAN
    scale = HEAD_DIM ** -0.5
    n_blocks = seq // blk
    kd_ref[0:blk, :] = jnp.zeros((blk, HEAD_DIM), BF16)
    vd_ref[0:blk, 0:HEAD_DIM] = jnp.zeros((blk, HEAD_DIM), BF16)
    vd_ref[:, HEAD_DIM:] = jnp.ones((seq + blk, HEAD_DIM), BF16)
    ii = lax.broadcasted_iota(jnp.int32, (blk, 2 * blk), 0)
    jj = lax.broadcasted_iota(jnp.int32, (blk, 2 * blk), 1)
    dist = blk + ii - jj
    in_window = (dist >= 0) & (dist <= DIL_SPAN)
    bias_ref[0] = jnp.where(in_window & (jj >= blk), 0.0, NEG_INF)
    bias_ref[1] = jnp.where(in_window, 0.0, NEG_INF)
    quarter = seq // 4

    for p, r in enumerate(DILATIONS):
        per_seq = n_blocks // r
        shift = per_seq.bit_length() - 1

        def block_of(n, per_seq=per_seq, shift=shift):
            return lax.shift_right_logical(n, shift), n & (per_seq - 1)

        def natural_rows(n, r=r, block_of=block_of):
            c, i = block_of(n)
            src = c + r * blk * i
            if r == 1:
                return pl.ds(pl.multiple_of(src, blk), blk)
            return pl.ds(src, blk, stride=r)

        def gather(n, _, r=r, block_of=block_of, natural_rows=natural_rows):
            here = pl.ds(pl.multiple_of(blk * n, blk), blk)
            if r == 16:
                c, i = block_of(n)
                rows = pl.ds((c & 3) * quarter + lax.shift_right_logical(c, 2) + 4 * blk * i, blk, stride=4)
                q, k, v = q4_ref[rows, :], k4_ref[rows, :], v4_ref[rows, :]
            else:
                rows = natural_rows(n)
                q, k, v = q_ref[rows, :], k_ref[rows, :], v_ref[rows, :]
            if r == 4:
                q4_ref[here, :], k4_ref[here, :], v4_ref[here, :] = q, k, v
            dst = pl.ds(pl.multiple_of(blk * (n + 1), blk), blk)
            kd_ref[dst, :] = k.astype(BF16)
            vd_ref[dst, 0:HEAD_DIM] = v.astype(BF16)
            qd_ref[here, :] = (q * scale).astype(BF16)
            return 0

        lax.fori_loop(0, n_blocks, gather, 0, unroll=unroll // 2)

        def attend(n, p=p, block_of=block_of, natural_rows=natural_rows):
            _, i = block_of(n)
            rows = natural_rows(n)
            q = qd_ref[pl.ds(pl.multiple_of(blk * n, blk), blk), :]
            win = pl.ds(pl.multiple_of(blk * n, blk), 2 * blk)
            s = _dot_nt(q, kd_ref[win, :])
            yield
            s = s + bias_ref[jnp.where(i == 0, 0, 1)]
            m = jnp.max(s, axis=-1, keepdims=True)
            pv = _dot(jnp.exp((s - m).astype(BF16)), vd_ref[win, :])
            yield
            l = pv[:, HEAD_DIM:]
            op_ref[p, rows, :] = pv[:, :HEAD_DIM] / l
            lse_ref[p, rows, :] = m + jnp.log(l)

        def attend_group(t, _, attend=attend):
            _interleave([attend(t * unroll + u) for u in range(unroll)])
            return 0

        lax.fori_loop(0, n_blocks // unroll, attend_group, 0)

    g = g_ref[...]

    def merge(n, _):
        rows = pl.ds(pl.multiple_of(n * blk, blk), blk)
        lses = [lse_ref[p, rows, :] for p in range(len(DILATIONS))]
        top = functools.reduce(jnp.maximum, lses)
        ws = [jnp.exp(x - top) for x in lses]
        den = functools.reduce(lambda a, b_: a + b_, ws)
        num = functools.reduce(lambda a, b_: a + b_,
                               [op_ref[p, rows, :] * ws[p] for p in range(len(DILATIONS))])
        o_ref[rows, :] = _rms(num * (1.0 / den), g).astype(o_ref.dtype)
        return 0

    lax.fori_loop(0, n_blocks, merge, 0, unroll=unroll)


def dil_attention(pd, g, unroll=8):
    b, s, _ = pd.shape
    n_pat = len(DILATIONS)
    assert DILATIONS == (1, 4, 16)
    return pl.pallas_call(
        functools.partial(_dil_kernel, seq=s, unroll=unroll),
        grid=(b, N_DIL),
        in_specs=[pl.BlockSpec((None, s, HEAD_DIM), lambda b_, h: (b_, 0, h)),
                  pl.BlockSpec((None, s, HEAD_DIM), lambda b_, h: (b_, 0, N_DIL + h)),
                  pl.BlockSpec((None, s, HEAD_DIM), lambda b_, h: (b_, 0, 2 * N_DIL + h)),
                  pl.BlockSpec((1, HEAD_DIM), lambda b_, h: (0, 0))],
        out_specs=pl.BlockSpec((None, s, HEAD_DIM), lambda b_, h: (b_, 0, h)),
        out_shape=jax.ShapeDtypeStruct((b, s, W_DIL), BF16),
        scratch_shapes=[pltpu.VMEM((s, HEAD_DIM), BF16),
                        pltpu.VMEM((s + DIL_SPAN, HEAD_DIM), BF16),
                        pltpu.VMEM((s + DIL_SPAN, 2 * HEAD_DIM), BF16),
                        pltpu.VMEM((s, HEAD_DIM), F32),
                        pltpu.VMEM((s, HEAD_DIM), F32),
                        pltpu.VMEM((s, HEAD_DIM), F32),
                        pltpu.VMEM((2, DIL_SPAN, 2 * DIL_SPAN), F32),
                        pltpu.VMEM((n_pat, s, HEAD_DIM), F32),
                        pltpu.VMEM((n_pat, s, HEAD_DIM), F32)],
        compiler_params=_params("parallel", "arbitrary"),
        name="dil_attention",
    )(pd, pd, pd, g.reshape(1, HEAD_DIM))


def kernel(x, positions, norm_mix_g, w_in, lambda_q1, lambda_k1, lambda_q2, lambda_k2, g_sb_out, g_diff_out, g_dil_out, w_out, norm_ffn_g, w_gate, w_up, w_down, norm_final_g):
    batch, seq, d = x.shape
    m = batch * seq
    depth = w_in.shape[0]
    xf = x.reshape(m, d)
    tabs = rope_tables(positions.reshape(m, 1))
    w_in, w_out, w_gate, w_up, w_down = (cast_bf16(w) for w in (w_in, w_out, w_gate, w_up, w_down))
    for layer in range(depth):
        pa, pd = in_proj(xf, norm_mix_g[layer], w_in, layer, tabs)
        pa = pa.reshape(batch, seq, W_A)
        pd = pd.reshape(batch, seq, 3 * W_DIL)
        lambda_init = 0.8 - 0.6 * math.exp(-0.3 * layer)
        lam_vecs = jnp.stack([lambda_q1[layer], lambda_k1[layer],
                              lambda_q2[layer], lambda_k2[layer]]).astype(F32)
        o_sb = sb_attention(pa, g_sb_out[layer]).reshape(m, W_SB)
        o_df = diff_attention(pa, lam_vecs, g_diff_out[layer], lambda_init).reshape(m, W_DIFF)
        o_dl = dil_attention(pd, g_dil_out[layer]).reshape(m, W_DIL)
        xf, h = out_proj(o_sb, o_df, o_dl, w_out, layer, xf, norm_ffn_g[layer])
        xf = ffn(h, w_gate, w_up, w_down, layer, xf,
                 final_g=norm_final_g if layer == depth - 1 else None)
    return xf.reshape(batch, seq, d)
```

```python
import functools
import math

import jax
import jax.numpy as jnp
from jax import lax
from jax.experimental import pallas as pl
from jax.experimental.pallas import tpu as pltpu

F32 = jnp.float32
BF16 = jnp.bfloat16

HEAD_DIM = 128
N_SB = 4
N_DIFF = 4
N_DIL = 8
W_SB = N_SB * HEAD_DIM
W_DIFF = N_DIFF * HEAD_DIM
W_DIL = N_DIL * HEAD_DIM
DIFF_QK = HEAD_DIM // 2
ROPE_THETA = 500000.0
ROPE_FRACTION = 0.25
DILATIONS = (1, 4, 16)
DIL_SPAN = 128
NORM_EPS = 1e-6
LANES = 128
VMEM_LIMIT = 56 * 1024 * 1024

NEG_INF = float("-inf")
LOG2_E = math.log2(math.e)
SB_DEAD_MASS = 106.0


def _params(*sem):
    return pltpu.CompilerParams(dimension_semantics=sem, vmem_limit_bytes=VMEM_LIMIT)


def _dot(a, b):
    return jnp.dot(a, b, preferred_element_type=F32)


def _dot_nt(a, b):
    return lax.dot_general(a, b, (((1,), (1,)), ((), ())), preferred_element_type=F32)


def _tile(x, n):
    return x if n == 1 else jnp.concatenate([x] * n, axis=1)


def _rms(x, g):
    return x * lax.rsqrt(jnp.mean(x * x, axis=-1, keepdims=True) + NORM_EPS) * g


def _cast_kernel(x_ref, o_ref):
    o_ref[...] = x_ref[...].astype(o_ref.dtype)


def cast_bf16(w, rows=256):
    l, r, c = w.shape
    return pl.pallas_call(
        _cast_kernel,
        grid=(l * r // rows,),
        in_specs=[pl.BlockSpec((rows, c), lambda i: (i, 0))],
        out_specs=pl.BlockSpec((rows, c), lambda i: (i, 0)),
        out_shape=jax.ShapeDtypeStruct((l * r, c), BF16),
        compiler_params=_params("parallel"),
        name="cast_bf16",
    )(w.reshape(l * r, c)).reshape(l, r, c)


def _rope_lane_consts():
    lane = jnp.arange(LANES)
    rows = []
    for chunk in (DIFF_QK, HEAD_DIM):
        rot = int(chunk * ROPE_FRACTION)
        half = rot // 2
        inv_freq = ROPE_THETA ** (-jnp.arange(half, dtype=F32) / half)
        pos = lane % chunk
        freq = jnp.where(pos < rot, inv_freq[pos % half], 0.0)
        lo = jnp.where(pos < half, -1.0, 0.0)
        hi = jnp.where((pos >= half) & (pos < rot), 1.0, 0.0)
        rows += [freq, lo, hi]
    rows += [jnp.zeros((LANES,)), jnp.zeros((LANES,))]
    return jnp.stack(rows).astype(F32)


def _rope_table_kernel(pos_ref, c_ref, o_ref):
    p = pos_ref[...].astype(F32)
    for k in range(2):
        ang = p * c_ref[3 * k:3 * k + 1, :]
        s = jnp.sin(ang)
        o_ref[3 * k] = jnp.cos(ang)
        o_ref[3 * k + 1] = s * c_ref[3 * k + 1:3 * k + 2, :]
        o_ref[3 * k + 2] = s * c_ref[3 * k + 2:3 * k + 3, :]


def rope_tables(positions, tm=512):
    m = positions.shape[0]
    return pl.pallas_call(
        _rope_table_kernel,
        grid=(m // tm,),
        in_specs=[pl.BlockSpec((tm, 1), lambda i: (i, 0)),
                  pl.BlockSpec((8, LANES), lambda i: (0, 0))],
        out_specs=pl.BlockSpec((6, tm, LANES), lambda i: (0, i, 0)),
        out_shape=jax.ShapeDtypeStruct((6, m, LANES), F32),
        compiler_params=_params("parallel"),
        name="rope_tables",
    )(positions, _rope_lane_consts())


ROPE_NONE, ROPE_DIFF, ROPE_DIL = 0, 1, 2
_ROPE_SHIFT = {ROPE_DIFF: int(DIFF_QK * ROPE_FRACTION) // 2, ROPE_DIL: int(HEAD_DIM * ROPE_FRACTION) // 2}
W_A = 3 * (W_SB + W_DIFF)
PROJ_TN = 512
_PROJ_KINDS = (ROPE_NONE,) * 3 + (ROPE_DIFF,) * 2 + (ROPE_NONE,) + (ROPE_DIL,) * 4 + (ROPE_NONE,) * 2


def _resident(shape, layer=None):
    if layer is None:
        return pl.BlockSpec(shape, lambda *_: (0,) * len(shape), pipeline_mode=pl.Buffered(1))
    return pl.BlockSpec((None,) + tuple(shape), lambda *_: (layer,) + (0,) * len(shape),
                        pipeline_mode=pl.Buffered(1))


def _proj_kernel(x_ref, g_ref, w_ref, tab_ref, oa_ref, od_ref, h_ref):
    h_ref[...] = _rms(x_ref[...], g_ref[...]).astype(BF16)
    tn = PROJ_TN
    for idx, kind in enumerate(_PROJ_KINDS):
        acc = _dot(h_ref[...], w_ref[:, idx * tn:(idx + 1) * tn])
        o_ref, col = (oa_ref, idx * tn) if idx * tn < W_A else (od_ref, idx * tn - W_A)
        if kind == ROPE_NONE:
            o_ref[:, col:col + tn] = acc.astype(o_ref.dtype)
            continue
        base = 3 * (kind - 1)
        shift = _ROPE_SHIFT[kind]
        c, lo, hi = tab_ref[base], tab_ref[base + 1], tab_ref[base + 2]
        for q in range(tn // LANES):
            xk = acc[:, q * LANES:(q + 1) * LANES]
            rot = xk * c + pltpu.roll(xk, LANES - shift, 1) * lo + pltpu.roll(xk, shift, 1) * hi
            o_ref[:, col + q * LANES:col + (q + 1) * LANES] = rot.astype(o_ref.dtype)


def in_proj(x, g, w, layer, tabs, tm=256):
    m, d = x.shape
    n = w.shape[2]
    assert n == len(_PROJ_KINDS) * PROJ_TN
    return pl.pallas_call(
        _proj_kernel,
        grid=(m // tm,),
        in_specs=[pl.BlockSpec((tm, d), lambda i: (i, 0)),
                  _resident((1, d)),
                  _resident((d, n), layer),
                  pl.BlockSpec((6, tm, LANES), lambda i: (0, i, 0))],
        out_specs=[pl.BlockSpec((tm, W_A), lambda i: (i, 0)),
                   pl.BlockSpec((tm, n - W_A), lambda i: (i, 0))],
        out_shape=[jax.ShapeDtypeStruct((m, W_A), BF16),
                   jax.ShapeDtypeStruct((m, n - W_A), F32)],
        scratch_shapes=[pltpu.VMEM((tm, d), BF16)],
        compiler_params=_params("parallel"),
        name="in_proj",
    )(x, g.reshape(1, d), w, tabs)


def _out_proj_kernel(sb_ref, df_ref, dl_ref, w_ref, x_ref, g_ref, xo_ref, ho_ref):
    mixed = (_dot(sb_ref[...], w_ref[0:W_SB, :])
             + _dot(df_ref[...], w_ref[W_SB:W_SB + W_DIFF, :])
             + _dot(dl_ref[...], w_ref[W_SB + W_DIFF:, :]))
    xn = x_ref[...] + mixed
    xo_ref[...] = xn
    ho_ref[...] = _rms(xn, g_ref[...]).astype(BF16)


def out_proj(o_sb, o_df, o_dl, w, layer, x, g, tm=256):
    m, d = x.shape
    return pl.pallas_call(
        _out_proj_kernel,
        grid=(m // tm,),
        in_specs=[pl.BlockSpec((tm, W_SB), lambda i: (i, 0)),
                  pl.BlockSpec((tm, W_DIFF), lambda i: (i, 0)),
                  pl.BlockSpec((tm, W_DIL), lambda i: (i, 0)),
                  _resident(w.shape[1:], layer),
                  pl.BlockSpec((tm, d), lambda i: (i, 0)),
                  _resident((1, d))],
        out_specs=[pl.BlockSpec((tm, d), lambda i: (i, 0)),
                   pl.BlockSpec((tm, d), lambda i: (i, 0))],
        out_shape=[jax.ShapeDtypeStruct((m, d), F32),
                   jax.ShapeDtypeStruct((m, d), BF16)],
        compiler_params=_params("parallel"),
        name="out_proj",
    )(o_sb, o_df, o_dl, w, x, g.reshape(1, d))


def _ffn_kernel(*refs, final_norm):
    if final_norm:
        h_ref, wg_ref, wu_ref, wd_ref, x_ref, gf_ref, o_ref, acc_ref = refs
    else:
        h_ref, wg_ref, wu_ref, wd_ref, x_ref, o_ref, acc_ref = refs
    j = pl.program_id(1)

    @pl.when(j == 0)
    def _():
        acc_ref[...] = jnp.zeros_like(acc_ref)

    h = h_ref[...]
    gate = _dot(h, wg_ref[...])
    up = _dot(h, wu_ref[...])
    act = (gate * jax.nn.sigmoid(gate)) * up
    acc_ref[...] += _dot(act.astype(BF16), wd_ref[...])

    @pl.when(j == pl.num_programs(1) - 1)
    def _():
        xn = x_ref[...] + acc_ref[...]
        o_ref[...] = _rms(xn, gf_ref[...]) if final_norm else xn


def ffn(h, wg, wu, wd, layer, x, final_g=None, tm=512, tf=512):
    m, d = h.shape
    f = wg.shape[2]
    final_norm = final_g is not None
    in_specs = [pl.BlockSpec((tm, d), lambda i, j: (i, 0)),
                pl.BlockSpec((None, d, tf), lambda i, j: (layer, 0, j)),
                pl.BlockSpec((None, d, tf), lambda i, j: (layer, 0, j)),
                pl.BlockSpec((None, tf, d), lambda i, j: (layer, j, 0)),
                pl.BlockSpec((tm, d), lambda i, j: (i, 0))]
    args = [h, wg, wu, wd, x]
    if final_norm:
        in_specs.append(pl.BlockSpec((1, d), lambda i, j: (0, 0)))
        args.append(final_g.reshape(1, d))
    return pl.pallas_call(
        functools.partial(_ffn_kernel, final_norm=final_norm),
        grid=(m // tm, f // tf),
        in_specs=in_specs,
        out_specs=pl.BlockSpec((tm, d), lambda i, j: (i, 0)),
        out_shape=jax.ShapeDtypeStruct((m, d), F32),
        scratch_shapes=[pltpu.VMEM((tm, d), F32)],
        compiler_params=_params("parallel", "arbitrary"),
        name="ffn",
    )(*args)


def _causal_sweep(i, n_chain, step, keys_per_iter=1, exhausted=None, wide=False, merge_diagonal=False):
    assert n_chain % keys_per_iter == 0
    base = n_chain * i
    diagonal = [[step(c, base + d, c == d) for c in range(d, n_chain)] for d in range(n_chain - 1, -1, -1)]
    if wide:
        _interleave([step(c, base + c, True) for c in range(n_chain)])
        _interleave([step(c, base, False, c) for c in range(1, n_chain)])
    elif merge_diagonal:
        _interleave(sum(diagonal, []))
    else:
        for group in diagonal:
            _interleave(group)

    def body(t, _):
        first = base - 1 - t * keys_per_iter
        if wide:
            _interleave([step(c, first - (keys_per_iter - 1), False, keys_per_iter) for c in range(n_chain)])
        else:
            _interleave([step(c, first - u, False) for u in range(keys_per_iter) for c in range(n_chain)])
        return 0

    trips = (n_chain // keys_per_iter) * i
    if exhausted is None:
        lax.fori_loop(0, trips, body, 0)
        return

    def more(state):
        t, done = state
        return (t < trips) & jnp.logical_not(done)

    def advance(state):
        body(state[0], 0)
        return state[0] + 1, exhausted()

    lax.while_loop(more, advance, (0, exhausted()))


def _interleave(steps):
    steps = list(steps)
    while steps:
        alive = []
        for g in steps:
            try:
                next(g)
                alive.append(g)
            except StopIteration:
                pass
        steps = alive


def _from_key_matrix(blk):
    idx = jnp.arange(blk)
    tri = (idx[:, None] >= idx[None, :]).astype(BF16)
    return jnp.concatenate([tri, tri], axis=0)


def _sb_kernel(q_ref, k_ref, v_ref, later_ref, g_ref, o_ref, qs_ref, acc_ref, carry_ref, *,
               blk, n_chain, keys_per_iter):
    i = pl.program_id(2)
    scale = HEAD_DIM ** -0.5
    qs_ref[...] = (q_ref[...].astype(F32) * scale).astype(BF16)
    acc_ref[...] = jnp.zeros_like(acc_ref)
    carry_ref[...] = jnp.zeros_like(carry_ref)
    row = lax.broadcasted_iota(jnp.int32, (blk, blk), 0)
    col = lax.broadcasted_iota(jnp.int32, (blk, blk), 1)
    strict = col < row

    def step(c, j, diag):
        rows = slice(c * blk, (c + 1) * blk)
        start = pl.multiple_of(j * blk, blk)
        k = k_ref[pl.ds(start, blk), :]
        v = v_ref[pl.ds(start, blk), :]
        z = _dot_nt(qs_ref[rows, :], k)
        yield
        sp = jnp.maximum(z, 0.0) + jnp.log(1.0 + jnp.exp2(jnp.abs(z) * -LOG2_E))
        spm = jnp.where(strict, sp, 0.0) if diag else sp
        hi = spm.astype(BF16)
        lo = (spm - hi.astype(F32)).astype(BF16)
        from_key = _dot(jnp.concatenate([hi, lo], axis=1), later_ref[...])
        yield
        loga = z - (from_key + _tile(carry_ref[rows, :], blk // LANES))
        if diag:
            loga = jnp.where(strict, loga, NEG_INF)
        pv = _dot(jnp.exp(loga).astype(BF16), v)
        carry_ref[rows, :] += from_key[:, 0:1]
        yield
        acc_ref[rows, :] += pv

    _causal_sweep(i, n_chain, step, keys_per_iter, merge_diagonal=True,
                  exhausted=lambda: jnp.min(carry_ref[...]) >= SB_DEAD_MASS)
    o_ref[...] = _rms(acc_ref[...], g_ref[...]).astype(o_ref.dtype)


def sb_attention(pa, g, blk=256, n_chain=4, keys_per_iter=2):
    b, s, _ = pa.shape
    bq = blk * n_chain
    return pl.pallas_call(
        functools.partial(_sb_kernel, blk=blk, n_chain=n_chain, keys_per_iter=keys_per_iter),
        grid=(b, N_SB, s // bq),
        in_specs=[pl.BlockSpec((None, bq, HEAD_DIM), lambda b_, h, i: (b_, i, h)),
                  pl.BlockSpec((None, s, HEAD_DIM), lambda b_, h, i: (b_, 0, N_SB + h)),
                  pl.BlockSpec((None, s, HEAD_DIM), lambda b_, h, i: (b_, 0, 2 * N_SB + h)),
                  pl.BlockSpec((2 * blk, blk), lambda b_, h, i: (0, 0)),
                  pl.BlockSpec((1, HEAD_DIM), lambda b_, h, i: (0, 0))],
        out_specs=pl.BlockSpec((None, bq, HEAD_DIM), lambda b_, h, i: (b_, i, h)),
        out_shape=jax.ShapeDtypeStruct((b, s, W_SB), BF16),
        scratch_shapes=[pltpu.VMEM((bq, HEAD_DIM), BF16),
                        pltpu.VMEM((bq, HEAD_DIM), F32),
                        pltpu.VMEM((bq, LANES), F32)],
        compiler_params=_params("parallel", "parallel", "arbitrary"),
        name="sb_attention",
    )(pa, pa, pa, _from_key_matrix(blk), g.reshape(1, HEAD_DIM))


def _diff_kernel(q_ref, k_ref, v_ref, lam_ref, g_ref, o_ref, qs_ref, vx_ref, m_ref, acc_ref, *,
                 blk, n_chain, keys_per_iter, lambda_init):
    i = pl.program_id(2)
    seq = v_ref.shape[0]

    @pl.when(i == 0)
    def _():
        vx_ref[:, 0:HEAD_DIM] = v_ref[...]
        vx_ref[:, HEAD_DIM:] = jnp.ones((seq, HEAD_DIM), BF16)

    scale = DIFF_QK ** -0.5
    q = q_ref[...].astype(F32) * scale
    lc = lax.broadcasted_iota(jnp.int32, (blk, HEAD_DIM), 1)
    for c in range(n_chain):
        qc = q[c * blk:(c + 1) * blk]
        qs_ref[(2 * c) * blk:(2 * c + 1) * blk, :] = jnp.where(lc < DIFF_QK, qc, 0.0).astype(BF16)
        qs_ref[(2 * c + 1) * blk:(2 * c + 2) * blk, :] = jnp.where(lc >= DIFF_QK, qc, 0.0).astype(BF16)
    m_ref[...] = jnp.full(m_ref.shape, NEG_INF, F32)
    acc_ref[...] = jnp.zeros_like(acc_ref)
    row = lax.broadcasted_iota(jnp.int32, (blk, blk), 0)
    col = lax.broadcasted_iota(jnp.int32, (blk, blk), 1)
    causal = col <= row

    def step(c, j, diag, width=1):
        start = pl.multiple_of(j * blk, blk)
        k = k_ref[pl.ds(start, width * blk), :]
        vx = vx_ref[pl.ds(start, width * blk), :]
        rows = [slice((2 * c + e) * blk, (2 * c + e + 1) * blk) for e in range(2)]
        s = [_dot_nt(qs_ref[r, :], k) for r in rows]
        yield
        alpha, pv = [], []
        for e, r in enumerate(rows):
            se = jnp.where(causal, s[e], NEG_INF) if diag else s[e]
            m_prev = m_ref[r, :]
            m_new = jnp.maximum(m_prev, jnp.max(se, axis=-1, keepdims=True))
            alpha.append(jnp.exp(m_prev - m_new))
            p = jnp.exp((se - _tile(m_new, width * blk // LANES)).astype(BF16))
            m_ref[r, :] = m_new
            pv.append(_dot(p, vx))
        yield
        for e, r in enumerate(rows):
            acc_ref[r, :] = _tile(alpha[e], 2) * acc_ref[r, :] + pv[e]

    _causal_sweep(i, n_chain, step, keys_per_iter, wide=True)
    lq = lam_ref[...]
    lam = (jnp.exp(jnp.sum(lq[0:1] * lq[1:2], axis=-1, keepdims=True))
           - jnp.exp(jnp.sum(lq[2:3] * lq[3:4], axis=-1, keepdims=True)) + lambda_init)
    g = g_ref[...]
    for c in range(n_chain):
        o = []
        for e in range(2):
            acc = acc_ref[(2 * c + e) * blk:(2 * c + e + 1) * blk, :]
            o.append(acc[:, :HEAD_DIM] / acc[:, HEAD_DIM:])
        out = o[0] - lam * o[1]
        o_ref[c * blk:(c + 1) * blk, :] = (_rms(out, g) * (1.0 - lambda_init)).astype(o_ref.dtype)


def diff_attention(pa, lam_vecs, g, lambda_init, blk=256, n_chain=4, keys_per_iter=4):
    b, s, _ = pa.shape
    c0 = 3 * N_SB
    bq = blk * n_chain
    return pl.pallas_call(
        functools.partial(_diff_kernel, blk=blk, n_chain=n_chain, keys_per_iter=keys_per_iter,
                          lambda_init=lambda_init),
        grid=(b, N_DIFF, s // bq),
        in_specs=[pl.BlockSpec((None, bq, HEAD_DIM), lambda b_, h, i: (b_, i, c0 + h)),
                  pl.BlockSpec((None, s, HEAD_DIM), lambda b_, h, i: (b_, 0, c0 + N_DIFF + h)),
                  pl.BlockSpec((None, s, HEAD_DIM), lambda b_, h, i: (b_, 0, c0 + 2 * N_DIFF + h)),
                  pl.BlockSpec((4, DIFF_QK), lambda b_, h, i: (0, 0)),
                  pl.BlockSpec((1, HEAD_DIM), lambda b_, h, i: (0, 0))],
        out_specs=pl.BlockSpec((None, bq, HEAD_DIM), lambda b_, h, i: (b_, i, h)),
        out_shape=jax.ShapeDtypeStruct((b, s, W_DIFF), BF16),
        scratch_shapes=[pltpu.VMEM((2 * bq, HEAD_DIM), BF16),
                        pltpu.VMEM((s, 2 * HEAD_DIM), BF16),
                        pltpu.VMEM((2 * bq, LANES), F32),
                        pltpu.VMEM((2 * bq, 2 * HEAD_DIM), F32)],
        compiler_params=_params("parallel", "parallel", "arbitrary"),
        name="diff_attention",
    )(pa, pa, pa, lam_vecs, g.reshape(1, HEAD_DIM))


def _dil_kernel(q_ref, k_ref, v_ref, g_ref, o_ref, qd_ref, kd_ref, vd_ref, q4_ref, k4_ref, v4_ref,
                bias_ref, op_ref, lse_ref, *, seq, unroll):
    blk = DIL_SPAN
    scale = HEAD_DIM ** -0.5
    n_blocks = seq // blk
    kd_ref[0:blk, :] = jnp.zeros((blk, HEAD_DIM), BF16)
    vd_ref[0:blk, 0:HEAD_DIM] = jnp.zeros((blk, HEAD_DIM), BF16)
    vd_ref[:, HEAD_DIM:] = jnp.ones((seq + blk, HEAD_DIM), BF16)
    ii = lax.broadcasted_iota(jnp.int32, (blk, 2 * blk), 0)
    jj = lax.broadcasted_iota(jnp.int32, (blk, 2 * blk), 1)
    dist = blk + ii - jj
    in_window = (dist >= 0) & (dist <= DIL_SPAN)
    bias_ref[0] = jnp.where(in_window & (jj >= blk), 0.0, NEG_INF)
    bias_ref[1] = jnp.where(in_window, 0.0, NEG_INF)
    quarter = seq // 4

    for p, r in enumerate(DILATIONS):
        per_seq = n_blocks // r
        shift = per_seq.bit_length() - 1

        def block_of(n, per_seq=per_seq, shift=shift):
            return lax.shift_right_logical(n, shift), n & (per_seq - 1)

        def natural_rows(n, r=r, block_of=block_of):
            c, i = block_of(n)
            src = c + r * blk * i
            if r == 1:
                return pl.ds(pl.multiple_of(src, blk), blk)
            return pl.ds(src, blk, stride=r)

        def gather(n, _, r=r, block_of=block_of, natural_rows=natural_rows):
            here = pl.ds(pl.multiple_of(blk * n, blk), blk)
            if r == 16:
                c, i = block_of(n)
                rows = pl.ds((c & 3) * quarter + lax.shift_right_logical(c, 2) + 4 * blk * i, blk, stride=4)
                q, k, v = q4_ref[rows, :], k4_ref[rows, :], v4_ref[rows, :]
            else:
                rows = natural_rows(n)
                q, k, v = q_ref[rows, :], k_ref[rows, :], v_ref[rows, :]
            if r == 4:
                q4_ref[here, :], k4_ref[here, :], v4_ref[here, :] = q, k, v
            dst = pl.ds(pl.multiple_of(blk * (n + 1), blk), blk)
            kd_ref[dst, :] = k.astype(BF16)
            vd_ref[dst, 0:HEAD_DIM] = v.astype(BF16)
            qd_ref[here, :] = (q * scale).astype(BF16)
            return 0

        lax.fori_loop(0, n_blocks, gather, 0, unroll=unroll // 2)

        def attend(n, p=p, block_of=block_of, natural_rows=natural_rows):
            _, i = block_of(n)
            rows = natural_rows(n)
            q = qd_ref[pl.ds(pl.multiple_of(blk * n, blk), blk), :]
            win = pl.ds(pl.multiple_of(blk * n, blk), 2 * blk)
            s = _dot_nt(q, kd_ref[win, :])
            yield
            s = s + bias_ref[jnp.where(i == 0, 0, 1)]
            m = jnp.max(s, axis=-1, keepdims=True)
            pv = _dot(jnp.exp((s - m).astype(BF16)), vd_ref[win, :])
            yield
            l = pv[:, HEAD_DIM:]
            op_ref[p, rows, :] = pv[:, :HEAD_DIM] / l
            lse_ref[p, rows, :] = m + jnp.log(l)

        def attend_group(t, _, attend=attend):
            _interleave([attend(t * unroll + u) for u in range(unroll)])
            return 0

        lax.fori_loop(0, n_blocks // unroll, attend_group, 0)

    g = g_ref[...]

    def merge(n, _):
        rows = pl.ds(pl.multiple_of(n * blk, blk), blk)
        lses = [lse_ref[p, rows, :] for p in range(len(DILATIONS))]
        top = functools.reduce(jnp.maximum, lses)
        ws = [jnp.exp(x - top) for x in lses]
        den = functools.reduce(lambda a, b_: a + b_, ws)
        num = functools.reduce(lambda a, b_: a + b_,
                               [op_ref[p, rows, :] * ws[p] for p in range(len(DILATIONS))])
        o_ref[rows, :] = _rms(num * (1.0 / den), g).astype(o_ref.dtype)
        return 0

    lax.fori_loop(0, n_blocks, merge, 0, unroll=unroll)


def dil_attention(pd, g, unroll=8):
    b, s, _ = pd.shape
    n_pat = len(DILATIONS)
    assert DILATIONS == (1, 4, 16)
    return pl.pallas_call(
        functools.partial(_dil_kernel, seq=s, unroll=unroll),
        grid=(b, N_DIL),
        in_specs=[pl.BlockSpec((None, s, HEAD_DIM), lambda b_, h: (b_, 0, h)),
                  pl.BlockSpec((None, s, HEAD_DIM), lambda b_, h: (b_, 0, N_DIL + h)),
                  pl.BlockSpec((None, s, HEAD_DIM), lambda b_, h: (b_, 0, 2 * N_DIL + h)),
                  pl.BlockSpec((1, HEAD_DIM), lambda b_, h: (0, 0))],
        out_specs=pl.BlockSpec((None, s, HEAD_DIM), lambda b_, h: (b_, 0, h)),
        out_shape=jax.ShapeDtypeStruct((b, s, W_DIL), BF16),
        scratch_shapes=[pltpu.VMEM((s, HEAD_DIM), BF16),
                        pltpu.VMEM((s + DIL_SPAN, HEAD_DIM), BF16),
                        pltpu.VMEM((s + DIL_SPAN, 2 * HEAD_DIM), BF16),
                        pltpu.VMEM((s, HEAD_DIM), F32),
                        pltpu.VMEM((s, HEAD_DIM), F32),
                        pltpu.VMEM((s, HEAD_DIM), F32),
                        pltpu.VMEM((2, DIL_SPAN, 2 * DIL_SPAN), F32),
                        pltpu.VMEM((n_pat, s, HEAD_DIM), F32),
                        pltpu.VMEM((n_pat, s, HEAD_DIM), F32)],
        compiler_params=_params("parallel", "arbitrary"),
        name="dil_attention",
    )(pd, pd, pd, g.reshape(1, HEAD_DIM))


def kernel(x, positions, norm_mix_g, w_in, lambda_q1, lambda_k1, lambda_q2, lambda_k2, g_sb_out, g_diff_out, g_dil_out, w_out, norm_ffn_g, w_gate, w_up, w_down, norm_final_g):
    batch, seq, d = x.shape
    m = batch * seq
    depth = w_in.shape[0]
    xf = x.reshape(m, d)
    tabs = rope_tables(positions.reshape(m, 1))
    w_in, w_out, w_gate, w_up, w_down = (cast_bf16(w) for w in (w_in, w_out, w_gate, w_up, w_down))
    for layer in range(depth):
        pa, pd = in_proj(xf, norm_mix_g[layer], w_in, layer, tabs)
        pa = pa.reshape(batch, seq, W_A)
        pd = pd.reshape(batch, seq, 3 * W_DIL)
        lambda_init = 0.8 - 0.6 * math.exp(-0.3 * layer)
        lam_vecs = jnp.stack([lambda_q1[layer], lambda_k1[layer],
                              lambda_q2[layer], lambda_k2[layer]]).astype(F32)
        o_sb = sb_attention(pa, g_sb_out[layer]).reshape(m, W_SB)
        o_df = diff_attention(pa, lam_vecs, g_diff_out[layer], lambda_init).reshape(m, W_DIFF)
        o_dl = dil_attention(pd, g_dil_out[layer]).reshape(m, W_DIL)
        xf, h = out_proj(o_sb, o_df, o_dl, w_out, layer, xf, norm_ffn_g[layer])
        xf = ffn(h, w_gate, w_up, w_down, layer, xf,
                 final_g=norm_final_g if layer == depth - 1 else None)
    return xf.reshape(batch, seq, d)
```

```python
import functools
import math

import jax
import jax.numpy as jnp
from jax import lax
from jax.experimental import pallas as pl
from jax.experimental.pallas import tpu as pltpu

F32 = jnp.float32
BF16 = jnp.bfloat16

HEAD_DIM = 128
N_SB = 4
N_DIFF = 4
N_DIL = 8
W_SB = N_SB * HEAD_DIM
W_DIFF = N_DIFF * HEAD_DIM
W_DIL = N_DIL * HEAD_DIM
DIFF_QK = HEAD_DIM // 2
ROPE_THETA = 500000.0
ROPE_FRACTION = 0.25
DILATIONS = (1, 4, 16)
DIL_SPAN = 128
NORM_EPS = 1e-6
LANES = 128
VMEM_LIMIT = 56 * 1024 * 1024

NEG_INF = float("-inf")
LOG2_E = math.log2(math.e)
SB_DEAD_MASS = 106.0


def _params(*sem):
    return pltpu.CompilerParams(dimension_semantics=sem, vmem_limit_bytes=VMEM_LIMIT)


def _dot(a, b):
    return jnp.dot(a, b, preferred_element_type=F32)


def _dot_nt(a, b):
    return lax.dot_general(a, b, (((1,), (1,)), ((), ())), preferred_element_type=F32)


def _tile(x, n):
    return x if n == 1 else jnp.concatenate([x] * n, axis=1)


def _rms(x, g):
    return x * lax.rsqrt(jnp.mean(x * x, axis=-1, keepdims=True) + NORM_EPS) * g


def _cast_kernel(x_ref, o_ref):
    o_ref[...] = x_ref[...].astype(o_ref.dtype)


def cast_bf16(w, rows=256):
    l, r, c = w.shape
    return pl.pallas_call(
        _cast_kernel,
        grid=(l * r // rows,),
        in_specs=[pl.BlockSpec((rows, c), lambda i: (i, 0))],
        out_specs=pl.BlockSpec((rows, c), lambda i: (i, 0)),
        out_shape=jax.ShapeDtypeStruct((l * r, c), BF16),
        compiler_params=_params("parallel"),
        name="cast_bf16",
    )(w.reshape(l * r, c)).reshape(l, r, c)


def _rope_lane_consts():
    lane = jnp.arange(LANES)
    rows = []
    for chunk in (DIFF_QK, HEAD_DIM):
        rot = int(chunk * ROPE_FRACTION)
        half = rot // 2
        inv_freq = ROPE_THETA ** (-jnp.arange(half, dtype=F32) / half)
        pos = lane % chunk
        freq = jnp.where(pos < rot, inv_freq[pos % half], 0.0)
        lo = jnp.where(pos < half, -1.0, 0.0)
        hi = jnp.where((pos >= half) & (pos < rot), 1.0, 0.0)
        rows += [freq, lo, hi]
    rows += [jnp.zeros((LANES,)), jnp.zeros((LANES,))]
    return jnp.stack(rows).astype(F32)


def _rope_table_kernel(pos_ref, c_ref, o_ref):
    p = pos_ref[...].astype(F32)
    for k in range(2):
        ang = p * c_ref[3 * k:3 * k + 1, :]
        s = jnp.sin(ang)
        o_ref[3 * k] = jnp.cos(ang)
        o_ref[3 * k + 1] = s * c_ref[3 * k + 1:3 * k + 2, :]
        o_ref[3 * k + 2] = s * c_ref[3 * k + 2:3 * k + 3, :]


def rope_tables(positions, tm=512):
    m = positions.shape[0]
    return pl.pallas_call(
        _rope_table_kernel,
        grid=(m // tm,),
        in_specs=[pl.BlockSpec((tm, 1), lambda i: (i, 0)),
                  pl.BlockSpec((8, LANES), lambda i: (0, 0))],
        out_specs=pl.BlockSpec((6, tm, LANES), lambda i: (0, i, 0)),
        out_shape=jax.ShapeDtypeStruct((6, m, LANES), F32),
        compiler_params=_params("parallel"),
        name="rope_tables",
    )(positions, _rope_lane_consts())


ROPE_NONE, ROPE_DIFF, ROPE_DIL = 0, 1, 2
_ROPE_SHIFT = {ROPE_DIFF: int(DIFF_QK * ROPE_FRACTION) // 2, ROPE_DIL: int(HEAD_DIM * ROPE_FRACTION) // 2}
W_A = 3 * (W_SB + W_DIFF)
PROJ_TN = 512
_PROJ_KINDS = (ROPE_NONE,) * 3 + (ROPE_DIFF,) * 2 + (ROPE_NONE,) + (ROPE_DIL,) * 4 + (ROPE_NONE,) * 2


def _resident(shape, layer=None):
    if layer is None:
        return pl.BlockSpec(shape, lambda *_: (0,) * len(shape), pipeline_mode=pl.Buffered(1))
    return pl.BlockSpec((None,) + tuple(shape), lambda *_: (layer,) + (0,) * len(shape),
                        pipeline_mode=pl.Buffered(1))


def _proj_kernel(x_ref, g_ref, w_ref, tab_ref, oa_ref, od_ref, h_ref):
    h_ref[...] = _rms(x_ref[...], g_ref[...]).astype(BF16)
    tn = PROJ_TN
    for idx, kind in enumerate(_PROJ_KINDS):
        acc = _dot(h_ref[...], w_ref[:, idx * tn:(idx + 1) * tn])
        o_ref, col = (oa_ref, idx * tn) if idx * tn < W_A else (od_ref, idx * tn - W_A)
        if kind == ROPE_NONE:
            o_ref[:, col:col + tn] = acc.astype(o_ref.dtype)
            continue
        base = 3 * (kind - 1)
        shift = _ROPE_SHIFT[kind]
        c, lo, hi = tab_ref[base], tab_ref[base + 1], tab_ref[base + 2]
        for q in range(tn // LANES):
            xk = acc[:, q * LANES:(q + 1) * LANES]
            rot = xk * c + pltpu.roll(xk, LANES - shift, 1) * lo + pltpu.roll(xk, shift, 1) * hi
            o_ref[:, col + q * LANES:col + (q + 1) * LANES] = rot.astype(o_ref.dtype)


def in_proj(x, g, w, layer, tabs, tm=256):
    m, d = x.shape
    n = w.shape[2]
    assert n == len(_PROJ_KINDS) * PROJ_TN
    return pl.pallas_call(
        _proj_kernel,
        grid=(m // tm,),
        in_specs=[pl.BlockSpec((tm, d), lambda i: (i, 0)),
                  _resident((1, d)),
                  _resident((d, n), layer),
                  pl.BlockSpec((6, tm, LANES), lambda i: (0, i, 0))],
        out_specs=[pl.BlockSpec((tm, W_A), lambda i: (i, 0)),
                   pl.BlockSpec((tm, n - W_A), lambda i: (i, 0))],
        out_shape=[jax.ShapeDtypeStruct((m, W_A), BF16),
                   jax.ShapeDtypeStruct((m, n - W_A), F32)],
        scratch_shapes=[pltpu.VMEM((tm, d), BF16)],
        compiler_params=_params("parallel"),
        name="in_proj",
    )(x, g.reshape(1, d), w, tabs)


def _mlp_kernel(*refs, final_norm):
    if final_norm:
        sb_ref, df_ref, dl_ref, wo_ref, x_ref, g_ref, wg_ref, wu_ref, wd_ref, gf_ref, o_ref, h_ref = refs
    else:
        sb_ref, df_ref, dl_ref, wo_ref, x_ref, g_ref, wg_ref, wu_ref, wd_ref, o_ref, h_ref = refs
    j = pl.program_id(1)

    @pl.when(j == 0)
    def _():
        mixed = (_dot(sb_ref[...], wo_ref[0:W_SB, :])
                 + _dot(df_ref[...], wo_ref[W_SB:W_SB + W_DIFF, :])
                 + _dot(dl_ref[...], wo_ref[W_SB + W_DIFF:, :]))
        xn = x_ref[...] + mixed
        o_ref[...] = xn
        h_ref[...] = _rms(xn, g_ref[...]).astype(BF16)

    h = h_ref[...]
    gate = _dot(h, wg_ref[...])
    up = _dot(h, wu_ref[...])
    act = (gate * jax.nn.sigmoid(gate)) * up
    o_ref[...] += _dot(act.astype(BF16), wd_ref[...])

    if final_norm:
        @pl.when(j == pl.num_programs(1) - 1)
        def _():
            o_ref[...] = _rms(o_ref[...], gf_ref[...])


def mlp_block(o_sb, o_df, o_dl, w_out, x, g, wg, wu, wd, layer, final_g=None, tm=512, tf=512):
    m, d = x.shape
    f = wg.shape[2]
    final_norm = final_g is not None
    in_specs = [pl.BlockSpec((tm, W_SB), lambda i, j: (i, 0)),
                pl.BlockSpec((tm, W_DIFF), lambda i, j: (i, 0)),
                pl.BlockSpec((tm, W_DIL), lambda i, j: (i, 0)),
                _resident(w_out.shape[1:], layer),
                pl.BlockSpec((tm, d), lambda i, j: (i, 0)),
                _resident((1, d)),
                pl.BlockSpec((None, d, tf), lambda i, j: (layer, 0, j)),
                pl.BlockSpec((None, d, tf), lambda i, j: (layer, 0, j)),
                pl.BlockSpec((None, tf, d), lambda i, j: (layer, j, 0))]
    args = [o_sb, o_df, o_dl, w_out, x, g.reshape(1, d), wg, wu, wd]
    if final_norm:
        in_specs.append(_resident((1, d)))
        args.append(final_g.reshape(1, d))
    return pl.pallas_call(
        functools.partial(_mlp_kernel, final_norm=final_norm),
        grid=(m // tm, f // tf),
        in_specs=in_specs,
        out_specs=pl.BlockSpec((tm, d), lambda i, j: (i, 0)),
        out_shape=jax.ShapeDtypeStruct((m, d), F32),
        scratch_shapes=[pltpu.VMEM((tm, d), BF16)],
        compiler_params=_params("parallel", "arbitrary"),
        name="mlp_block",
    )(*args)


def _causal_sweep(i, n_chain, step, keys_per_iter=1, exhausted=None, wide=False, merge_diagonal=False):
    assert n_chain % keys_per_iter == 0
    base = n_chain * i
    diagonal = [[step(c, base + d, c == d) for c in range(d, n_chain)] for d in range(n_chain - 1, -1, -1)]
    if wide:
        _interleave([step(c, base + c, True) for c in range(n_chain)])
        _interleave([step(c, base, False, c) for c in range(1, n_chain)])
    elif merge_diagonal:
        _interleave(sum(diagonal, []))
    else:
        for group in diagonal:
            _interleave(group)

    def body(t, _):
        first = base - 1 - t * keys_per_iter
        if wide:
            _interleave([step(c, first - (keys_per_iter - 1), False, keys_per_iter) for c in range(n_chain)])
        else:
            _interleave([step(c, first - u, False) for u in range(keys_per_iter) for c in range(n_chain)])
        return 0

    trips = (n_chain // keys_per_iter) * i
    if exhausted is None:
        lax.fori_loop(0, trips, body, 0)
        return

    def more(state):
        t, done = state
        return (t < trips) & jnp.logical_not(done)

    def advance(state):
        body(state[0], 0)
        return state[0] + 1, exhausted()

    lax.while_loop(more, advance, (0, exhausted()))


def _interleave(steps):
    steps = list(steps)
    while steps:
        alive = []
        for g in steps:
            try:
                next(g)
                alive.append(g)
            except StopIteration:
                pass
        steps = alive


def _from_key_matrix(blk):
    idx = jnp.arange(blk)
    tri = (idx[:, None] >= idx[None, :]).astype(BF16)
    return jnp.concatenate([tri, tri], axis=0)


def _sb_kernel(q_ref, k_ref, v_ref, later_ref, g_ref, o_ref, qs_ref, acc_ref, carry_ref, *,
               blk, n_chain, keys_per_iter):
    i = pl.program_id(2)
    scale = HEAD_DIM ** -0.5
    qs_ref[...] = (q_ref[...].astype(F32) * scale).astype(BF16)
    acc_ref[...] = jnp.zeros_like(acc_ref)
    carry_ref[...] = jnp.zeros_like(carry_ref)
    row = lax.broadcasted_iota(jnp.int32, (blk, blk), 0)
    col = lax.broadcasted_iota(jnp.int32, (blk, blk), 1)
    strict = col < row

    def step(c, j, diag):
        rows = slice(c * blk, (c + 1) * blk)
        start = pl.multiple_of(j * blk, blk)
        k = k_ref[pl.ds(start, blk), :]
        v = v_ref[pl.ds(start, blk), :]
        z = _dot_nt(qs_ref[rows, :], k)
        yield
        sp = jnp.maximum(z, 0.0) + jnp.log(1.0 + jnp.exp2(jnp.abs(z) * -LOG2_E))
        spm = jnp.where(strict, sp, 0.0) if diag else sp
        hi = spm.astype(BF16)
        lo = (spm - hi.astype(F32)).astype(BF16)
        from_key = _dot(jnp.concatenate([hi, lo], axis=1), later_ref[...])
        yield
        loga = z - (from_key + _tile(carry_ref[rows, :], blk // LANES))
        if diag:
            loga = jnp.where(strict, loga, NEG_INF)
        pv = _dot(jnp.exp(loga).astype(BF16), v)
        carry_ref[rows, :] += from_key[:, 0:1]
        yield
        acc_ref[rows, :] += pv

    _causal_sweep(i, n_chain, step, keys_per_iter, merge_diagonal=True,
                  exhausted=lambda: jnp.min(carry_ref[...]) >= SB_DEAD_MASS)
    o_ref[...] = _rms(acc_ref[...], g_ref[...]).astype(o_ref.dtype)


def sb_attention(pa, g, blk=256, n_chain=4, keys_per_iter=2):
    b, s, _ = pa.shape
    bq = blk * n_chain
    return pl.pallas_call(
        functools.partial(_sb_kernel, blk=blk, n_chain=n_chain, keys_per_iter=keys_per_iter),
        grid=(b, N_SB, s // bq),
        in_specs=[pl.BlockSpec((None, bq, HEAD_DIM), lambda b_, h, i: (b_, i, h)),
                  pl.BlockSpec((None, s, HEAD_DIM), lambda b_, h, i: (b_, 0, N_SB + h)),
                  pl.BlockSpec((None, s, HEAD_DIM), lambda b_, h, i: (b_, 0, 2 * N_SB + h)),
                  pl.BlockSpec((2 * blk, blk), lambda b_, h, i: (0, 0)),
                  pl.BlockSpec((1, HEAD_DIM), lambda b_, h, i: (0, 0))],
        out_specs=pl.BlockSpec((None, bq, HEAD_DIM), lambda b_, h, i: (b_, i, h)),
        out_shape=jax.ShapeDtypeStruct((b, s, W_SB), BF16),
        scratch_shapes=[pltpu.VMEM((bq, HEAD_DIM), BF16),
                        pltpu.VMEM((bq, HEAD_DIM), F32),
                        pltpu.VMEM((bq, LANES), F32)],
        compiler_params=_params("parallel", "parallel", "arbitrary"),
        name="sb_attention",
    )(pa, pa, pa, _from_key_matrix(blk), g.reshape(1, HEAD_DIM))


def _diff_kernel(q_ref, k_ref, v_ref, lam_ref, g_ref, o_ref, qs_ref, vx_ref, m_ref, acc_ref, *,
                 blk, n_chain, keys_per_iter, lambda_init):
    i = pl.program_id(2)
    seq = v_ref.shape[0]

    @pl.when(i == 0)
    def _():
        vx_ref[:, 0:HEAD_DIM] = v_ref[...]
        vx_ref[:, HEAD_DIM:] = jnp.ones((seq, HEAD_DIM), BF16)

    scale = DIFF_QK ** -0.5
    q = q_ref[...].astype(F32) * scale
    lc = lax.broadcasted_iota(jnp.int32, (blk, HEAD_DIM), 1)
    for c in range(n_chain):
        qc = q[c * blk:(c + 1) * blk]
        qs_ref[(2 * c) * blk:(2 * c + 1) * blk, :] = jnp.where(lc < DIFF_QK, qc, 0.0).astype(BF16)
        qs_ref[(2 * c + 1) * blk:(2 * c + 2) * blk, :] = jnp.where(lc >= DIFF_QK, qc, 0.0).astype(BF16)
    m_ref[...] = jnp.full(m_ref.shape, NEG_INF, F32)
    acc_ref[...] = jnp.zeros_like(acc_ref)
    row = lax.broadcasted_iota(jnp.int32, (blk, blk), 0)
    col = lax.broadcasted_iota(jnp.int32, (blk, blk), 1)
    causal = col <= row

    def step(c, j, diag, width=1):
        start = pl.multiple_of(j * blk, blk)
        k = k_ref[pl.ds(start, width * blk), :]
        vx = vx_ref[pl.ds(start, width * blk), :]
        rows = [slice((2 * c + e) * blk, (2 * c + e + 1) * blk) for e in range(2)]
        s = [_dot_nt(qs_ref[r, :], k) for r in rows]
        yield
        alpha, pv = [], []
        for e, r in enumerate(rows):
            se = jnp.where(causal, s[e], NEG_INF) if diag else s[e]
            m_prev = m_ref[r, :]
            m_new = jnp.maximum(m_prev, jnp.max(se, axis=-1, keepdims=True))
            alpha.append(jnp.exp(m_prev - m_new))
            p = jnp.exp((se - _tile(m_new, width * blk // LANES)).astype(BF16))
            m_ref[r, :] = m_new
            pv.append(_dot(p, vx))
        yield
        for e, r in enumerate(rows):
            acc_ref[r, :] = _tile(alpha[e], 2) * acc_ref[r, :] + pv[e]

    _causal_sweep(i, n_chain, step, keys_per_iter, wide=True)
    lq = lam_ref[...]
    lam = (jnp.exp(jnp.sum(lq[0:1] * lq[1:2], axis=-1, keepdims=True))
           - jnp.exp(jnp.sum(lq[2:3] * lq[3:4], axis=-1, keepdims=True)) + lambda_init)
    g = g_ref[...]
    for c in range(n_chain):
        o = []
        for e in range(2):
            acc = acc_ref[(2 * c + e) * blk:(2 * c + e + 1) * blk, :]
            o.append(acc[:, :HEAD_DIM] / acc[:, HEAD_DIM:])
        out = o[0] - lam * o[1]
        o_ref[c * blk:(c + 1) * blk, :] = (_rms(out, g) * (1.0 - lambda_init)).astype(o_ref.dtype)


def diff_attention(pa, lam_vecs, g, lambda_init, blk=256, n_chain=4, keys_per_iter=4):
    b, s, _ = pa.shape
    c0 = 3 * N_SB
    bq = blk * n_chain
    return pl.pallas_call(
        functools.partial(_diff_kernel, blk=blk, n_chain=n_chain, keys_per_iter=keys_per_iter,
                          lambda_init=lambda_init),
        grid=(b, N_DIFF, s // bq),
        in_specs=[pl.BlockSpec((None, bq, HEAD_DIM), lambda b_, h, i: (b_, i, c0 + h)),
                  pl.BlockSpec((None, s, HEAD_DIM), lambda b_, h, i: (b_, 0, c0 + N_DIFF + h)),
                  pl.BlockSpec((None, s, HEAD_DIM), lambda b_, h, i: (b_, 0, c0 + 2 * N_DIFF + h)),
                  pl.BlockSpec((4, DIFF_QK), lambda b_, h, i: (0, 0)),
                  pl.BlockSpec((1, HEAD_DIM), lambda b_, h, i: (0, 0))],
        out_specs=pl.BlockSpec((None, bq, HEAD_DIM), lambda b_, h, i: (b_, i, h)),
        out_shape=jax.ShapeDtypeStruct((b, s, W_DIFF), BF16),
        scratch_shapes=[pltpu.VMEM((2 * bq, HEAD_DIM), BF16),
                        pltpu.VMEM((s, 2 * HEAD_DIM), BF16),
                        pltpu.VMEM((2 * bq, LANES), F32),
                        pltpu.VMEM((2 * bq, 2 * HEAD_DIM), F32)],
        compiler_params=_params("parallel", "parallel", "arbitrary"),
        name="diff_attention",
    )(pa, pa, pa, lam_vecs, g.reshape(1, HEAD_DIM))


def _dil_kernel(q_ref, k_ref, v_ref, g_ref, o_ref, qd_ref, kd_ref, vd_ref, q4_ref, k4_ref, v4_ref,
                bias_ref, op_ref, lse_ref, *, seq, group, unroll):
    blk = DIL_SPAN
    scale = HEAD_DIM ** -0.5
    n_blocks = seq // blk
    kd_ref[0:blk, :] = jnp.zeros((blk, HEAD_DIM), BF16)
    vd_ref[0:blk, 0:HEAD_DIM] = jnp.zeros((blk, HEAD_DIM), BF16)
    vd_ref[:, HEAD_DIM:] = jnp.ones((seq + blk, HEAD_DIM), BF16)
    ii = lax.broadcasted_iota(jnp.int32, (blk, 2 * blk), 0)
    jj = lax.broadcasted_iota(jnp.int32, (blk, 2 * blk), 1)
    dist = blk + ii - jj
    in_window = (dist >= 0) & (dist <= DIL_SPAN)
    bias_ref[0] = jnp.where(in_window & (jj >= blk), 0.0, NEG_INF)
    bias_ref[1] = jnp.where(in_window, 0.0, NEG_INF)
    quarter = seq // 4

    for p, r in enumerate(DILATIONS):
        per_seq = n_blocks // r
        shift = per_seq.bit_length() - 1

        def block_of(n, per_seq=per_seq, shift=shift):
            return lax.shift_right_logical(n, shift), n & (per_seq - 1)

        def natural_rows(n, r=r, block_of=block_of):
            c, i = block_of(n)
            src = c + r * blk * i
            if r == 1:
                return pl.ds(pl.multiple_of(src, blk), blk)
            return pl.ds(src, blk, stride=r)

        def gather(n, _, r=r, block_of=block_of, natural_rows=natural_rows):
            here = pl.ds(pl.multiple_of(blk * n, blk), blk)
            if r == 16:
                c, i = block_of(n)
                rows = pl.ds((c & 3) * quarter + lax.shift_right_logical(c, 2) + 4 * blk * i, blk, stride=4)
                q, k, v = q4_ref[rows, :], k4_ref[rows, :], v4_ref[rows, :]
            else:
                rows = natural_rows(n)
                q, k, v = q_ref[rows, :], k_ref[rows, :], v_ref[rows, :]
            if r == 4:
                q4_ref[here, :], k4_ref[here, :], v4_ref[here, :] = q, k, v
            dst = pl.ds(pl.multiple_of(blk * (n + 1), blk), blk)
            kd_ref[dst, :] = k.astype(BF16)
            vd_ref[dst, 0:HEAD_DIM] = v.astype(BF16)
            qd_ref[here, :] = (q * scale).astype(BF16)
            return 0

        lax.fori_loop(0, n_blocks, gather, 0, unroll=unroll)

        def attend(n, p=p, block_of=block_of, natural_rows=natural_rows):
            _, i = block_of(n)
            rows = natural_rows(n)
            q = qd_ref[pl.ds(pl.multiple_of(blk * n, blk), blk), :]
            win = pl.ds(pl.multiple_of(blk * n, blk), 2 * blk)
            s = _dot_nt(q, kd_ref[win, :])
            yield
            s = s + bias_ref[jnp.where(i == 0, 0, 1)]
            m = jnp.max(s, axis=-1, keepdims=True)
            pv = _dot(jnp.exp((s - m).astype(BF16)), vd_ref[win, :])
            yield
            l = pv[:, HEAD_DIM:]
            op_ref[p, rows, :] = pv[:, :HEAD_DIM] / l
            lse_ref[p, rows, :] = m + jnp.log(l)

        def attend_group(t, _, attend=attend):
            _interleave([attend(t * group + u) for u in range(group)])
            return 0

        lax.fori_loop(0, n_blocks // group, attend_group, 0)

    g = g_ref[...]

    def merge(n, _):
        rows = pl.ds(pl.multiple_of(n * blk, blk), blk)
        lses = [lse_ref[p, rows, :] for p in range(len(DILATIONS))]
        top = functools.reduce(jnp.maximum, lses)
        ws = [jnp.exp(x - top) for x in lses]
        den = functools.reduce(lambda a, b_: a + b_, ws)
        num = functools.reduce(lambda a, b_: a + b_,
                               [op_ref[p, rows, :] * ws[p] for p in range(len(DILATIONS))])
        o_ref[rows, :] = _rms(num * (1.0 / den), g).astype(o_ref.dtype)
        return 0

    lax.fori_loop(0, n_blocks, merge, 0, unroll=2 * unroll)


def dil_attention(pd, g, group=32, unroll=4):
    b, s, _ = pd.shape
    n_pat = len(DILATIONS)
    assert DILATIONS == (1, 4, 16) and (s // DIL_SPAN) % group == 0
    return pl.pallas_call(
        functools.partial(_dil_kernel, seq=s, group=group, unroll=unroll),
        grid=(b, N_DIL),
        in_specs=[pl.BlockSpec((None, s, HEAD_DIM), lambda b_, h: (b_, 0, h)),
                  pl.BlockSpec((None, s, HEAD_DIM), lambda b_, h: (b_, 0, N_DIL + h)),
                  pl.BlockSpec((None, s, HEAD_DIM), lambda b_, h: (b_, 0, 2 * N_DIL + h)),
                  pl.BlockSpec((1, HEAD_DIM), lambda b_, h: (0, 0))],
        out_specs=pl.BlockSpec((None, s, HEAD_DIM), lambda b_, h: (b_, 0, h)),
        out_shape=jax.ShapeDtypeStruct((b, s, W_DIL), BF16),
        scratch_shapes=[pltpu.VMEM((s, HEAD_DIM), BF16),
                        pltpu.VMEM((s + DIL_SPAN, HEAD_DIM), BF16),
                        pltpu.VMEM((s + DIL_SPAN, 2 * HEAD_DIM), BF16),
                        pltpu.VMEM((s, HEAD_DIM), F32),
                        pltpu.VMEM((s, HEAD_DIM), F32),
                        pltpu.VMEM((s, HEAD_DIM), F32),
                        pltpu.VMEM((2, DIL_SPAN, 2 * DIL_SPAN), F32),
                        pltpu.VMEM((n_pat, s, HEAD_DIM), F32),
                        pltpu.VMEM((n_pat, s, HEAD_DIM), F32)],
        compiler_params=_params("parallel", "arbitrary"),
        name="dil_attention",
    )(pd, pd, pd, g.reshape(1, HEAD_DIM))


def kernel(x, positions, norm_mix_g, w_in, lambda_q1, lambda_k1, lambda_q2, lambda_k2, g_sb_out, g_diff_out, g_dil_out, w_out, norm_ffn_g, w_gate, w_up, w_down, norm_final_g):
    batch, seq, d = x.shape
    m = batch * seq
    depth = w_in.shape[0]
    xf = x.reshape(m, d)
    tabs = rope_tables(positions.reshape(m, 1))
    w_in, w_out, w_gate, w_up, w_down = (cast_bf16(w) for w in (w_in, w_out, w_gate, w_up, w_down))
    for layer in range(depth):
        pa, pd = in_proj(xf, norm_mix_g[layer], w_in, layer, tabs)
        pa = pa.reshape(batch, seq, W_A)
        pd = pd.reshape(batch, seq, 3 * W_DIL)
        lambda_init = 0.8 - 0.6 * math.exp(-0.3 * layer)
        lam_vecs = jnp.stack([lambda_q1[layer], lambda_k1[layer],
                              lambda_q2[layer], lambda_k2[layer]]).astype(F32)
        o_sb = sb_attention(pa, g_sb_out[layer]).reshape(m, W_SB)
        o_df = diff_attention(pa, lam_vecs, g_diff_out[layer], lambda_init).reshape(m, W_DIFF)
        o_dl = dil_attention(pd, g_dil_out[layer]).reshape(m, W_DIL)
        xf = mlp_block(o_sb, o_df, o_dl, w_out, xf, norm_ffn_g[layer], w_gate, w_up, w_down, layer,
                       final_g=norm_final_g if layer == depth - 1 else None)
    return xf.reshape(batch, seq, d)
```

```python
import functools
import math

import jax
import jax.numpy as jnp
from jax import lax
from jax.experimental import pallas as pl
from jax.experimental.pallas import tpu as pltpu

F32 = jnp.float32
BF16 = jnp.bfloat16

HEAD_DIM = 128
N_SB = 4
N_DIFF = 4
N_DIL = 8
W_SB = N_SB * HEAD_DIM
W_DIFF = N_DIFF * HEAD_DIM
W_DIL = N_DIL * HEAD_DIM
DIFF_QK = HEAD_DIM // 2
ROPE_THETA = 500000.0
ROPE_FRACTION = 0.25
DILATIONS = (1, 4, 16)
DIL_SPAN = 128
NORM_EPS = 1e-6
LANES = 128
VMEM_LIMIT = 56 * 1024 * 1024

NEG_INF = float("-inf")
LOG2_E = math.log2(math.e)
SB_DEAD_MASS = 106.0


def _params(*sem):
    return pltpu.CompilerParams(dimension_semantics=sem, vmem_limit_bytes=VMEM_LIMIT)


def _dot(a, b):
    return jnp.dot(a, b, preferred_element_type=F32)


def _dot_nt(a, b):
    return lax.dot_general(a, b, (((1,), (1,)), ((), ())), preferred_element_type=F32)


def _tile(x, n):
    return x if n == 1 else jnp.concatenate([x] * n, axis=1)


def _rms(x, g):
    return x * lax.rsqrt(jnp.mean(x * x, axis=-1, keepdims=True) + NORM_EPS) * g


def _cast_kernel(x_ref, o_ref):
    o_ref[...] = x_ref[...].astype(o_ref.dtype)


def cast_bf16(w, layer, rows=256):
    _, r, c = w.shape
    return pl.pallas_call(
        _cast_kernel,
        grid=(r // rows,),
        in_specs=[pl.BlockSpec((None, rows, c), lambda i: (layer, i, 0))],
        out_specs=pl.BlockSpec((rows, c), lambda i: (i, 0)),
        out_shape=jax.ShapeDtypeStruct((r, c), BF16),
        compiler_params=_params("parallel"),
        name="cast_bf16",
    )(w)


def _side_cast_specs(side, n_steps, step_index):
    in_specs, out_specs, out_shape = [], [], []
    for w, layer in side:
        _, r, c = w.shape
        rows = r // n_steps
        assert rows * n_steps == r and rows % 16 == 0
        in_specs.append(pl.BlockSpec((None, rows, c), lambda *g, layer=layer: (layer, step_index(*g), 0)))
        out_specs.append(pl.BlockSpec((rows, c), lambda *g: (step_index(*g), 0)))
        out_shape.append(jax.ShapeDtypeStruct((r, c), BF16))
    return in_specs, out_specs, out_shape


def _rope_lane_consts():
    lane = jnp.arange(LANES)
    rows = []
    for chunk in (DIFF_QK, HEAD_DIM):
        rot = int(chunk * ROPE_FRACTION)
        half = rot // 2
        inv_freq = ROPE_THETA ** (-jnp.arange(half, dtype=F32) / half)
        pos = lane % chunk
        freq = jnp.where(pos < rot, inv_freq[pos % half], 0.0)
        lo = jnp.where(pos < half, -1.0, 0.0)
        hi = jnp.where((pos >= half) & (pos < rot), 1.0, 0.0)
        rows += [freq, lo, hi]
    rows += [jnp.zeros((LANES,)), jnp.zeros((LANES,))]
    return jnp.stack(rows).astype(F32)


def _rope_table_kernel(pos_ref, c_ref, o_ref):
    p = pos_ref[...].astype(F32)
    for k in range(2):
        ang = p * c_ref[3 * k:3 * k + 1, :]
        s = jnp.sin(ang)
        o_ref[3 * k] = jnp.cos(ang)
        o_ref[3 * k + 1] = s * c_ref[3 * k + 1:3 * k + 2, :]
        o_ref[3 * k + 2] = s * c_ref[3 * k + 2:3 * k + 3, :]


def rope_tables(positions, tm=512):
    m = positions.shape[0]
    return pl.pallas_call(
        _rope_table_kernel,
        grid=(m // tm,),
        in_specs=[pl.BlockSpec((tm, 1), lambda i: (i, 0)),
                  pl.BlockSpec((8, LANES), lambda i: (0, 0))],
        out_specs=pl.BlockSpec((6, tm, LANES), lambda i: (0, i, 0)),
        out_shape=jax.ShapeDtypeStruct((6, m, LANES), F32),
        compiler_params=_params("parallel"),
        name="rope_tables",
    )(positions, _rope_lane_consts())


ROPE_NONE, ROPE_DIFF, ROPE_DIL = 0, 1, 2
_ROPE_SHIFT = {ROPE_DIFF: int(DIFF_QK * ROPE_FRACTION) // 2, ROPE_DIL: int(HEAD_DIM * ROPE_FRACTION) // 2}
W_A = 3 * (W_SB + W_DIFF)
PROJ_TN = 512
_PROJ_KINDS = (ROPE_NONE,) * 3 + (ROPE_DIFF,) * 2 + (ROPE_NONE,) + (ROPE_DIL,) * 4 + (ROPE_NONE,) * 2


def _resident(shape):
    return pl.BlockSpec(shape, lambda *_: (0,) * len(shape), pipeline_mode=pl.Buffered(1))


def _proj_kernel(x_ref, g_ref, w_ref, tab_ref, oa_ref, od_ref, h_ref):
    h_ref[...] = _rms(x_ref[...], g_ref[...]).astype(BF16)
    tn = PROJ_TN
    for idx, kind in enumerate(_PROJ_KINDS):
        acc = _dot(h_ref[...], w_ref[:, idx * tn:(idx + 1) * tn])
        o_ref, col = (oa_ref, idx * tn) if idx * tn < W_A else (od_ref, idx * tn - W_A)
        if kind == ROPE_NONE:
            o_ref[:, col:col + tn] = acc.astype(o_ref.dtype)
            continue
        base = 3 * (kind - 1)
        shift = _ROPE_SHIFT[kind]
        c, lo, hi = tab_ref[base], tab_ref[base + 1], tab_ref[base + 2]
        for q in range(tn // LANES):
            xk = acc[:, q * LANES:(q + 1) * LANES]
            rot = xk * c + pltpu.roll(xk, LANES - shift, 1) * lo + pltpu.roll(xk, shift, 1) * hi
            o_ref[:, col + q * LANES:col + (q + 1) * LANES] = rot.astype(o_ref.dtype)


def in_proj(x, g, w, tabs, tm=256):
    m, d = x.shape
    n = w.shape[1]
    assert n == len(_PROJ_KINDS) * PROJ_TN
    return pl.pallas_call(
        _proj_kernel,
        grid=(m // tm,),
        in_specs=[pl.BlockSpec((tm, d), lambda i: (i, 0)),
                  _resident((1, d)),
                  _resident((d, n)),
                  pl.BlockSpec((6, tm, LANES), lambda i: (0, i, 0))],
        out_specs=[pl.BlockSpec((tm, W_A), lambda i: (i, 0)),
                   pl.BlockSpec((tm, n - W_A), lambda i: (i, 0))],
        out_shape=[jax.ShapeDtypeStruct((m, W_A), BF16),
                   jax.ShapeDtypeStruct((m, n - W_A), F32)],
        scratch_shapes=[pltpu.VMEM((tm, d), BF16)],
        compiler_params=_params("parallel"),
        name="in_proj",
    )(x, g.reshape(1, d), w, tabs)


def _mlp_kernel(*refs, final_norm):
    if final_norm:
        sb_ref, df_ref, dl_ref, wo_ref, x_ref, g_ref, wg_ref, wu_ref, wd_ref, gf_ref, o_ref, h_ref = refs
    else:
        sb_ref, df_ref, dl_ref, wo_ref, x_ref, g_ref, wg_ref, wu_ref, wd_ref, o_ref, h_ref = refs
    j = pl.program_id(1)

    @pl.when(j == 0)
    def _():
        mixed = (_dot(sb_ref[...], wo_ref[0:W_SB, :])
                 + _dot(df_ref[...], wo_ref[W_SB:W_SB + W_DIFF, :])
                 + _dot(dl_ref[...], wo_ref[W_SB + W_DIFF:, :]))
        xn = x_ref[...] + mixed
        o_ref[...] = xn
        h_ref[...] = _rms(xn, g_ref[...]).astype(BF16)

    h = h_ref[...]
    gate = _dot(h, wg_ref[...])
    up = _dot(h, wu_ref[...])
    act = (gate * jax.nn.sigmoid(gate)) * up
    o_ref[...] += _dot(act.astype(BF16), wd_ref[...])

    if final_norm:
        @pl.when(j == pl.num_programs(1) - 1)
        def _():
            o_ref[...] = _rms(o_ref[...], gf_ref[...])


def mlp_block(o_sb, o_df, o_dl, w_out, x, g, wg, wu, wd, final_g=None, tm=512, tf=512):
    m, d = x.shape
    f = wg.shape[1]
    final_norm = final_g is not None
    in_specs = [pl.BlockSpec((tm, W_SB), lambda i, j: (i, 0)),
                pl.BlockSpec((tm, W_DIFF), lambda i, j: (i, 0)),
                pl.BlockSpec((tm, W_DIL), lambda i, j: (i, 0)),
                _resident(w_out.shape),
                pl.BlockSpec((tm, d), lambda i, j: (i, 0)),
                _resident((1, d)),
                pl.BlockSpec((d, tf), lambda i, j: (0, j)),
                pl.BlockSpec((d, tf), lambda i, j: (0, j)),
                pl.BlockSpec((tf, d), lambda i, j: (j, 0))]
    args = [o_sb, o_df, o_dl, w_out, x, g.reshape(1, d), wg, wu, wd]
    if final_norm:
        in_specs.append(_resident((1, d)))
        args.append(final_g.reshape(1, d))
    return pl.pallas_call(
        functools.partial(_mlp_kernel, final_norm=final_norm),
        grid=(m // tm, f // tf),
        in_specs=in_specs,
        out_specs=pl.BlockSpec((tm, d), lambda i, j: (i, 0)),
        out_shape=jax.ShapeDtypeStruct((m, d), F32),
        scratch_shapes=[pltpu.VMEM((tm, d), BF16)],
        compiler_params=_params("parallel", "arbitrary"),
        name="mlp_block",
    )(*args)


def _causal_sweep(i, n_chain, step, keys_per_iter=1, exhausted=None, wide=False, merge_diagonal=False):
    assert n_chain % keys_per_iter == 0
    base = n_chain * i
    diagonal = [[step(c, base + d, c == d) for c in range(d, n_chain)] for d in range(n_chain - 1, -1, -1)]
    if wide:
        _interleave([step(c, base + c, True) for c in range(n_chain)])
        _interleave([step(c, base, False, c) for c in range(1, n_chain)])
    elif merge_diagonal:
        _interleave(sum(diagonal, []))
    else:
        for group in diagonal:
            _interleave(group)

    def body(t, _):
        first = base - 1 - t * keys_per_iter
        if wide:
            _interleave([step(c, first - (keys_per_iter - 1), False, keys_per_iter) for c in range(n_chain)])
        else:
            _interleave([step(c, first - u, False) for u in range(keys_per_iter) for c in range(n_chain)])
        return 0

    trips = (n_chain // keys_per_iter) * i
    if exhausted is None:
        lax.fori_loop(0, trips, body, 0)
        return

    def more(state):
        t, done = state
        return (t < trips) & jnp.logical_not(done)

    def advance(state):
        body(state[0], 0)
        return state[0] + 1, exhausted()

    lax.while_loop(more, advance, (0, exhausted()))


def _interleave(steps):
    steps = list(steps)
    while steps:
        alive = []
        for g in steps:
            try:
                next(g)
                alive.append(g)
            except StopIteration:
                pass
        steps = alive


def _from_key_matrix(blk):
    idx = jnp.arange(blk)
    tri = (idx[:, None] >= idx[None, :]).astype(BF16)
    return jnp.concatenate([tri, tri], axis=0)


def _sb_kernel(q_ref, k_ref, v_ref, later_ref, g_ref, o_ref, qs_ref, acc_ref, carry_ref, *,
               blk, n_chain, keys_per_iter):
    i = pl.program_id(2)
    scale = HEAD_DIM ** -0.5
    qs_ref[...] = (q_ref[...].astype(F32) * scale).astype(BF16)
    acc_ref[...] = jnp.zeros_like(acc_ref)
    carry_ref[...] = jnp.zeros_like(carry_ref)
    row = lax.broadcasted_iota(jnp.int32, (blk, blk), 0)
    col = lax.broadcasted_iota(jnp.int32, (blk, blk), 1)
    strict = col < row

    def step(c, j, diag):
        rows = slice(c * blk, (c + 1) * blk)
        start = pl.multiple_of(j * blk, blk)
        k = k_ref[pl.ds(start, blk), :]
        v = v_ref[pl.ds(start, blk), :]
        z = _dot_nt(qs_ref[rows, :], k)
        yield
        sp = jnp.maximum(z, 0.0) + jnp.log(1.0 + jnp.exp2(jnp.abs(z) * -LOG2_E))
        spm = jnp.where(strict, sp, 0.0) if diag else sp
        hi = spm.astype(BF16)
        lo = (spm - hi.astype(F32)).astype(BF16)
        from_key = _dot(jnp.concatenate([hi, lo], axis=1), later_ref[...])
        yield
        loga = z - (from_key + _tile(carry_ref[rows, :], blk // LANES))
        if diag:
            loga = jnp.where(strict, loga, NEG_INF)
        pv = _dot(jnp.exp(loga).astype(BF16), v)
        carry_ref[rows, :] += from_key[:, 0:1]
        yield
        acc_ref[rows, :] += pv

    _causal_sweep(i, n_chain, step, keys_per_iter, merge_diagonal=True,
                  exhausted=lambda: jnp.min(carry_ref[...]) >= SB_DEAD_MASS)
    o_ref[...] = _rms(acc_ref[...], g_ref[...]).astype(o_ref.dtype)


def sb_attention(pa, g, blk=256, n_chain=4, keys_per_iter=2):
    b, s, _ = pa.shape
    bq = blk * n_chain
    return pl.pallas_call(
        functools.partial(_sb_kernel, blk=blk, n_chain=n_chain, keys_per_iter=keys_per_iter),
        grid=(b, N_SB, s // bq),
        in_specs=[pl.BlockSpec((None, bq, HEAD_DIM), lambda b_, h, i: (b_, i, h)),
                  pl.BlockSpec((None, s, HEAD_DIM), lambda b_, h, i: (b_, 0, N_SB + h)),
                  pl.BlockSpec((None, s, HEAD_DIM), lambda b_, h, i: (b_, 0, 2 * N_SB + h)),
                  pl.BlockSpec((2 * blk, blk), lambda b_, h, i: (0, 0)),
                  pl.BlockSpec((1, HEAD_DIM), lambda b_, h, i: (0, 0))],
        out_specs=pl.BlockSpec((None, bq, HEAD_DIM), lambda b_, h, i: (b_, i, h)),
        out_shape=jax.ShapeDtypeStruct((b, s, W_SB), BF16),
        scratch_shapes=[pltpu.VMEM((bq, HEAD_DIM), BF16),
                        pltpu.VMEM((bq, HEAD_DIM), F32),
                        pltpu.VMEM((bq, LANES), F32)],
        compiler_params=_params("parallel", "parallel", "arbitrary"),
        name="sb_attention",
    )(pa, pa, pa, _from_key_matrix(blk), g.reshape(1, HEAD_DIM))


def _diff_kernel(*refs, blk, n_chain, keys_per_iter, lambda_init, n_side):
    q_ref, k_ref, v_ref, lam_ref, g_ref = refs[:5]
    side_in, o_ref = refs[5:5 + n_side], refs[5 + n_side]
    side_out = refs[6 + n_side:6 + 2 * n_side]
    qs_ref, vx_ref, m_ref, acc_ref = refs[6 + 2 * n_side:]
    for src, dst in zip(side_in, side_out):
        dst[...] = src[...].astype(BF16)
    i = pl.program_id(2)
    seq = v_ref.shape[0]

    @pl.when(i == 0)
    def _():
        vx_ref[:, 0:HEAD_DIM] = v_ref[...]
        vx_ref[:, HEAD_DIM:] = jnp.ones((seq, HEAD_DIM), BF16)

    scale = DIFF_QK ** -0.5
    q = q_ref[...].astype(F32) * scale
    lc = lax.broadcasted_iota(jnp.int32, (blk, HEAD_DIM), 1)
    for c in range(n_chain):
        qc = q[c * blk:(c + 1) * blk]
        qs_ref[(2 * c) * blk:(2 * c + 1) * blk, :] = jnp.where(lc < DIFF_QK, qc, 0.0).astype(BF16)
        qs_ref[(2 * c + 1) * blk:(2 * c + 2) * blk, :] = jnp.where(lc >= DIFF_QK, qc, 0.0).astype(BF16)
    m_ref[...] = jnp.full(m_ref.shape, NEG_INF, F32)
    acc_ref[...] = jnp.zeros_like(acc_ref)
    row = lax.broadcasted_iota(jnp.int32, (blk, blk), 0)
    col = lax.broadcasted_iota(jnp.int32, (blk, blk), 1)
    causal = col <= row

    def step(c, j, diag, width=1):
        start = pl.multiple_of(j * blk, blk)
        k = k_ref[pl.ds(start, width * blk), :]
        vx = vx_ref[pl.ds(start, width * blk), :]
        rows = [slice((2 * c + e) * blk, (2 * c + e + 1) * blk) for e in range(2)]
        s = [_dot_nt(qs_ref[r, :], k) for r in rows]
        yield
        alpha, pv = [], []
        for e, r in enumerate(rows):
            se = jnp.where(causal, s[e], NEG_INF) if diag else s[e]
            m_prev = m_ref[r, :]
            m_new = jnp.maximum(m_prev, jnp.max(se, axis=-1, keepdims=True))
            alpha.append(jnp.exp(m_prev - m_new))
            p = jnp.exp((se - _tile(m_new, width * blk // LANES)).astype(BF16))
            m_ref[r, :] = m_new
            pv.append(_dot(p, vx))
        yield
        for e, r in enumerate(rows):
            acc_ref[r, :] = _tile(alpha[e], 2) * acc_ref[r, :] + pv[e]

    _causal_sweep(i, n_chain, step, keys_per_iter, wide=True)
    lq = lam_ref[...]
    lam = (jnp.exp(jnp.sum(lq[0:1] * lq[1:2], axis=-1, keepdims=True))
           - jnp.exp(jnp.sum(lq[2:3] * lq[3:4], axis=-1, keepdims=True)) + lambda_init)
    g = g_ref[...]
    for c in range(n_chain):
        o = []
        for e in range(2):
            acc = acc_ref[(2 * c + e) * blk:(2 * c + e + 1) * blk, :]
            o.append(acc[:, :HEAD_DIM] / acc[:, HEAD_DIM:])
        out = o[0] - lam * o[1]
        o_ref[c * blk:(c + 1) * blk, :] = (_rms(out, g) * (1.0 - lambda_init)).astype(o_ref.dtype)


def diff_attention(pa, lam_vecs, g, lambda_init, side=(), blk=256, n_chain=4, keys_per_iter=4):
    b, s, _ = pa.shape
    c0 = 3 * N_SB
    bq = blk * n_chain
    n_i = s // bq
    side_in, side_out, side_shape = _side_cast_specs(
        side, b * N_DIFF * n_i, lambda b_, h, i: (b_ * N_DIFF + h) * n_i + i)
    out = pl.pallas_call(
        functools.partial(_diff_kernel, blk=blk, n_chain=n_chain, keys_per_iter=keys_per_iter,
                          lambda_init=lambda_init, n_side=len(side)),
        grid=(b, N_DIFF, n_i),
        in_specs=[pl.BlockSpec((None, bq, HEAD_DIM), lambda b_, h, i: (b_, i, c0 + h)),
                  pl.BlockSpec((None, s, HEAD_DIM), lambda b_, h, i: (b_, 0, c0 + N_DIFF + h)),
                  pl.BlockSpec((None, s, HEAD_DIM), lambda b_, h, i: (b_, 0, c0 + 2 * N_DIFF + h)),
                  pl.BlockSpec((4, DIFF_QK), lambda b_, h, i: (0, 0)),
                  pl.BlockSpec((1, HEAD_DIM), lambda b_, h, i: (0, 0))] + side_in,
        out_specs=[pl.BlockSpec((None, bq, HEAD_DIM), lambda b_, h, i: (b_, i, h))] + side_out,
        out_shape=[jax.ShapeDtypeStruct((b, s, W_DIFF), BF16)] + side_shape,
        scratch_shapes=[pltpu.VMEM((2 * bq, HEAD_DIM), BF16),
                        pltpu.VMEM((s, 2 * HEAD_DIM), BF16),
                        pltpu.VMEM((2 * bq, LANES), F32),
                        pltpu.VMEM((2 * bq, 2 * HEAD_DIM), F32)],
        compiler_params=_params("parallel", "parallel", "arbitrary"),
        name="diff_attention",
    )(pa, pa, pa, lam_vecs, g.reshape(1, HEAD_DIM), *[w for w, _ in side])
    return out[0], out[1:]


def _dil_kernel(q_ref, k_ref, v_ref, g_ref, o_ref, qd_ref, kd_ref, vd_ref, q4_ref, k4_ref, v4_ref,
                bias_ref, op_ref, lse_ref, *, seq, group, unroll):
    blk = DIL_SPAN
    scale = HEAD_DIM ** -0.5
    n_blocks = seq // blk
    kd_ref[0:blk, :] = jnp.zeros((blk, HEAD_DIM), BF16)
    vd_ref[0:blk, 0:HEAD_DIM] = jnp.zeros((blk, HEAD_DIM), BF16)
    vd_ref[:, HEAD_DIM:] = jnp.ones((seq + blk, HEAD_DIM), BF16)
    ii = lax.broadcasted_iota(jnp.int32, (blk, 2 * blk), 0)
    jj = lax.broadcasted_iota(jnp.int32, (blk, 2 * blk), 1)
    dist = blk + ii - jj
    in_window = (dist >= 0) & (dist <= DIL_SPAN)
    bias_ref[0] = jnp.where(in_window & (jj >= blk), 0.0, NEG_INF)
    bias_ref[1] = jnp.where(in_window, 0.0, NEG_INF)
    quarter = seq // 4

    for p, r in enumerate(DILATIONS):
        per_seq = n_blocks // r
        shift = per_seq.bit_length() - 1

        def block_of(n, per_seq=per_seq, shift=shift):
            return lax.shift_right_logical(n, shift), n & (per_seq - 1)

        def natural_rows(n, r=r, block_of=block_of):
            c, i = block_of(n)
            src = c + r * blk * i
            if r == 1:
                return pl.ds(pl.multiple_of(src, blk), blk)
            return pl.ds(src, blk, stride=r)

        def gather(n, _, r=r, block_of=block_of, natural_rows=natural_rows):
            here = pl.ds(pl.multiple_of(blk * n, blk), blk)
            if r == 16:
                c, i = block_of(n)
                rows = pl.ds((c & 3) * quarter + lax.shift_right_logical(c, 2) + 4 * blk * i, blk, stride=4)
                q, k, v = q4_ref[rows, :], k4_ref[rows, :], v4_ref[rows, :]
            else:
                rows = natural_rows(n)
                q, k, v = q_ref[rows, :], k_ref[rows, :], v_ref[rows, :]
            if r == 4:
                q4_ref[here, :], k4_ref[here, :], v4_ref[here, :] = q, k, v
            dst = pl.ds(pl.multiple_of(blk * (n + 1), blk), blk)
            kd_ref[dst, :] = k.astype(BF16)
            vd_ref[dst, 0:HEAD_DIM] = v.astype(BF16)
            qd_ref[here, :] = (q * scale).astype(BF16)
            return 0

        lax.fori_loop(0, n_blocks, gather, 0, unroll=unroll)

        def attend(n, p=p, block_of=block_of, natural_rows=natural_rows):
            _, i = block_of(n)
            rows = natural_rows(n)
            q = qd_ref[pl.ds(pl.multiple_of(blk * n, blk), blk), :]
            win = pl.ds(pl.multiple_of(blk * n, blk), 2 * blk)
            s = _dot_nt(q, kd_ref[win, :])
            yield
            s = s + bias_ref[jnp.where(i == 0, 0, 1)]
            m = jnp.max(s, axis=-1, keepdims=True)
            pv = _dot(jnp.exp((s - m).astype(BF16)), vd_ref[win, :])
            yield
            l = pv[:, HEAD_DIM:]
            op_ref[p, rows, :] = pv[:, :HEAD_DIM] / l
            lse_ref[p, rows, :] = m + jnp.log(l)

        def attend_group(t, _, attend=attend):
            _interleave([attend(t * group + u) for u in range(group)])
            return 0

        lax.fori_loop(0, n_blocks // group, attend_group, 0)

    g = g_ref[...]

    def merge(n, _):
        rows = pl.ds(pl.multiple_of(n * blk, blk), blk)
        lses = [lse_ref[p, rows, :] for p in range(len(DILATIONS))]
        top = functools.reduce(jnp.maximum, lses)
        ws = [jnp.exp(x - top) for x in lses]
        den = functools.reduce(lambda a, b_: a + b_, ws)
        num = functools.reduce(lambda a, b_: a + b_,
                               [op_ref[p, rows, :] * ws[p] for p in range(len(DILATIONS))])
        o_ref[rows, :] = _rms(num * (1.0 / den), g).astype(o_ref.dtype)
        return 0

    lax.fori_loop(0, n_blocks, merge, 0, unroll=2 * unroll)


def dil_attention(pd, g, group=32, unroll=4):
    b, s, _ = pd.shape
    n_pat = len(DILATIONS)
    assert DILATIONS == (1, 4, 16) and (s // DIL_SPAN) % group == 0
    return pl.pallas_call(
        functools.partial(_dil_kernel, seq=s, group=group, unroll=unroll),
        grid=(b, N_DIL),
        in_specs=[pl.BlockSpec((None, s, HEAD_DIM), lambda b_, h: (b_, 0, h)),
                  pl.BlockSpec((None, s, HEAD_DIM), lambda b_, h: (b_, 0, N_DIL + h)),
                  pl.BlockSpec((None, s, HEAD_DIM), lambda b_, h: (b_, 0, 2 * N_DIL + h)),
                  pl.BlockSpec((1, HEAD_DIM), lambda b_, h: (0, 0))],
        out_specs=pl.BlockSpec((None, s, HEAD_DIM), lambda b_, h: (b_, 0, h)),
        out_shape=jax.ShapeDtypeStruct((b, s, W_DIL), BF16),
        scratch_shapes=[pltpu.VMEM((s, HEAD_DIM), BF16),
                        pltpu.VMEM((s + DIL_SPAN, HEAD_DIM), BF16),
                        pltpu.VMEM((s + DIL_SPAN, 2 * HEAD_DIM), BF16),
                        pltpu.VMEM((s, HEAD_DIM), F32),
                        pltpu.VMEM((s, HEAD_DIM), F32),
                        pltpu.VMEM((s, HEAD_DIM), F32),
                        pltpu.VMEM((2, DIL_SPAN, 2 * DIL_SPAN), F32),
                        pltpu.VMEM((n_pat, s, HEAD_DIM), F32),
                        pltpu.VMEM((n_pat, s, HEAD_DIM), F32)],
        compiler_params=_params("parallel", "arbitrary"),
        name="dil_attention",
    )(pd, pd, pd, g.reshape(1, HEAD_DIM))


def kernel(x, positions, norm_mix_g, w_in, lambda_q1, lambda_k1, lambda_q2, lambda_k2, g_sb_out, g_diff_out, g_dil_out, w_out, norm_ffn_g, w_gate, w_up, w_down, norm_final_g):
    batch, seq, d = x.shape
    m = batch * seq
    depth = w_in.shape[0]
    xf = x.reshape(m, d)
    tabs = rope_tables(positions.reshape(m, 1))
    w_in_bf = cast_bf16(w_in, 0)
    for layer in range(depth):
        pa, pd = in_proj(xf, norm_mix_g[layer], w_in_bf, tabs)
        pa = pa.reshape(batch, seq, W_A)
        pd = pd.reshape(batch, seq, 3 * W_DIL)
        lambda_init = 0.8 - 0.6 * math.exp(-0.3 * layer)
        lam_vecs = jnp.stack([lambda_q1[layer], lambda_k1[layer],
                              lambda_q2[layer], lambda_k2[layer]]).astype(F32)
        o_sb = sb_attention(pa, g_sb_out[layer]).reshape(m, W_SB)
        side = [(w, layer) for w in (w_out, w_gate, w_up, w_down)]
        side += [(w_in, layer + 1)] if layer + 1 < depth else []
        o_df, converted = diff_attention(pa, lam_vecs, g_diff_out[layer], lambda_init, side)
        o_df = o_df.reshape(m, W_DIFF)
        w_out_bf, w_gate_bf, w_up_bf, w_down_bf = converted[:4]
        w_in_bf = converted[4] if layer + 1 < depth else None
        o_dl = dil_attention(pd, g_dil_out[layer]).reshape(m, W_DIL)
        xf = mlp_block(o_sb, o_df, o_dl, w_out_bf, xf, norm_ffn_g[layer], w_gate_bf, w_up_bf, w_down_bf,
                       final_g=norm_final_g if layer == depth - 1 else None)
    return xf.reshape(batch, seq, d)
```

```python
import functools
import math

import jax
import jax.numpy as jnp
import numpy as np
from jax import lax
from jax.experimental import pallas as pl
from jax.experimental.pallas import tpu as pltpu

F32 = jnp.float32
BF16 = jnp.bfloat16

HEAD_DIM = 128
N_SB = 4
N_DIFF = 4
N_DIL = 8
W_SB = N_SB * HEAD_DIM
W_DIFF = N_DIFF * HEAD_DIM
W_DIL = N_DIL * HEAD_DIM
DIFF_QK = HEAD_DIM // 2
ROPE_THETA = 500000.0
ROPE_FRACTION = 0.25
DILATIONS = (1, 4, 16)
DIL_SPAN = 128
NORM_EPS = 1e-6
LANES = 128
VMEM_LIMIT = 56 * 1024 * 1024

NEG_INF = float("-inf")
LOG2_E = math.log2(math.e)
SB_DEAD_MASS = 106.0


def _params(*sem):
    return pltpu.CompilerParams(dimension_semantics=sem, vmem_limit_bytes=VMEM_LIMIT)


def _dot(a, b):
    return jnp.dot(a, b, preferred_element_type=F32)


def _dot_nt(a, b):
    return lax.dot_general(a, b, (((1,), (1,)), ((), ())), preferred_element_type=F32)


def _tile(x, n):
    return x if n == 1 else jnp.concatenate([x] * n, axis=1)


def _rms(x, g):
    return x * lax.rsqrt(jnp.mean(x * x, axis=-1, keepdims=True) + NORM_EPS) * g


def _cast_kernel(x_ref, o_ref):
    o_ref[...] = x_ref[...].astype(o_ref.dtype)


def cast_bf16(w, layer, rows=256):
    _, r, c = w.shape
    return pl.pallas_call(
        _cast_kernel,
        grid=(r // rows,),
        in_specs=[pl.BlockSpec((None, rows, c), lambda i: (layer, i, 0))],
        out_specs=pl.BlockSpec((rows, c), lambda i: (i, 0)),
        out_shape=jax.ShapeDtypeStruct((r, c), BF16),
        compiler_params=_params("parallel"),
        name="cast_bf16",
    )(w)


def _side_cast_specs(side, n_steps, step_index):
    in_specs, out_specs, out_shape = [], [], []
    for w, layer in side:
        _, r, c = w.shape
        rows = r // n_steps
        assert rows * n_steps == r and rows % 16 == 0
        in_specs.append(pl.BlockSpec((None, rows, c), lambda *g, layer=layer: (layer, step_index(*g), 0)))
        out_specs.append(pl.BlockSpec((rows, c), lambda *g: (step_index(*g), 0)))
        out_shape.append(jax.ShapeDtypeStruct((r, c), BF16))
    return in_specs, out_specs, out_shape


def _rope_lane_consts():
    pos = np.arange(LANES)
    rows = []
    for chunk in (DIFF_QK, HEAD_DIM):
        rot = int(chunk * ROPE_FRACTION)
        half = rot // 2
        inv_freq = ROPE_THETA ** (-jnp.arange(half, dtype=F32) / half)
        per_chunk = jnp.concatenate([inv_freq, inv_freq, jnp.zeros((chunk - rot,), F32)])
        freq = jnp.tile(per_chunk, LANES // chunk)
        lo = np.where(pos % chunk < half, -1.0, 0.0)
        hi = np.where((pos % chunk >= half) & (pos % chunk < rot), 1.0, 0.0)
        rows += [freq, jnp.asarray(lo, F32), jnp.asarray(hi, F32)]
    rows += [jnp.zeros((LANES,), F32)] * 2
    return jnp.stack(rows)


def _rope_table_kernel(pos_ref, c_ref, o_ref):
    p = pos_ref[...].astype(F32)
    for k in range(2):
        ang = p * c_ref[3 * k:3 * k + 1, :]
        s = jnp.sin(ang)
        o_ref[3 * k] = jnp.cos(ang)
        o_ref[3 * k + 1] = s * c_ref[3 * k + 1:3 * k + 2, :]
        o_ref[3 * k + 2] = s * c_ref[3 * k + 2:3 * k + 3, :]


def rope_tables(positions, tm=512):
    m = positions.shape[0]
    return pl.pallas_call(
        _rope_table_kernel,
        grid=(m // tm,),
        in_specs=[pl.BlockSpec((tm, 1), lambda i: (i, 0)),
                  pl.BlockSpec((8, LANES), lambda i: (0, 0))],
        out_specs=pl.BlockSpec((6, tm, LANES), lambda i: (0, i, 0)),
        out_shape=jax.ShapeDtypeStruct((6, m, LANES), F32),
        compiler_params=_params("parallel"),
        name="rope_tables",
    )(positions, _rope_lane_consts())


ROPE_NONE, ROPE_DIFF, ROPE_DIL = 0, 1, 2
_ROPE_SHIFT = {ROPE_DIFF: int(DIFF_QK * ROPE_FRACTION) // 2, ROPE_DIL: int(HEAD_DIM * ROPE_FRACTION) // 2}
W_A = 3 * (W_SB + W_DIFF)
PROJ_TN = 512
_PROJ_KINDS = (ROPE_NONE,) * 3 + (ROPE_DIFF,) * 2 + (ROPE_NONE,) + (ROPE_DIL,) * 4 + (ROPE_NONE,) * 2


def _resident(shape):
    return pl.BlockSpec(shape, lambda *_: (0,) * len(shape), pipeline_mode=pl.Buffered(1))


def _proj_kernel(x_ref, g_ref, w_ref, tab_ref, oa_ref, od_ref, h_ref):
    h_ref[...] = _rms(x_ref[...], g_ref[...]).astype(BF16)
    tn = PROJ_TN
    for idx, kind in enumerate(_PROJ_KINDS):
        acc = _dot(h_ref[...], w_ref[:, idx * tn:(idx + 1) * tn])
        o_ref, col = (oa_ref, idx * tn) if idx * tn < W_A else (od_ref, idx * tn - W_A)
        if kind == ROPE_NONE:
            o_ref[:, col:col + tn] = acc.astype(o_ref.dtype)
            continue
        base = 3 * (kind - 1)
        shift = _ROPE_SHIFT[kind]
        c, lo, hi = tab_ref[base], tab_ref[base + 1], tab_ref[base + 2]
        for q in range(tn // LANES):
            xk = acc[:, q * LANES:(q + 1) * LANES]
            rot = xk * c + pltpu.roll(xk, LANES - shift, 1) * lo + pltpu.roll(xk, shift, 1) * hi
            o_ref[:, col + q * LANES:col + (q + 1) * LANES] = rot.astype(o_ref.dtype)


def in_proj(x, g, w, tabs, tm=256):
    m, d = x.shape
    n = w.shape[1]
    assert n == len(_PROJ_KINDS) * PROJ_TN
    return pl.pallas_call(
        _proj_kernel,
        grid=(m // tm,),
        in_specs=[pl.BlockSpec((tm, d), lambda i: (i, 0)),
                  _resident((1, d)),
                  _resident((d, n)),
                  pl.BlockSpec((6, tm, LANES), lambda i: (0, i, 0))],
        out_specs=[pl.BlockSpec((tm, W_A), lambda i: (i, 0)),
                   pl.BlockSpec((tm, n - W_A), lambda i: (i, 0))],
        out_shape=[jax.ShapeDtypeStruct((m, W_A), BF16),
                   jax.ShapeDtypeStruct((m, n - W_A), F32)],
        scratch_shapes=[pltpu.VMEM((tm, d), BF16)],
        compiler_params=_params("parallel"),
        name="in_proj",
    )(x, g.reshape(1, d), w, tabs)


def _mlp_kernel(*refs, final_norm):
    if final_norm:
        sb_ref, df_ref, dl_ref, wo_ref, x_ref, g_ref, wg_ref, wu_ref, wd_ref, gf_ref, o_ref, h_ref = refs
    else:
        sb_ref, df_ref, dl_ref, wo_ref, x_ref, g_ref, wg_ref, wu_ref, wd_ref, o_ref, h_ref = refs
    j = pl.program_id(1)

    @pl.when(j == 0)
    def _():
        mixed = (_dot(sb_ref[...], wo_ref[0:W_SB, :])
                 + _dot(df_ref[...], wo_ref[W_SB:W_SB + W_DIFF, :])
                 + _dot(dl_ref[...], wo_ref[W_SB + W_DIFF:, :]))
        xn = x_ref[...] + mixed
        o_ref[...] = xn
        h_ref[...] = _rms(xn, g_ref[...]).astype(BF16)

    h = h_ref[...]
    gate = _dot(h, wg_ref[...])
    up = _dot(h, wu_ref[...])
    act = (gate * jax.nn.sigmoid(gate)) * up
    o_ref[...] += _dot(act.astype(BF16), wd_ref[...])

    if final_norm:
        @pl.when(j == pl.num_programs(1) - 1)
        def _():
            o_ref[...] = _rms(o_ref[...], gf_ref[...])


def mlp_block(o_sb, o_df, o_dl, w_out, x, g, wg, wu, wd, final_g=None, tm=512, tf=512):
    m, d = x.shape
    f = wg.shape[1]
    final_norm = final_g is not None
    in_specs = [pl.BlockSpec((tm, W_SB), lambda i, j: (i, 0)),
                pl.BlockSpec((tm, W_DIFF), lambda i, j: (i, 0)),
                pl.BlockSpec((tm, W_DIL), lambda i, j: (i, 0)),
                _resident(w_out.shape),
                pl.BlockSpec((tm, d), lambda i, j: (i, 0)),
                _resident((1, d)),
                pl.BlockSpec((d, tf), lambda i, j: (0, j)),
                pl.BlockSpec((d, tf), lambda i, j: (0, j)),
                pl.BlockSpec((tf, d), lambda i, j: (j, 0))]
    args = [o_sb, o_df, o_dl, w_out, x, g.reshape(1, d), wg, wu, wd]
    if final_norm:
        in_specs.append(_resident((1, d)))
        args.append(final_g.reshape(1, d))
    return pl.pallas_call(
        functools.partial(_mlp_kernel, final_norm=final_norm),
        grid=(m // tm, f // tf),
        in_specs=in_specs,
        out_specs=pl.BlockSpec((tm, d), lambda i, j: (i, 0)),
        out_shape=jax.ShapeDtypeStruct((m, d), F32),
        scratch_shapes=[pltpu.VMEM((tm, d), BF16)],
        compiler_params=_params("parallel", "arbitrary"),
        name="mlp_block",
    )(*args)


def _causal_sweep(i, n_chain, step, keys_per_iter, wide=False, exhausted=None):
    assert n_chain % keys_per_iter == 0
    base = n_chain * i
    if wide:
        _interleave([step(c, base + c, True) for c in range(n_chain)])
        _interleave([step(c, base, False, c) for c in range(1, n_chain)])
    else:
        _interleave([step(c, base + d, c == d) for d in range(n_chain - 1, -1, -1) for c in range(d, n_chain)])

    def body(t, _):
        first = base - 1 - t * keys_per_iter
        if wide:
            _interleave([step(c, first - (keys_per_iter - 1), False, keys_per_iter) for c in range(n_chain)])
        else:
            _interleave([step(c, first - u, False) for u in range(keys_per_iter) for c in range(n_chain)])
        return 0

    trips = (n_chain // keys_per_iter) * i
    if exhausted is None:
        lax.fori_loop(0, trips, body, 0)
        return

    def more(state):
        t, done = state
        return (t < trips) & jnp.logical_not(done)

    def advance(state):
        body(state[0], 0)
        return state[0] + 1, exhausted()

    lax.while_loop(more, advance, (0, exhausted()))


def _interleave(steps):
    steps = list(steps)
    while steps:
        alive = []
        for g in steps:
            try:
                next(g)
                alive.append(g)
            except StopIteration:
                pass
        steps = alive


def _from_key_matrix(blk):
    tri = np.tril(np.ones((blk, blk), np.float32))
    return jnp.asarray(np.concatenate([tri, tri], axis=0), BF16)


def _sb_kernel(q_ref, k_ref, v_ref, later_ref, g_ref, o_ref, qs_ref, acc_ref, carry_ref, *,
               blk, n_chain, keys_per_iter):
    i = pl.program_id(2)
    scale = HEAD_DIM ** -0.5
    qs_ref[...] = (q_ref[...].astype(F32) * scale).astype(BF16)
    acc_ref[...] = jnp.zeros_like(acc_ref)
    carry_ref[...] = jnp.zeros_like(carry_ref)
    row = lax.broadcasted_iota(jnp.int32, (blk, blk), 0)
    col = lax.broadcasted_iota(jnp.int32, (blk, blk), 1)
    strict = col < row

    def step(c, j, diag):
        rows = slice(c * blk, (c + 1) * blk)
        start = pl.multiple_of(j * blk, blk)
        k = k_ref[pl.ds(start, blk), :]
        v = v_ref[pl.ds(start, blk), :]
        z = _dot_nt(qs_ref[rows, :], k)
        yield
        sp = jnp.maximum(z, 0.0) + jnp.log(1.0 + jnp.exp2(jnp.abs(z) * -LOG2_E))
        spm = jnp.where(strict, sp, 0.0) if diag else sp
        hi = spm.astype(BF16)
        lo = (spm - hi.astype(F32)).astype(BF16)
        from_key = _dot(jnp.concatenate([hi, lo], axis=1), later_ref[...])
        yield
        loga = z - (from_key + _tile(carry_ref[rows, :], blk // LANES))
        if diag:
            loga = jnp.where(strict, loga, NEG_INF)
        pv = _dot(jnp.exp(loga).astype(BF16), v)
        carry_ref[rows, :] += from_key[:, 0:1]
        yield
        acc_ref[rows, :] += pv

    _causal_sweep(i, n_chain, step, keys_per_iter,
                  exhausted=lambda: jnp.min(carry_ref[...]) >= SB_DEAD_MASS)
    o_ref[...] = _rms(acc_ref[...], g_ref[...]).astype(o_ref.dtype)


def sb_attention(pa, g, blk=256, n_chain=4, keys_per_iter=2):
    b, s, _ = pa.shape
    bq = blk * n_chain
    return pl.pallas_call(
        functools.partial(_sb_kernel, blk=blk, n_chain=n_chain, keys_per_iter=keys_per_iter),
        grid=(b, N_SB, s // bq),
        in_specs=[pl.BlockSpec((None, bq, HEAD_DIM), lambda b_, h, i: (b_, i, h)),
                  pl.BlockSpec((None, s, HEAD_DIM), lambda b_, h, i: (b_, 0, N_SB + h)),
                  pl.BlockSpec((None, s, HEAD_DIM), lambda b_, h, i: (b_, 0, 2 * N_SB + h)),
                  pl.BlockSpec((2 * blk, blk), lambda b_, h, i: (0, 0)),
                  pl.BlockSpec((1, HEAD_DIM), lambda b_, h, i: (0, 0))],
        out_specs=pl.BlockSpec((None, bq, HEAD_DIM), lambda b_, h, i: (b_, i, h)),
        out_shape=jax.ShapeDtypeStruct((b, s, W_SB), BF16),
        scratch_shapes=[pltpu.VMEM((bq, HEAD_DIM), BF16),
                        pltpu.VMEM((bq, HEAD_DIM), F32),
                        pltpu.VMEM((bq, LANES), F32)],
        compiler_params=_params("parallel", "parallel", "arbitrary"),
        name="sb_attention",
    )(pa, pa, pa, _from_key_matrix(blk), g.reshape(1, HEAD_DIM))


def _diff_kernel(*refs, blk, n_chain, keys_per_iter, lambda_init, n_side):
    q_ref, k_ref, v_ref, lam_ref, g_ref = refs[:5]
    side_in, o_ref = refs[5:5 + n_side], refs[5 + n_side]
    side_out = refs[6 + n_side:6 + 2 * n_side]
    qs_ref, vx_ref, m_ref, acc_ref = refs[6 + 2 * n_side:]
    for src, dst in zip(side_in, side_out):
        dst[...] = src[...].astype(BF16)
    i = pl.program_id(2)
    seq = v_ref.shape[0]

    @pl.when(i == 0)
    def _():
        vx_ref[:, 0:HEAD_DIM] = v_ref[...]
        vx_ref[:, HEAD_DIM:] = jnp.ones((seq, HEAD_DIM), BF16)

    scale = DIFF_QK ** -0.5
    q = q_ref[...].astype(F32) * scale
    lc = lax.broadcasted_iota(jnp.int32, (blk, HEAD_DIM), 1)
    for c in range(n_chain):
        qc = q[c * blk:(c + 1) * blk]
        qs_ref[(2 * c) * blk:(2 * c + 1) * blk, :] = jnp.where(lc < DIFF_QK, qc, 0.0).astype(BF16)
        qs_ref[(2 * c + 1) * blk:(2 * c + 2) * blk, :] = jnp.where(lc >= DIFF_QK, qc, 0.0).astype(BF16)
    m_ref[...] = jnp.full(m_ref.shape, NEG_INF, F32)
    acc_ref[...] = jnp.zeros_like(acc_ref)
    row = lax.broadcasted_iota(jnp.int32, (blk, blk), 0)
    col = lax.broadcasted_iota(jnp.int32, (blk, blk), 1)
    causal = col <= row

    def step(c, j, diag, width=1):
        start = pl.multiple_of(j * blk, blk)
        k = k_ref[pl.ds(start, width * blk), :]
        vx = vx_ref[pl.ds(start, width * blk), :]
        rows = [slice((2 * c + e) * blk, (2 * c + e + 1) * blk) for e in range(2)]
        s = [_dot_nt(qs_ref[r, :], k) for r in rows]
        yield
        alpha, pv = [], []
        for e, r in enumerate(rows):
            se = jnp.where(causal, s[e], NEG_INF) if diag else s[e]
            m_prev = m_ref[r, :]
            m_new = jnp.maximum(m_prev, jnp.max(se, axis=-1, keepdims=True))
            alpha.append(jnp.exp(m_prev - m_new))
            p = jnp.exp((se - _tile(m_new, width * blk // LANES)).astype(BF16))
            m_ref[r, :] = m_new
            pv.append(_dot(p, vx))
        yield
        for e, r in enumerate(rows):
            acc_ref[r, :] = _tile(alpha[e], 2) * acc_ref[r, :] + pv[e]

    _causal_sweep(i, n_chain, step, keys_per_iter, wide=True)
    lq = lam_ref[...]
    lam = (jnp.exp(jnp.sum(lq[0:1] * lq[1:2], axis=-1, keepdims=True))
           - jnp.exp(jnp.sum(lq[2:3] * lq[3:4], axis=-1, keepdims=True)) + lambda_init)
    g = g_ref[...]
    for c in range(n_chain):
        o = []
        for e in range(2):
            acc = acc_ref[(2 * c + e) * blk:(2 * c + e + 1) * blk, :]
            o.append(acc[:, :HEAD_DIM] / acc[:, HEAD_DIM:])
        out = o[0] - lam * o[1]
        o_ref[c * blk:(c + 1) * blk, :] = (_rms(out, g) * (1.0 - lambda_init)).astype(o_ref.dtype)


def diff_attention(pa, lam_vecs, g, lambda_init, side=(), blk=256, n_chain=4, keys_per_iter=4):
    b, s, _ = pa.shape
    c0 = 3 * N_SB
    bq = blk * n_chain
    n_i = s // bq
    side_in, side_out, side_shape = _side_cast_specs(
        side, b * N_DIFF * n_i, lambda b_, h, i: (b_ * N_DIFF + h) * n_i + i)
    out = pl.pallas_call(
        functools.partial(_diff_kernel, blk=blk, n_chain=n_chain, keys_per_iter=keys_per_iter,
                          lambda_init=lambda_init, n_side=len(side)),
        grid=(b, N_DIFF, n_i),
        in_specs=[pl.BlockSpec((None, bq, HEAD_DIM), lambda b_, h, i: (b_, i, c0 + h)),
                  pl.BlockSpec((None, s, HEAD_DIM), lambda b_, h, i: (b_, 0, c0 + N_DIFF + h)),
                  pl.BlockSpec((None, s, HEAD_DIM), lambda b_, h, i: (b_, 0, c0 + 2 * N_DIFF + h)),
                  pl.BlockSpec((4, DIFF_QK), lambda b_, h, i: (0, 0)),
                  pl.BlockSpec((1, HEAD_DIM), lambda b_, h, i: (0, 0))] + side_in,
        out_specs=[pl.BlockSpec((None, bq, HEAD_DIM), lambda b_, h, i: (b_, i, h))] + side_out,
        out_shape=[jax.ShapeDtypeStruct((b, s, W_DIFF), BF16)] + side_shape,
        scratch_shapes=[pltpu.VMEM((2 * bq, HEAD_DIM), BF16),
                        pltpu.VMEM((s, 2 * HEAD_DIM), BF16),
                        pltpu.VMEM((2 * bq, LANES), F32),
                        pltpu.VMEM((2 * bq, 2 * HEAD_DIM), F32)],
        compiler_params=_params("parallel", "parallel", "arbitrary"),
        name="diff_attention",
    )(pa, pa, pa, lam_vecs, g.reshape(1, HEAD_DIM), *[w for w, _ in side])
    return out[0], out[1:]


def _dil_kernel(q_ref, k_ref, v_ref, g_ref, o_ref, qd_ref, kd_ref, vd_ref, q4_ref, k4_ref, v4_ref,
                bias_ref, op_ref, lse_ref, *, seq, group, unroll):
    blk = DIL_SPAN
    scale = HEAD_DIM ** -0.5
    n_blocks = seq // blk
    kd_ref[0:blk, :] = jnp.zeros((blk, HEAD_DIM), BF16)
    vd_ref[0:blk, 0:HEAD_DIM] = jnp.zeros((blk, HEAD_DIM), BF16)
    vd_ref[:, HEAD_DIM:] = jnp.ones((seq + blk, HEAD_DIM), BF16)
    ii = lax.broadcasted_iota(jnp.int32, (blk, 2 * blk), 0)
    jj = lax.broadcasted_iota(jnp.int32, (blk, 2 * blk), 1)
    dist = blk + ii - jj
    in_window = (dist >= 0) & (dist <= DIL_SPAN)
    bias_ref[0] = jnp.where(in_window & (jj >= blk), 0.0, NEG_INF)
    bias_ref[1] = jnp.where(in_window, 0.0, NEG_INF)
    quarter = seq // 4

    for p, r in enumerate(DILATIONS):
        per_seq = n_blocks // r
        shift = per_seq.bit_length() - 1

        def block_of(n, per_seq=per_seq, shift=shift):
            return lax.shift_right_logical(n, shift), n & (per_seq - 1)

        def natural_rows(n, r=r, block_of=block_of):
            c, i = block_of(n)
            src = c + r * blk * i
            if r == 1:
                return pl.ds(pl.multiple_of(src, blk), blk)
            return pl.ds(src, blk, stride=r)

        def gather(n, _, r=r, block_of=block_of, natural_rows=natural_rows):
            here = pl.ds(pl.multiple_of(blk * n, blk), blk)
            if r == 16:
                c, i = block_of(n)
                rows = pl.ds((c & 3) * quarter + lax.shift_right_logical(c, 2) + 4 * blk * i, blk, stride=4)
                q, k, v = q4_ref[rows, :], k4_ref[rows, :], v4_ref[rows, :]
            else:
                rows = natural_rows(n)
                q, k, v = q_ref[rows, :], k_ref[rows, :], v_ref[rows, :]
            if r == 4:
                q4_ref[here, :], k4_ref[here, :], v4_ref[here, :] = q, k, v
            dst = pl.ds(pl.multiple_of(blk * (n + 1), blk), blk)
            kd_ref[dst, :] = k.astype(BF16)
            vd_ref[dst, 0:HEAD_DIM] = v.astype(BF16)
            qd_ref[here, :] = (q * scale).astype(BF16)
            return 0

        lax.fori_loop(0, n_blocks, gather, 0, unroll=unroll)

        def attend(n, p=p, block_of=block_of, natural_rows=natural_rows):
            _, i = block_of(n)
            rows = natural_rows(n)
            q = qd_ref[pl.ds(pl.multiple_of(blk * n, blk), blk), :]
            win = pl.ds(pl.multiple_of(blk * n, blk), 2 * blk)
            s = _dot_nt(q, kd_ref[win, :])
            yield
            s = s + bias_ref[jnp.where(i == 0, 0, 1)]
            m = jnp.max(s, axis=-1, keepdims=True)
            pv = _dot(jnp.exp((s - m).astype(BF16)), vd_ref[win, :])
            yield
            l = pv[:, HEAD_DIM:]
            op_ref[p, rows, :] = pv[:, :HEAD_DIM] / l
            lse_ref[p, rows, :] = m + jnp.log(l)

        def attend_group(t, _, attend=attend):
            _interleave([attend(t * group + u) for u in range(group)])
            return 0

        lax.fori_loop(0, n_blocks // group, attend_group, 0)

    g = g_ref[...]

    def merge(n, _):
        rows = pl.ds(pl.multiple_of(n * blk, blk), blk)
        lses = [lse_ref[p, rows, :] for p in range(len(DILATIONS))]
        top = functools.reduce(jnp.maximum, lses)
        ws = [jnp.exp(x - top) for x in lses]
        den = functools.reduce(lambda a, b_: a + b_, ws)
        num = functools.reduce(lambda a, b_: a + b_,
                               [op_ref[p, rows, :] * ws[p] for p in range(len(DILATIONS))])
        o_ref[rows, :] = _rms(num * (1.0 / den), g).astype(o_ref.dtype)
        return 0

    lax.fori_loop(0, n_blocks, merge, 0, unroll=2 * unroll)


def dil_attention(pd, g, group=32, unroll=4):
    b, s, _ = pd.shape
    n_pat = len(DILATIONS)
    assert DILATIONS == (1, 4, 16) and (s // DIL_SPAN) % group == 0
    return pl.pallas_call(
        functools.partial(_dil_kernel, seq=s, group=group, unroll=unroll),
        grid=(b, N_DIL),
        in_specs=[pl.BlockSpec((None, s, HEAD_DIM), lambda b_, h: (b_, 0, h)),
                  pl.BlockSpec((None, s, HEAD_DIM), lambda b_, h: (b_, 0, N_DIL + h)),
                  pl.BlockSpec((None, s, HEAD_DIM), lambda b_, h: (b_, 0, 2 * N_DIL + h)),
                  pl.BlockSpec((1, HEAD_DIM), lambda b_, h: (0, 0))],
        out_specs=pl.BlockSpec((None, s, HEAD_DIM), lambda b_, h: (b_, 0, h)),
        out_shape=jax.ShapeDtypeStruct((b, s, W_DIL), BF16),
        scratch_shapes=[pltpu.VMEM((s, HEAD_DIM), BF16),
                        pltpu.VMEM((s + DIL_SPAN, HEAD_DIM), BF16),
                        pltpu.VMEM((s + DIL_SPAN, 2 * HEAD_DIM), BF16),
                        pltpu.VMEM((s, HEAD_DIM), F32),
                        pltpu.VMEM((s, HEAD_DIM), F32),
                        pltpu.VMEM((s, HEAD_DIM), F32),
                        pltpu.VMEM((2, DIL_SPAN, 2 * DIL_SPAN), F32),
                        pltpu.VMEM((n_pat, s, HEAD_DIM), F32),
                        pltpu.VMEM((n_pat, s, HEAD_DIM), F32)],
        compiler_params=_params("parallel", "arbitrary"),
        name="dil_attention",
    )(pd, pd, pd, g.reshape(1, HEAD_DIM))


def kernel(x, positions, norm_mix_g, w_in, lambda_q1, lambda_k1, lambda_q2, lambda_k2, g_sb_out, g_diff_out, g_dil_out, w_out, norm_ffn_g, w_gate, w_up, w_down, norm_final_g):
    batch, seq, d = x.shape
    m = batch * seq
    depth = w_in.shape[0]
    xf = x.reshape(m, d)
    tabs = rope_tables(positions.reshape(m, 1))
    w_in_bf = cast_bf16(w_in, 0)
    for layer in range(depth):
        pa, pd = in_proj(xf, norm_mix_g[layer], w_in_bf, tabs)
        pa = pa.reshape(batch, seq, W_A)
        pd = pd.reshape(batch, seq, 3 * W_DIL)
        lambda_init = 0.8 - 0.6 * math.exp(-0.3 * layer)
        lam_vecs = jnp.stack([lambda_q1[layer], lambda_k1[layer],
                              lambda_q2[layer], lambda_k2[layer]]).astype(F32)
        o_sb = sb_attention(pa, g_sb_out[layer]).reshape(m, W_SB)
        side = [(w, layer) for w in (w_out, w_gate, w_up, w_down)]
        side += [(w_in, layer + 1)] if layer + 1 < depth else []
        o_df, converted = diff_attention(pa, lam_vecs, g_diff_out[layer], lambda_init, side)
        o_df = o_df.reshape(m, W_DIFF)
        w_out_bf, w_gate_bf, w_up_bf, w_down_bf = converted[:4]
        w_in_bf = converted[4] if layer + 1 < depth else None
        o_dl = dil_attention(pd, g_dil_out[layer]).reshape(m, W_DIL)
        xf = mlp_block(o_sb, o_df, o_dl, w_out_bf, xf, norm_ffn_g[layer], w_gate_bf, w_up_bf, w_down_bf,
                       final_g=norm_final_g if layer == depth - 1 else None)
    return xf.reshape(batch, seq, d)
```

```python
import functools
import math

import jax
import jax.numpy as jnp
import numpy as np
from jax import lax
from jax.experimental import pallas as pl
from jax.experimental.pallas import tpu as pltpu

F32 = jnp.float32
BF16 = jnp.bfloat16

HEAD_DIM = 128
N_SB = 4
N_DIFF = 4
N_DIL = 8
W_SB = N_SB * HEAD_DIM
W_DIFF = N_DIFF * HEAD_DIM
W_DIL = N_DIL * HEAD_DIM
DIFF_QK = HEAD_DIM // 2
ROPE_THETA = 500000.0
ROPE_FRACTION = 0.25
DILATIONS = (1, 4, 16)
DIL_SPAN = 128
NORM_EPS = 1e-6
LANES = 128
VMEM_LIMIT = 56 * 1024 * 1024

NEG_INF = float("-inf")
LOG2_E = math.log2(math.e)
SB_DEAD_MASS = 106.0


def _params(*sem):
    return pltpu.CompilerParams(dimension_semantics=sem, vmem_limit_bytes=VMEM_LIMIT)


def _dot(a, b):
    return jnp.dot(a, b, preferred_element_type=F32)


def _dot_nt(a, b):
    return lax.dot_general(a, b, (((1,), (1,)), ((), ())), preferred_element_type=F32)


def _tile(x, n):
    return x if n == 1 else jnp.concatenate([x] * n, axis=1)


def _rms(x, g):
    return x * lax.rsqrt(jnp.mean(x * x, axis=-1, keepdims=True) + NORM_EPS) * g


def _cast_kernel(x_ref, o_ref):
    o_ref[...] = x_ref[...].astype(o_ref.dtype)


def cast_bf16(w, layer, rows=256):
    _, r, c = w.shape
    return pl.pallas_call(
        _cast_kernel,
        grid=(r // rows,),
        in_specs=[pl.BlockSpec((None, rows, c), lambda i: (layer, i, 0))],
        out_specs=pl.BlockSpec((rows, c), lambda i: (i, 0)),
        out_shape=jax.ShapeDtypeStruct((r, c), BF16),
        compiler_params=_params("parallel"),
        name="cast_bf16",
    )(w)


def _side_cast_specs(side, n_steps, step_index):
    in_specs, out_specs, out_shape = [], [], []
    for w, layer in side:
        _, r, c = w.shape
        rows = r // n_steps
        assert rows * n_steps == r and rows % 16 == 0
        in_specs.append(pl.BlockSpec((None, rows, c), lambda *g, layer=layer: (layer, step_index(*g), 0)))
        out_specs.append(pl.BlockSpec((rows, c), lambda *g: (step_index(*g), 0)))
        out_shape.append(jax.ShapeDtypeStruct((r, c), BF16))
    return in_specs, out_specs, out_shape


def _rope_lane_consts():
    pos = np.arange(LANES)
    rows = []
    for chunk in (DIFF_QK, HEAD_DIM):
        rot = int(chunk * ROPE_FRACTION)
        half = rot // 2
        inv_freq = ROPE_THETA ** (-jnp.arange(half, dtype=F32) / half)
        per_chunk = jnp.concatenate([inv_freq, inv_freq, jnp.zeros((chunk - rot,), F32)])
        freq = jnp.tile(per_chunk, LANES // chunk)
        lo = np.where(pos % chunk < half, -1.0, 0.0)
        hi = np.where((pos % chunk >= half) & (pos % chunk < rot), 1.0, 0.0)
        rows += [freq, jnp.asarray(lo, F32), jnp.asarray(hi, F32)]
    rows += [jnp.zeros((LANES,), F32)] * 2
    return jnp.stack(rows)


def _rope_table_kernel(pos_ref, c_ref, o_ref):
    p = pos_ref[...].astype(F32)
    for k in range(2):
        ang = p * c_ref[3 * k:3 * k + 1, :]
        s = jnp.sin(ang)
        o_ref[3 * k] = jnp.cos(ang)
        o_ref[3 * k + 1] = s * c_ref[3 * k + 1:3 * k + 2, :]
        o_ref[3 * k + 2] = s * c_ref[3 * k + 2:3 * k + 3, :]


def rope_tables(positions, tm=512):
    m = positions.shape[0]
    return pl.pallas_call(
        _rope_table_kernel,
        grid=(m // tm,),
        in_specs=[pl.BlockSpec((tm, 1), lambda i: (i, 0)),
                  pl.BlockSpec((8, LANES), lambda i: (0, 0))],
        out_specs=pl.BlockSpec((6, tm, LANES), lambda i: (0, i, 0)),
        out_shape=jax.ShapeDtypeStruct((6, m, LANES), F32),
        compiler_params=_params("parallel"),
        name="rope_tables",
    )(positions, _rope_lane_consts())


ROPE_NONE, ROPE_DIFF, ROPE_DIL = 0, 1, 2
_ROPE_SHIFT = {ROPE_DIFF: int(DIFF_QK * ROPE_FRACTION) // 2, ROPE_DIL: int(HEAD_DIM * ROPE_FRACTION) // 2}
W_A = 3 * (W_SB + W_DIFF)
PROJ_TN = 512
_PROJ_KINDS = (ROPE_NONE,) * 3 + (ROPE_DIFF,) * 2 + (ROPE_NONE,) + (ROPE_DIL,) * 4 + (ROPE_NONE,) * 2


def _resident(shape):
    return pl.BlockSpec(shape, lambda *_: (0,) * len(shape), pipeline_mode=pl.Buffered(1))


def _proj_kernel(x_ref, g_ref, w_ref, tab_ref, oa_ref, od_ref, h_ref):
    h_ref[...] = _rms(x_ref[...], g_ref[...]).astype(BF16)
    tn = PROJ_TN
    for idx, kind in enumerate(_PROJ_KINDS):
        acc = _dot(h_ref[...], w_ref[:, idx * tn:(idx + 1) * tn])
        o_ref, col = (oa_ref, idx * tn) if idx * tn < W_A else (od_ref, idx * tn - W_A)
        if kind == ROPE_NONE:
            o_ref[:, col:col + tn] = acc.astype(o_ref.dtype)
            continue
        base = 3 * (kind - 1)
        shift = _ROPE_SHIFT[kind]
        c, lo, hi = tab_ref[base], tab_ref[base + 1], tab_ref[base + 2]
        for q in range(tn // LANES):
            xk = acc[:, q * LANES:(q + 1) * LANES]
            rot = xk * c + pltpu.roll(xk, LANES - shift, 1) * lo + pltpu.roll(xk, shift, 1) * hi
            o_ref[:, col + q * LANES:col + (q + 1) * LANES] = rot.astype(o_ref.dtype)


def in_proj(x, g, w, tabs, tm=256):
    m, d = x.shape
    n = w.shape[1]
    assert n == len(_PROJ_KINDS) * PROJ_TN
    return pl.pallas_call(
        _proj_kernel,
        grid=(m // tm,),
        in_specs=[pl.BlockSpec((tm, d), lambda i: (i, 0)),
                  _resident((1, d)),
                  _resident((d, n)),
                  pl.BlockSpec((6, tm, LANES), lambda i: (0, i, 0))],
        out_specs=[pl.BlockSpec((tm, W_A), lambda i: (i, 0)),
                   pl.BlockSpec((tm, n - W_A), lambda i: (i, 0))],
        out_shape=[jax.ShapeDtypeStruct((m, W_A), BF16),
                   jax.ShapeDtypeStruct((m, n - W_A), F32)],
        scratch_shapes=[pltpu.VMEM((tm, d), BF16)],
        compiler_params=_params("parallel"),
        name="in_proj",
    )(x, g.reshape(1, d), w, tabs)


def _mlp_kernel(*refs, final_norm):
    if final_norm:
        sb_ref, df_ref, dl_ref, wo_ref, x_ref, g_ref, wg_ref, wu_ref, wd_ref, gf_ref, o_ref, h_ref = refs
    else:
        sb_ref, df_ref, dl_ref, wo_ref, x_ref, g_ref, wg_ref, wu_ref, wd_ref, o_ref, h_ref = refs
    j = pl.program_id(1)

    @pl.when(j == 0)
    def _():
        mixed = (_dot(sb_ref[...], wo_ref[0:W_SB, :])
                 + _dot(df_ref[...], wo_ref[W_SB:W_SB + W_DIFF, :])
                 + _dot(dl_ref[...], wo_ref[W_SB + W_DIFF:, :]))
        xn = x_ref[...] + mixed
        o_ref[...] = xn
        h_ref[...] = _rms(xn, g_ref[...]).astype(BF16)

    h = h_ref[...]
    gate = _dot(h, wg_ref[...])
    up = _dot(h, wu_ref[...])
    act = (gate * jax.nn.sigmoid(gate)) * up
    o_ref[...] += _dot(act.astype(BF16), wd_ref[...])

    if final_norm:
        @pl.when(j == pl.num_programs(1) - 1)
        def _():
            o_ref[...] = _rms(o_ref[...], gf_ref[...])


def mlp_block(o_sb, o_df, o_dl, w_out, x, g, wg, wu, wd, final_g=None, tm=512, tf=512):
    m, d = x.shape
    f = wg.shape[1]
    final_norm = final_g is not None
    in_specs = [pl.BlockSpec((tm, W_SB), lambda i, j: (i, 0)),
                pl.BlockSpec((tm, W_DIFF), lambda i, j: (i, 0)),
                pl.BlockSpec((tm, W_DIL), lambda i, j: (i, 0)),
                _resident(w_out.shape),
                pl.BlockSpec((tm, d), lambda i, j: (i, 0)),
                _resident((1, d)),
                pl.BlockSpec((d, tf), lambda i, j: (0, j)),
                pl.BlockSpec((d, tf), lambda i, j: (0, j)),
                pl.BlockSpec((tf, d), lambda i, j: (j, 0))]
    args = [o_sb, o_df, o_dl, w_out, x, g.reshape(1, d), wg, wu, wd]
    if final_norm:
        in_specs.append(_resident((1, d)))
        args.append(final_g.reshape(1, d))
    return pl.pallas_call(
        functools.partial(_mlp_kernel, final_norm=final_norm),
        grid=(m // tm, f // tf),
        in_specs=in_specs,
        out_specs=pl.BlockSpec((tm, d), lambda i, j: (i, 0)),
        out_shape=jax.ShapeDtypeStruct((m, d), F32),
        scratch_shapes=[pltpu.VMEM((tm, d), BF16)],
        compiler_params=_params("parallel", "arbitrary"),
        name="mlp_block",
    )(*args)


def _causal_sweep(i, n_chain, step, keys_per_iter, wide=False, exhausted=None):
    assert n_chain % keys_per_iter == 0
    base = n_chain * i
    if wide:
        _interleave([step(c, base + c, True) for c in range(n_chain)])
        _interleave([step(c, base, False, c) for c in range(1, n_chain)])
    else:
        _interleave([step(c, base + d, c == d) for d in range(n_chain - 1, -1, -1) for c in range(d, n_chain)])

    def body(t, _):
        first = base - 1 - t * keys_per_iter
        if wide:
            _interleave([step(c, first - (keys_per_iter - 1), False, keys_per_iter) for c in range(n_chain)])
        else:
            _interleave([step(c, first - u, False) for u in range(keys_per_iter) for c in range(n_chain)])
        return 0

    trips = (n_chain // keys_per_iter) * i
    if exhausted is None:
        lax.fori_loop(0, trips, body, 0)
        return

    def more(state):
        t, done = state
        return (t < trips) & jnp.logical_not(done)

    def advance(state):
        body(state[0], 0)
        return state[0] + 1, exhausted()

    lax.while_loop(more, advance, (0, exhausted()))


def _interleave(steps):
    steps = list(steps)
    while steps:
        alive = []
        for g in steps:
            try:
                next(g)
                alive.append(g)
            except StopIteration:
                pass
        steps = alive


def _from_key_matrix(blk):
    tri = np.tril(np.ones((blk, blk), np.float32))
    return jnp.asarray(np.concatenate([tri, tri], axis=0), BF16)


def _sb_kernel(q_ref, k_ref, v_ref, later_ref, g_ref, o_ref, qs_ref, acc_ref, carry_ref, *,
               blk, n_chain, keys_per_iter):
    i = pl.program_id(2)
    scale = HEAD_DIM ** -0.5
    qs_ref[...] = (q_ref[...].astype(F32) * scale).astype(BF16)
    acc_ref[...] = jnp.zeros_like(acc_ref)
    carry_ref[...] = jnp.zeros_like(carry_ref)
    row = lax.broadcasted_iota(jnp.int32, (blk, blk), 0)
    col = lax.broadcasted_iota(jnp.int32, (blk, blk), 1)
    strict = col < row

    def step(c, j, diag):
        rows = slice(c * blk, (c + 1) * blk)
        start = pl.multiple_of(j * blk, blk)
        k = k_ref[pl.ds(start, blk), :]
        v = v_ref[pl.ds(start, blk), :]
        z = _dot_nt(qs_ref[rows, :], k)
        yield
        sp = jnp.maximum(z, 0.0) + jnp.log(1.0 + jnp.exp2(jnp.abs(z) * -LOG2_E))
        spm = jnp.where(strict, sp, 0.0) if diag else sp
        hi = spm.astype(BF16)
        lo = (spm - hi.astype(F32)).astype(BF16)
        from_key = _dot(jnp.concatenate([hi, lo], axis=1), later_ref[...])
        yield
        loga = z - (from_key + _tile(carry_ref[rows, :], blk // LANES))
        if diag:
            loga = jnp.where(strict, loga, NEG_INF)
        pv = _dot(jnp.exp(loga).astype(BF16), v)
        carry_ref[rows, :] += from_key[:, 0:1]
        yield
        acc_ref[rows, :] += pv

    _causal_sweep(i, n_chain, step, keys_per_iter,
                  exhausted=lambda: jnp.min(carry_ref[...]) >= SB_DEAD_MASS)
    o_ref[...] = _rms(acc_ref[...], g_ref[...]).astype(o_ref.dtype)


def sb_attention(pa, g, blk=256, n_chain=4, keys_per_iter=1):
    b, s, _ = pa.shape
    bq = blk * n_chain
    return pl.pallas_call(
        functools.partial(_sb_kernel, blk=blk, n_chain=n_chain, keys_per_iter=keys_per_iter),
        grid=(b, N_SB, s // bq),
        in_specs=[pl.BlockSpec((None, bq, HEAD_DIM), lambda b_, h, i: (b_, i, h)),
                  pl.BlockSpec((None, s, HEAD_DIM), lambda b_, h, i: (b_, 0, N_SB + h)),
                  pl.BlockSpec((None, s, HEAD_DIM), lambda b_, h, i: (b_, 0, 2 * N_SB + h)),
                  pl.BlockSpec((2 * blk, blk), lambda b_, h, i: (0, 0)),
                  pl.BlockSpec((1, HEAD_DIM), lambda b_, h, i: (0, 0))],
        out_specs=pl.BlockSpec((None, bq, HEAD_DIM), lambda b_, h, i: (b_, i, h)),
        out_shape=jax.ShapeDtypeStruct((b, s, W_SB), BF16),
        scratch_shapes=[pltpu.VMEM((bq, HEAD_DIM), BF16),
                        pltpu.VMEM((bq, HEAD_DIM), F32),
                        pltpu.VMEM((bq, LANES), F32)],
        compiler_params=_params("parallel", "parallel", "arbitrary"),
        name="sb_attention",
    )(pa, pa, pa, _from_key_matrix(blk), g.reshape(1, HEAD_DIM))


def _diff_kernel(*refs, blk, n_chain, keys_per_iter, lambda_init, n_side):
    q_ref, k_ref, v_ref, lam_ref, g_ref = refs[:5]
    side_in, o_ref = refs[5:5 + n_side], refs[5 + n_side]
    side_out = refs[6 + n_side:6 + 2 * n_side]
    qs_ref, vx_ref, m_ref, acc_ref = refs[6 + 2 * n_side:]
    for src, dst in zip(side_in, side_out):
        dst[...] = src[...].astype(BF16)
    i = pl.program_id(2)
    seq = v_ref.shape[0]

    @pl.when(i == 0)
    def _():
        vx_ref[:, 0:HEAD_DIM] = v_ref[...]
        vx_ref[:, HEAD_DIM:] = jnp.ones((seq, HEAD_DIM), BF16)

    scale = DIFF_QK ** -0.5
    q = q_ref[...].astype(F32) * scale
    lc = lax.broadcasted_iota(jnp.int32, (blk, HEAD_DIM), 1)
    for c in range(n_chain):
        qc = q[c * blk:(c + 1) * blk]
        qs_ref[(2 * c) * blk:(2 * c + 1) * blk, :] = jnp.where(lc < DIFF_QK, qc, 0.0).astype(BF16)
        qs_ref[(2 * c + 1) * blk:(2 * c + 2) * blk, :] = jnp.where(lc >= DIFF_QK, qc, 0.0).astype(BF16)
    m_ref[...] = jnp.full(m_ref.shape, NEG_INF, F32)
    acc_ref[...] = jnp.zeros_like(acc_ref)
    row = lax.broadcasted_iota(jnp.int32, (blk, blk), 0)
    col = lax.broadcasted_iota(jnp.int32, (blk, blk), 1)
    causal = col <= row

    def step(c, j, diag, width=1):
        start = pl.multiple_of(j * blk, blk)
        k = k_ref[pl.ds(start, width * blk), :]
        vx = vx_ref[pl.ds(start, width * blk), :]
        rows = [slice((2 * c + e) * blk, (2 * c + e + 1) * blk) for e in range(2)]
        s = [_dot_nt(qs_ref[r, :], k) for r in rows]
        yield
        alpha, pv = [], []
        for e, r in enumerate(rows):
            se = jnp.where(causal, s[e], NEG_INF) if diag else s[e]
            m_prev = m_ref[r, :]
            m_new = jnp.maximum(m_prev, jnp.max(se, axis=-1, keepdims=True))
            alpha.append(jnp.exp(m_prev - m_new))
            p = jnp.exp((se - _tile(m_new, width * blk // LANES)).astype(BF16))
            m_ref[r, :] = m_new
            pv.append(_dot(p, vx))
        yield
        for e, r in enumerate(rows):
            acc_ref[r, :] = _tile(alpha[e], 2) * acc_ref[r, :] + pv[e]

    _causal_sweep(i, n_chain, step, keys_per_iter, wide=True)
    lq = lam_ref[...]
    lam = (jnp.exp(jnp.sum(lq[0:1] * lq[1:2], axis=-1, keepdims=True))
           - jnp.exp(jnp.sum(lq[2:3] * lq[3:4], axis=-1, keepdims=True)) + lambda_init)
    g = g_ref[...]
    for c in range(n_chain):
        o = []
        for e in range(2):
            acc = acc_ref[(2 * c + e) * blk:(2 * c + e + 1) * blk, :]
            o.append(acc[:, :HEAD_DIM] / acc[:, HEAD_DIM:])
        out = o[0] - lam * o[1]
        o_ref[c * blk:(c + 1) * blk, :] = (_rms(out, g) * (1.0 - lambda_init)).astype(o_ref.dtype)


def diff_attention(pa, lam_vecs, g, lambda_init, side=(), blk=256, n_chain=4, keys_per_iter=4):
    b, s, _ = pa.shape
    c0 = 3 * N_SB
    bq = blk * n_chain
    n_i = s // bq
    side_in, side_out, side_shape = _side_cast_specs(
        side, b * N_DIFF * n_i, lambda b_, h, i: (b_ * N_DIFF + h) * n_i + i)
    out = pl.pallas_call(
        functools.partial(_diff_kernel, blk=blk, n_chain=n_chain, keys_per_iter=keys_per_iter,
                          lambda_init=lambda_init, n_side=len(side)),
        grid=(b, N_DIFF, n_i),
        in_specs=[pl.BlockSpec((None, bq, HEAD_DIM), lambda b_, h, i: (b_, i, c0 + h)),
                  pl.BlockSpec((None, s, HEAD_DIM), lambda b_, h, i: (b_, 0, c0 + N_DIFF + h)),
                  pl.BlockSpec((None, s, HEAD_DIM), lambda b_, h, i: (b_, 0, c0 + 2 * N_DIFF + h)),
                  pl.BlockSpec((4, DIFF_QK), lambda b_, h, i: (0, 0)),
                  pl.BlockSpec((1, HEAD_DIM), lambda b_, h, i: (0, 0))] + side_in,
        out_specs=[pl.BlockSpec((None, bq, HEAD_DIM), lambda b_, h, i: (b_, i, h))] + side_out,
        out_shape=[jax.ShapeDtypeStruct((b, s, W_DIFF), BF16)] + side_shape,
        scratch_shapes=[pltpu.VMEM((2 * bq, HEAD_DIM), BF16),
                        pltpu.VMEM((s, 2 * HEAD_DIM), BF16),
                        pltpu.VMEM((2 * bq, LANES), F32),
                        pltpu.VMEM((2 * bq, 2 * HEAD_DIM), F32)],
        compiler_params=_params("parallel", "parallel", "arbitrary"),
        name="diff_attention",
    )(pa, pa, pa, lam_vecs, g.reshape(1, HEAD_DIM), *[w for w, _ in side])
    return out[0], out[1:]


def _dil_kernel(q_ref, k_ref, v_ref, g_ref, o_ref, qd_ref, kd_ref, vd_ref, q4_ref, k4_ref, v4_ref,
                bias_ref, op_ref, lse_ref, *, seq, group, unroll):
    blk = DIL_SPAN
    scale = HEAD_DIM ** -0.5
    n_blocks = seq // blk
    kd_ref[0:blk, :] = jnp.zeros((blk, HEAD_DIM), BF16)
    vd_ref[0:blk, 0:HEAD_DIM] = jnp.zeros((blk, HEAD_DIM), BF16)
    vd_ref[:, HEAD_DIM:] = jnp.ones((seq + blk, HEAD_DIM), BF16)
    ii = lax.broadcasted_iota(jnp.int32, (blk, 2 * blk), 0)
    jj = lax.broadcasted_iota(jnp.int32, (blk, 2 * blk), 1)
    dist = blk + ii - jj
    in_window = (dist >= 0) & (dist <= DIL_SPAN)
    bias_ref[0] = jnp.where(in_window & (jj >= blk), 0.0, NEG_INF)
    bias_ref[1] = jnp.where(in_window, 0.0, NEG_INF)
    quarter = seq // 4

    for p, r in enumerate(DILATIONS):
        per_seq = n_blocks // r
        shift = per_seq.bit_length() - 1

        def block_of(n, per_seq=per_seq, shift=shift):
            return lax.shift_right_logical(n, shift), n & (per_seq - 1)

        def natural_rows(n, r=r, block_of=block_of):
            c, i = block_of(n)
            src = c + r * blk * i
            if r == 1:
                return pl.ds(pl.multiple_of(src, blk), blk)
            return pl.ds(src, blk, stride=r)

        def gather(n, _, r=r, block_of=block_of, natural_rows=natural_rows):
            here = pl.ds(pl.multiple_of(blk * n, blk), blk)
            if r == 16:
                c, i = block_of(n)
                rows = pl.ds((c & 3) * quarter + lax.shift_right_logical(c, 2) + 4 * blk * i, blk, stride=4)
                q, k, v = q4_ref[rows, :], k4_ref[rows, :], v4_ref[rows, :]
            else:
                rows = natural_rows(n)
                q, k, v = q_ref[rows, :], k_ref[rows, :], v_ref[rows, :]
            if r == 4:
                q4_ref[here, :], k4_ref[here, :], v4_ref[here, :] = q, k, v
            dst = pl.ds(pl.multiple_of(blk * (n + 1), blk), blk)
            kd_ref[dst, :] = k.astype(BF16)
            vd_ref[dst, 0:HEAD_DIM] = v.astype(BF16)
            qd_ref[here, :] = (q * scale).astype(BF16)
            return 0

        lax.fori_loop(0, n_blocks, gather, 0, unroll=unroll)

        def attend(n, p=p, block_of=block_of, natural_rows=natural_rows):
            _, i = block_of(n)
            rows = natural_rows(n)
            q = qd_ref[pl.ds(pl.multiple_of(blk * n, blk), blk), :]
            win = pl.ds(pl.multiple_of(blk * n, blk), 2 * blk)
            s = _dot_nt(q, kd_ref[win, :])
            yield
            s = s + bias_ref[jnp.where(i == 0, 0, 1)]
            m = jnp.max(s, axis=-1, keepdims=True)
            pv = _dot(jnp.exp((s - m).astype(BF16)), vd_ref[win, :])
            yield
            l = pv[:, HEAD_DIM:]
            op_ref[p, rows, :] = pv[:, :HEAD_DIM] / l
            lse_ref[p, rows, :] = m + jnp.log(l)

        def attend_group(t, _, attend=attend):
            _interleave([attend(t * group + u) for u in range(group)])
            return 0

        lax.fori_loop(0, n_blocks // group, attend_group, 0)

    g = g_ref[...]

    def merge(n, _):
        rows = pl.ds(pl.multiple_of(n * blk, blk), blk)
        lses = [lse_ref[p, rows, :] for p in range(len(DILATIONS))]
        top = functools.reduce(jnp.maximum, lses)
        ws = [jnp.exp(x - top) for x in lses]
        den = functools.reduce(lambda a, b_: a + b_, ws)
        num = functools.reduce(lambda a, b_: a + b_,
                               [op_ref[p, rows, :] * ws[p] for p in range(len(DILATIONS))])
        o_ref[rows, :] = _rms(num * (1.0 / den), g).astype(o_ref.dtype)
        return 0

    lax.fori_loop(0, n_blocks, merge, 0, unroll=2 * unroll)


def dil_attention(pd, g, group=32, unroll=4):
    b, s, _ = pd.shape
    n_pat = len(DILATIONS)
    assert DILATIONS == (1, 4, 16) and (s // DIL_SPAN) % group == 0
    return pl.pallas_call(
        functools.partial(_dil_kernel, seq=s, group=group, unroll=unroll),
        grid=(b, N_DIL),
        in_specs=[pl.BlockSpec((None, s, HEAD_DIM), lambda b_, h: (b_, 0, h)),
                  pl.BlockSpec((None, s, HEAD_DIM), lambda b_, h: (b_, 0, N_DIL + h)),
                  pl.BlockSpec((None, s, HEAD_DIM), lambda b_, h: (b_, 0, 2 * N_DIL + h)),
                  pl.BlockSpec((1, HEAD_DIM), lambda b_, h: (0, 0))],
        out_specs=pl.BlockSpec((None, s, HEAD_DIM), lambda b_, h: (b_, 0, h)),
        out_shape=jax.ShapeDtypeStruct((b, s, W_DIL), BF16),
        scratch_shapes=[pltpu.VMEM((s, HEAD_DIM), BF16),
                        pltpu.VMEM((s + DIL_SPAN, HEAD_DIM), BF16),
                        pltpu.VMEM((s + DIL_SPAN, 2 * HEAD_DIM), BF16),
                        pltpu.VMEM((s, HEAD_DIM), F32),
                        pltpu.VMEM((s, HEAD_DIM), F32),
                        pltpu.VMEM((s, HEAD_DIM), F32),
                        pltpu.VMEM((2, DIL_SPAN, 2 * DIL_SPAN), F32),
                        pltpu.VMEM((n_pat, s, HEAD_DIM), F32),
                        pltpu.VMEM((n_pat, s, HEAD_DIM), F32)],
        compiler_params=_params("parallel", "arbitrary"),
        name="dil_attention",
    )(pd, pd, pd, g.reshape(1, HEAD_DIM))


def kernel(x, positions, norm_mix_g, w_in, lambda_q1, lambda_k1, lambda_q2, lambda_k2, g_sb_out, g_diff_out, g_dil_out, w_out, norm_ffn_g, w_gate, w_up, w_down, norm_final_g):
    batch, seq, d = x.shape
    m = batch * seq
    depth = w_in.shape[0]
    xf = x.reshape(m, d)
    tabs = rope_tables(positions.reshape(m, 1))
    w_in_bf = cast_bf16(w_in, 0)
    for layer in range(depth):
        pa, pd = in_proj(xf, norm_mix_g[layer], w_in_bf, tabs)
        pa = pa.reshape(batch, seq, W_A)
        pd = pd.reshape(batch, seq, 3 * W_DIL)
        lambda_init = 0.8 - 0.6 * math.exp(-0.3 * layer)
        lam_vecs = jnp.stack([lambda_q1[layer], lambda_k1[layer],
                              lambda_q2[layer], lambda_k2[layer]]).astype(F32)
        o_sb = sb_attention(pa, g_sb_out[layer]).reshape(m, W_SB)
        side = [(w, layer) for w in (w_out, w_gate, w_up, w_down)]
        side += [(w_in, layer + 1)] if layer + 1 < depth else []
        o_df, converted = diff_attention(pa, lam_vecs, g_diff_out[layer], lambda_init, side)
        o_df = o_df.reshape(m, W_DIFF)
        w_out_bf, w_gate_bf, w_up_bf, w_down_bf = converted[:4]
        w_in_bf = converted[4] if layer + 1 < depth else None
        o_dl = dil_attention(pd, g_dil_out[layer]).reshape(m, W_DIL)
        xf = mlp_block(o_sb, o_df, o_dl, w_out_bf, xf, norm_ffn_g[layer], w_gate_bf, w_up_bf, w_down_bf,
                       final_g=norm_final_g if layer == depth - 1 else None)
    return xf.reshape(batch, seq, d)
```

```python
import functools
import math

import jax
import jax.numpy as jnp
import numpy as np
from jax import lax
from jax.experimental import pallas as pl
from jax.experimental.pallas import tpu as pltpu

F32 = jnp.float32
BF16 = jnp.bfloat16

HEAD_DIM = 128
N_SB = 4
N_DIFF = 4
N_DIL = 8
W_SB = N_SB * HEAD_DIM
W_DIFF = N_DIFF * HEAD_DIM
W_DIL = N_DIL * HEAD_DIM
DIFF_QK = HEAD_DIM // 2
ROPE_THETA = 500000.0
ROPE_FRACTION = 0.25
DILATIONS = (1, 4, 16)
DIL_SPAN = 128
NORM_EPS = 1e-6
LANES = 128
VMEM_LIMIT = 56 * 1024 * 1024

NEG_INF = float("-inf")
LOG2_E = math.log2(math.e)
SB_DEAD_MASS = 106.0


def _params(*sem):
    return pltpu.CompilerParams(dimension_semantics=sem, vmem_limit_bytes=VMEM_LIMIT)


def _dot(a, b):
    return jnp.dot(a, b, preferred_element_type=F32)


def _dot_nt(a, b):
    return lax.dot_general(a, b, (((1,), (1,)), ((), ())), preferred_element_type=F32)


def _tile(x, n):
    return x if n == 1 else jnp.concatenate([x] * n, axis=1)


def _rms(x, g):
    return x * lax.rsqrt(jnp.mean(x * x, axis=-1, keepdims=True) + NORM_EPS) * g


def _side_cast_specs(side, n_steps, step_index):
    in_specs, out_specs, out_shape = [], [], []
    for w, layer in side:
        _, r, c = w.shape
        rows = r // n_steps
        assert rows * n_steps == r and rows % 16 == 0
        in_specs.append(pl.BlockSpec((None, rows, c), lambda *g, layer=layer: (layer, step_index(*g), 0)))
        out_specs.append(pl.BlockSpec((rows, c), lambda *g: (step_index(*g), 0)))
        out_shape.append(jax.ShapeDtypeStruct((r, c), BF16))
    return in_specs, out_specs, out_shape


def _rope_lane_consts():
    pos = np.arange(LANES)
    rows = []
    for chunk in (DIFF_QK, HEAD_DIM):
        rot = int(chunk * ROPE_FRACTION)
        half = rot // 2
        inv_freq = ROPE_THETA ** (-jnp.arange(half, dtype=F32) / half)
        per_chunk = jnp.concatenate([inv_freq, inv_freq, jnp.zeros((chunk - rot,), F32)])
        freq = jnp.tile(per_chunk, LANES // chunk)
        lo = np.where(pos % chunk < half, -1.0, 0.0)
        hi = np.where((pos % chunk >= half) & (pos % chunk < rot), 1.0, 0.0)
        rows += [freq, jnp.asarray(lo, F32), jnp.asarray(hi, F32)]
    rows += [jnp.zeros((LANES,), F32)] * 2
    return jnp.stack(rows)


def _rope_table_kernel(*refs, n_side):
    pos_ref, c_ref = refs[:2]
    side_in, o_ref, side_out = refs[2:2 + n_side], refs[2 + n_side], refs[3 + n_side:]
    for src, dst in zip(side_in, side_out):
        dst[...] = src[...].astype(BF16)
    p = pos_ref[...].astype(F32)
    for k in range(2):
        ang = p * c_ref[3 * k:3 * k + 1, :]
        s = jnp.sin(ang)
        o_ref[3 * k] = jnp.cos(ang)
        o_ref[3 * k + 1] = s * c_ref[3 * k + 1:3 * k + 2, :]
        o_ref[3 * k + 2] = s * c_ref[3 * k + 2:3 * k + 3, :]


def rope_tables(positions, side=(), tm=512):
    m = positions.shape[0]
    side_in, side_out, side_shape = _side_cast_specs(side, m // tm, lambda i: i)
    out = pl.pallas_call(
        functools.partial(_rope_table_kernel, n_side=len(side)),
        grid=(m // tm,),
        in_specs=[pl.BlockSpec((tm, 1), lambda i: (i, 0)),
                  pl.BlockSpec((8, LANES), lambda i: (0, 0))] + side_in,
        out_specs=[pl.BlockSpec((6, tm, LANES), lambda i: (0, i, 0))] + side_out,
        out_shape=[jax.ShapeDtypeStruct((6, m, LANES), F32)] + side_shape,
        compiler_params=_params("parallel"),
        name="rope_tables",
    )(positions, _rope_lane_consts(), *[w for w, _ in side])
    return out[0], out[1:]


ROPE_NONE, ROPE_DIFF, ROPE_DIL = 0, 1, 2
_ROPE_SHIFT = {ROPE_DIFF: int(DIFF_QK * ROPE_FRACTION) // 2, ROPE_DIL: int(HEAD_DIM * ROPE_FRACTION) // 2}
W_A = 3 * (W_SB + W_DIFF)
PROJ_TN = 512
_PROJ_KINDS = (ROPE_NONE,) * 3 + (ROPE_DIFF,) * 2 + (ROPE_NONE,) + (ROPE_DIL,) * 4 + (ROPE_NONE,) * 2


def _resident(shape):
    return pl.BlockSpec(shape, lambda *_: (0,) * len(shape), pipeline_mode=pl.Buffered(1))


def _proj_kernel(x_ref, g_ref, w_ref, tab_ref, oa_ref, od_ref, h_ref):
    h_ref[...] = _rms(x_ref[...], g_ref[...]).astype(BF16)
    tn = PROJ_TN
    for idx, kind in enumerate(_PROJ_KINDS):
        acc = _dot(h_ref[...], w_ref[:, idx * tn:(idx + 1) * tn])
        o_ref, col = (oa_ref, idx * tn) if idx * tn < W_A else (od_ref, idx * tn - W_A)
        if kind == ROPE_NONE:
            o_ref[:, col:col + tn] = acc.astype(o_ref.dtype)
            continue
        base = 3 * (kind - 1)
        shift = _ROPE_SHIFT[kind]
        c, lo, hi = tab_ref[base], tab_ref[base + 1], tab_ref[base + 2]
        for q in range(tn // LANES):
            xk = acc[:, q * LANES:(q + 1) * LANES]
            rot = xk * c + pltpu.roll(xk, LANES - shift, 1) * lo + pltpu.roll(xk, shift, 1) * hi
            o_ref[:, col + q * LANES:col + (q + 1) * LANES] = rot.astype(o_ref.dtype)


def in_proj(x, g, w, tabs, tm=256):
    m, d = x.shape
    n = w.shape[1]
    assert n == len(_PROJ_KINDS) * PROJ_TN
    return pl.pallas_call(
        _proj_kernel,
        grid=(m // tm,),
        in_specs=[pl.BlockSpec((tm, d), lambda i: (i, 0)),
                  _resident((1, d)),
                  _resident((d, n)),
                  pl.BlockSpec((6, tm, LANES), lambda i: (0, i, 0))],
        out_specs=[pl.BlockSpec((tm, W_A), lambda i: (i, 0)),
                   pl.BlockSpec((tm, n - W_A), lambda i: (i, 0))],
        out_shape=[jax.ShapeDtypeStruct((m, W_A), BF16),
                   jax.ShapeDtypeStruct((m, n - W_A), F32)],
        scratch_shapes=[pltpu.VMEM((tm, d), BF16)],
        compiler_params=_params("parallel"),
        name="in_proj",
    )(x, g.reshape(1, d), w, tabs)


def _mlp_kernel(*refs, final_norm):
    if final_norm:
        sb_ref, df_ref, dl_ref, wo_ref, x_ref, g_ref, wg_ref, wu_ref, wd_ref, gf_ref, o_ref, h_ref = refs
    else:
        sb_ref, df_ref, dl_ref, wo_ref, x_ref, g_ref, wg_ref, wu_ref, wd_ref, o_ref, h_ref = refs
    j = pl.program_id(1)

    @pl.when(j == 0)
    def _():
        mixed = (_dot(sb_ref[...], wo_ref[0:W_SB, :])
                 + _dot(df_ref[...], wo_ref[W_SB:W_SB + W_DIFF, :])
                 + _dot(dl_ref[...], wo_ref[W_SB + W_DIFF:, :]))
        xn = x_ref[...] + mixed
        o_ref[...] = xn
        h_ref[...] = _rms(xn, g_ref[...]).astype(BF16)

    h = h_ref[...]
    gate = _dot(h, wg_ref[...])
    up = _dot(h, wu_ref[...])
    act = (gate * jax.nn.sigmoid(gate)) * up
    o_ref[...] += _dot(act.astype(BF16), wd_ref[...])

    if final_norm:
        @pl.when(j == pl.num_programs(1) - 1)
        def _():
            o_ref[...] = _rms(o_ref[...], gf_ref[...])


def mlp_block(o_sb, o_df, o_dl, w_out, x, g, wg, wu, wd, final_g=None, tm=512, tf=512):
    m, d = x.shape
    f = wg.shape[1]
    final_norm = final_g is not None
    in_specs = [pl.BlockSpec((tm, W_SB), lambda i, j: (i, 0)),
                pl.BlockSpec((tm, W_DIFF), lambda i, j: (i, 0)),
                pl.BlockSpec((tm, W_DIL), lambda i, j: (i, 0)),
                _resident(w_out.shape),
                pl.BlockSpec((tm, d), lambda i, j: (i, 0)),
                _resident((1, d)),
                pl.BlockSpec((d, tf), lambda i, j: (0, j)),
                pl.BlockSpec((d, tf), lambda i, j: (0, j)),
                pl.BlockSpec((tf, d), lambda i, j: (j, 0))]
    args = [o_sb, o_df, o_dl, w_out, x, g.reshape(1, d), wg, wu, wd]
    if final_norm:
        in_specs.append(_resident((1, d)))
        args.append(final_g.reshape(1, d))
    return pl.pallas_call(
        functools.partial(_mlp_kernel, final_norm=final_norm),
        grid=(m // tm, f // tf),
        in_specs=in_specs,
        out_specs=pl.BlockSpec((tm, d), lambda i, j: (i, 0)),
        out_shape=jax.ShapeDtypeStruct((m, d), F32),
        scratch_shapes=[pltpu.VMEM((tm, d), BF16)],
        compiler_params=_params("parallel", "arbitrary"),
        name="mlp_block",
    )(*args)


def _causal_sweep(i, n_chain, step, keys_per_iter, wide=False, exhausted=None):
    assert n_chain % keys_per_iter == 0
    base = n_chain * i
    if wide:
        _interleave([step(c, base + c, True) for c in range(n_chain)])
        _interleave([step(c, base, False, c) for c in range(1, n_chain)])
    else:
        _interleave([step(c, base + d, c == d) for d in range(n_chain - 1, -1, -1) for c in range(d, n_chain)])

    def body(t, _):
        first = base - 1 - t * keys_per_iter
        if wide:
            _interleave([step(c, first - (keys_per_iter - 1), False, keys_per_iter) for c in range(n_chain)])
        else:
            _interleave([step(c, first - u, False) for u in range(keys_per_iter) for c in range(n_chain)])
        return 0

    trips = (n_chain // keys_per_iter) * i
    if exhausted is None:
        lax.fori_loop(0, trips, body, 0)
        return

    def more(state):
        t, done = state
        return (t < trips) & jnp.logical_not(done)

    def advance(state):
        body(state[0], 0)
        return state[0] + 1, exhausted()

    lax.while_loop(more, advance, (0, exhausted()))


def _interleave(steps):
    steps = list(steps)
    while steps:
        alive = []
        for g in steps:
            try:
                next(g)
                alive.append(g)
            except StopIteration:
                pass
        steps = alive


def _from_key_matrix(blk):
    tri = np.tril(np.ones((blk, blk), np.float32))
    return jnp.asarray(np.concatenate([tri, tri], axis=0), BF16)


def _sb_kernel(q_ref, k_ref, v_ref, later_ref, g_ref, o_ref, qs_ref, acc_ref, carry_ref, *,
               blk, n_chain, keys_per_iter):
    i = pl.program_id(2)
    scale = HEAD_DIM ** -0.5
    qs_ref[...] = (q_ref[...].astype(F32) * scale).astype(BF16)
    acc_ref[...] = jnp.zeros_like(acc_ref)
    carry_ref[...] = jnp.zeros_like(carry_ref)
    row = lax.broadcasted_iota(jnp.int32, (blk, blk), 0)
    col = lax.broadcasted_iota(jnp.int32, (blk, blk), 1)
    strict = col < row

    def step(c, j, diag):
        rows = slice(c * blk, (c + 1) * blk)
        start = pl.multiple_of(j * blk, blk)
        k = k_ref[pl.ds(start, blk), :]
        v = v_ref[pl.ds(start, blk), :]
        z = _dot_nt(qs_ref[rows, :], k)
        yield
        sp = jnp.maximum(z, 0.0) + jnp.log(1.0 + jnp.exp2(jnp.abs(z) * -LOG2_E))
        spm = jnp.where(strict, sp, 0.0) if diag else sp
        hi = spm.astype(BF16)
        lo = (spm - hi.astype(F32)).astype(BF16)
        from_key = _dot(jnp.concatenate([hi, lo], axis=1), later_ref[...])
        yield
        loga = z - (from_key + _tile(carry_ref[rows, :], blk // LANES))
        if diag:
            loga = jnp.where(strict, loga, NEG_INF)
        pv = _dot(jnp.exp(loga).astype(BF16), v)
        carry_ref[rows, :] += from_key[:, 0:1]
        yield
        acc_ref[rows, :] += pv

    _causal_sweep(i, n_chain, step, keys_per_iter,
                  exhausted=lambda: jnp.min(carry_ref[...]) >= SB_DEAD_MASS)
    o_ref[...] = _rms(acc_ref[...], g_ref[...]).astype(o_ref.dtype)


def sb_attention(pa, g, blk=256, n_chain=4, keys_per_iter=1):
    b, s, _ = pa.shape
    bq = blk * n_chain
    return pl.pallas_call(
        functools.partial(_sb_kernel, blk=blk, n_chain=n_chain, keys_per_iter=keys_per_iter),
        grid=(b, N_SB, s // bq),
        in_specs=[pl.BlockSpec((None, bq, HEAD_DIM), lambda b_, h, i: (b_, i, h)),
                  pl.BlockSpec((None, s, HEAD_DIM), lambda b_, h, i: (b_, 0, N_SB + h)),
                  pl.BlockSpec((None, s, HEAD_DIM), lambda b_, h, i: (b_, 0, 2 * N_SB + h)),
                  pl.BlockSpec((2 * blk, blk), lambda b_, h, i: (0, 0)),
                  pl.BlockSpec((1, HEAD_DIM), lambda b_, h, i: (0, 0))],
        out_specs=pl.BlockSpec((None, bq, HEAD_DIM), lambda b_, h, i: (b_, i, h)),
        out_shape=jax.ShapeDtypeStruct((b, s, W_SB), BF16),
        scratch_shapes=[pltpu.VMEM((bq, HEAD_DIM), BF16),
                        pltpu.VMEM((bq, HEAD_DIM), F32),
                        pltpu.VMEM((bq, LANES), F32)],
        compiler_params=_params("parallel", "parallel", "arbitrary"),
        name="sb_attention",
    )(pa, pa, pa, _from_key_matrix(blk), g.reshape(1, HEAD_DIM))


def _diff_kernel(*refs, blk, n_chain, keys_per_iter, lambda_init, n_side):
    q_ref, k_ref, v_ref, lam_ref, g_ref = refs[:5]
    side_in, o_ref = refs[5:5 + n_side], refs[5 + n_side]
    side_out = refs[6 + n_side:6 + 2 * n_side]
    qs_ref, vx_ref, m_ref, acc_ref = refs[6 + 2 * n_side:]
    for src, dst in zip(side_in, side_out):
        dst[...] = src[...].astype(BF16)
    i = pl.program_id(2)
    seq = v_ref.shape[0]

    @pl.when(i == 0)
    def _():
        vx_ref[:, 0:HEAD_DIM] = v_ref[...]
        vx_ref[:, HEAD_DIM:] = jnp.ones((seq, HEAD_DIM), BF16)

    scale = DIFF_QK ** -0.5
    q = q_ref[...].astype(F32) * scale
    lc = lax.broadcasted_iota(jnp.int32, (blk, HEAD_DIM), 1)
    for c in range(n_chain):
        qc = q[c * blk:(c + 1) * blk]
        qs_ref[(2 * c) * blk:(2 * c + 1) * blk, :] = jnp.where(lc < DIFF_QK, qc, 0.0).astype(BF16)
        qs_ref[(2 * c + 1) * blk:(2 * c + 2) * blk, :] = jnp.where(lc >= DIFF_QK, qc, 0.0).astype(BF16)
    m_ref[...] = jnp.full(m_ref.shape, NEG_INF, F32)
    acc_ref[...] = jnp.zeros_like(acc_ref)
    row = lax.broadcasted_iota(jnp.int32, (blk, blk), 0)
    col = lax.broadcasted_iota(jnp.int32, (blk, blk), 1)
    causal = col <= row

    def step(c, j, diag, width=1):
        start = pl.multiple_of(j * blk, blk)
        k = k_ref[pl.ds(start, width * blk), :]
        vx = vx_ref[pl.ds(start, width * blk), :]
        rows = [slice((2 * c + e) * blk, (2 * c + e + 1) * blk) for e in range(2)]
        s = [_dot_nt(qs_ref[r, :], k) for r in rows]
        yield
        alpha, pv = [], []
        for e, r in enumerate(rows):
            se = jnp.where(causal, s[e], NEG_INF) if diag else s[e]
            m_prev = m_ref[r, :]
            m_new = jnp.maximum(m_prev, jnp.max(se, axis=-1, keepdims=True))
            alpha.append(jnp.exp(m_prev - m_new))
            p = jnp.exp((se - _tile(m_new, width * blk // LANES)).astype(BF16))
            m_ref[r, :] = m_new
            pv.append(_dot(p, vx))
        yield
        for e, r in enumerate(rows):
            acc_ref[r, :] = _tile(alpha[e], 2) * acc_ref[r, :] + pv[e]

    _causal_sweep(i, n_chain, step, keys_per_iter, wide=True)
    lq = lam_ref[...]
    lam = (jnp.exp(jnp.sum(lq[0:1] * lq[1:2], axis=-1, keepdims=True))
           - jnp.exp(jnp.sum(lq[2:3] * lq[3:4], axis=-1, keepdims=True)) + lambda_init)
    g = g_ref[...]
    for c in range(n_chain):
        o = []
        for e in range(2):
            acc = acc_ref[(2 * c + e) * blk:(2 * c + e + 1) * blk, :]
            o.append(acc[:, :HEAD_DIM] / acc[:, HEAD_DIM:])
        out = o[0] - lam * o[1]
        o_ref[c * blk:(c + 1) * blk, :] = (_rms(out, g) * (1.0 - lambda_init)).astype(o_ref.dtype)


def diff_attention(pa, lam_vecs, g, lambda_init, side=(), blk=256, n_chain=4, keys_per_iter=4):
    b, s, _ = pa.shape
    c0 = 3 * N_SB
    bq = blk * n_chain
    n_i = s // bq
    side_in, side_out, side_shape = _side_cast_specs(
        side, b * N_DIFF * n_i, lambda b_, h, i: (b_ * N_DIFF + h) * n_i + i)
    out = pl.pallas_call(
        functools.partial(_diff_kernel, blk=blk, n_chain=n_chain, keys_per_iter=keys_per_iter,
                          lambda_init=lambda_init, n_side=len(side)),
        grid=(b, N_DIFF, n_i),
        in_specs=[pl.BlockSpec((None, bq, HEAD_DIM), lambda b_, h, i: (b_, i, c0 + h)),
                  pl.BlockSpec((None, s, HEAD_DIM), lambda b_, h, i: (b_, 0, c0 + N_DIFF + h)),
                  pl.BlockSpec((None, s, HEAD_DIM), lambda b_, h, i: (b_, 0, c0 + 2 * N_DIFF + h)),
                  pl.BlockSpec((4, DIFF_QK), lambda b_, h, i: (0, 0)),
                  pl.BlockSpec((1, HEAD_DIM), lambda b_, h, i: (0, 0))] + side_in,
        out_specs=[pl.BlockSpec((None, bq, HEAD_DIM), lambda b_, h, i: (b_, i, h))] + side_out,
        out_shape=[jax.ShapeDtypeStruct((b, s, W_DIFF), BF16)] + side_shape,
        scratch_shapes=[pltpu.VMEM((2 * bq, HEAD_DIM), BF16),
                        pltpu.VMEM((s, 2 * HEAD_DIM), BF16),
                        pltpu.VMEM((2 * bq, LANES), F32),
                        pltpu.VMEM((2 * bq, 2 * HEAD_DIM), F32)],
        compiler_params=_params("parallel", "parallel", "arbitrary"),
        name="diff_attention",
    )(pa, pa, pa, lam_vecs, g.reshape(1, HEAD_DIM), *[w for w, _ in side])
    return out[0], out[1:]


def _dil_kernel(q_ref, k_ref, v_ref, g_ref, o_ref, qd_ref, kd_ref, vd_ref, q4_ref, k4_ref, v4_ref,
                bias_ref, op_ref, lse_ref, *, seq, group, unroll):
    blk = DIL_SPAN
    scale = HEAD_DIM ** -0.5
    n_blocks = seq // blk
    kd_ref[0:blk, :] = jnp.zeros((blk, HEAD_DIM), BF16)
    vd_ref[0:blk, 0:HEAD_DIM] = jnp.zeros((blk, HEAD_DIM), BF16)
    vd_ref[:, HEAD_DIM:] = jnp.ones((seq + blk, HEAD_DIM), BF16)
    ii = lax.broadcasted_iota(jnp.int32, (blk, 2 * blk), 0)
    jj = lax.broadcasted_iota(jnp.int32, (blk, 2 * blk), 1)
    dist = blk + ii - jj
    in_window = (dist >= 0) & (dist <= DIL_SPAN)
    bias_ref[0] = jnp.where(in_window & (jj >= blk), 0.0, NEG_INF)
    bias_ref[1] = jnp.where(in_window, 0.0, NEG_INF)
    quarter = seq // 4

    for p, r in enumerate(DILATIONS):
        per_seq = n_blocks // r
        shift = per_seq.bit_length() - 1

        def block_of(n, per_seq=per_seq, shift=shift):
            return lax.shift_right_logical(n, shift), n & (per_seq - 1)

        def natural_rows(n, r=r, block_of=block_of):
            c, i = block_of(n)
            src = c + r * blk * i
            if r == 1:
                return pl.ds(pl.multiple_of(src, blk), blk)
            return pl.ds(src, blk, stride=r)

        def gather(n, _, r=r, block_of=block_of, natural_rows=natural_rows):
            here = pl.ds(pl.multiple_of(blk * n, blk), blk)
            if r == 16:
                c, i = block_of(n)
                rows = pl.ds((c & 3) * quarter + lax.shift_right_logical(c, 2) + 4 * blk * i, blk, stride=4)
                q, k, v = q4_ref[rows, :], k4_ref[rows, :], v4_ref[rows, :]
            else:
                rows = natural_rows(n)
                q, k, v = q_ref[rows, :], k_ref[rows, :], v_ref[rows, :]
            if r == 4:
                q4_ref[here, :], k4_ref[here, :], v4_ref[here, :] = q, k, v
            dst = pl.ds(pl.multiple_of(blk * (n + 1), blk), blk)
            kd_ref[dst, :] = k.astype(BF16)
            vd_ref[dst, 0:HEAD_DIM] = v.astype(BF16)
            qd_ref[here, :] = (q * scale).astype(BF16)
            return 0

        lax.fori_loop(0, n_blocks, gather, 0, unroll=unroll)

        def attend(n, p=p, block_of=block_of, natural_rows=natural_rows):
            _, i = block_of(n)
            rows = natural_rows(n)
            q = qd_ref[pl.ds(pl.multiple_of(blk * n, blk), blk), :]
            win = pl.ds(pl.multiple_of(blk * n, blk), 2 * blk)
            s = _dot_nt(q, kd_ref[win, :])
            yield
            s = s + bias_ref[jnp.where(i == 0, 0, 1)]
            m = jnp.max(s, axis=-1, keepdims=True)
            pv = _dot(jnp.exp((s - m).astype(BF16)), vd_ref[win, :])
            yield
            l = pv[:, HEAD_DIM:]
            op_ref[p, rows, :] = pv[:, :HEAD_DIM] / l
            lse_ref[p, rows, :] = m + jnp.log(l)

        def attend_group(t, _, attend=attend):
            _interleave([attend(t * group + u) for u in range(group)])
            return 0

        lax.fori_loop(0, n_blocks // group, attend_group, 0)

    g = g_ref[...]

    def merge(n, _):
        rows = pl.ds(pl.multiple_of(n * blk, blk), blk)
        lses = [lse_ref[p, rows, :] for p in range(len(DILATIONS))]
        top = functools.reduce(jnp.maximum, lses)
        ws = [jnp.exp(x - top) for x in lses]
        den = functools.reduce(lambda a, b_: a + b_, ws)
        num = functools.reduce(lambda a, b_: a + b_,
                               [op_ref[p, rows, :] * ws[p] for p in range(len(DILATIONS))])
        o_ref[rows, :] = _rms(num * (1.0 / den), g).astype(o_ref.dtype)
        return 0

    lax.fori_loop(0, n_blocks, merge, 0, unroll=2 * unroll)


def dil_attention(pd, g, group=32, unroll=4):
    b, s, _ = pd.shape
    n_pat = len(DILATIONS)
    assert DILATIONS == (1, 4, 16) and (s // DIL_SPAN) % group == 0
    return pl.pallas_call(
        functools.partial(_dil_kernel, seq=s, group=group, unroll=unroll),
        grid=(b, N_DIL),
        in_specs=[pl.BlockSpec((None, s, HEAD_DIM), lambda b_, h: (b_, 0, h)),
                  pl.BlockSpec((None, s, HEAD_DIM), lambda b_, h: (b_, 0, N_DIL + h)),
                  pl.BlockSpec((None, s, HEAD_DIM), lambda b_, h: (b_, 0, 2 * N_DIL + h)),
                  pl.BlockSpec((1, HEAD_DIM), lambda b_, h: (0, 0))],
        out_specs=pl.BlockSpec((None, s, HEAD_DIM), lambda b_, h: (b_, 0, h)),
        out_shape=jax.ShapeDtypeStruct((b, s, W_DIL), BF16),
        scratch_shapes=[pltpu.VMEM((s, HEAD_DIM), BF16),
                        pltpu.VMEM((s + DIL_SPAN, HEAD_DIM), BF16),
                        pltpu.VMEM((s + DIL_SPAN, 2 * HEAD_DIM), BF16),
                        pltpu.VMEM((s, HEAD_DIM), F32),
                        pltpu.VMEM((s, HEAD_DIM), F32),
                        pltpu.VMEM((s, HEAD_DIM), F32),
                        pltpu.VMEM((2, DIL_SPAN, 2 * DIL_SPAN), F32),
                        pltpu.VMEM((n_pat, s, HEAD_DIM), F32),
                        pltpu.VMEM((n_pat, s, HEAD_DIM), F32)],
        compiler_params=_params("parallel", "arbitrary"),
        name="dil_attention",
    )(pd, pd, pd, g.reshape(1, HEAD_DIM))


def kernel(x, positions, norm_mix_g, w_in, lambda_q1, lambda_k1, lambda_q2, lambda_k2, g_sb_out, g_diff_out, g_dil_out, w_out, norm_ffn_g, w_gate, w_up, w_down, norm_final_g):
    batch, seq, d = x.shape
    m = batch * seq
    depth = w_in.shape[0]
    xf = x.reshape(m, d)
    tabs, (w_in_bf,) = rope_tables(positions.reshape(m, 1), [(w_in, 0)])
    for layer in range(depth):
        pa, pd = in_proj(xf, norm_mix_g[layer], w_in_bf, tabs)
        pa = pa.reshape(batch, seq, W_A)
        pd = pd.reshape(batch, seq, 3 * W_DIL)
        lambda_init = 0.8 - 0.6 * math.exp(-0.3 * layer)
        lam_vecs = jnp.stack([lambda_q1[layer], lambda_k1[layer],
                              lambda_q2[layer], lambda_k2[layer]]).astype(F32)
        o_sb = sb_attention(pa, g_sb_out[layer]).reshape(m, W_SB)
        side = [(w, layer) for w in (w_out, w_gate, w_up, w_down)]
        side += [(w_in, layer + 1)] if layer + 1 < depth else []
        o_df, converted = diff_attention(pa, lam_vecs, g_diff_out[layer], lambda_init, side)
        o_df = o_df.reshape(m, W_DIFF)
        w_out_bf, w_gate_bf, w_up_bf, w_down_bf = converted[:4]
        w_in_bf = converted[4] if layer + 1 < depth else None
        o_dl = dil_attention(pd, g_dil_out[layer]).reshape(m, W_DIL)
        xf = mlp_block(o_sb, o_df, o_dl, w_out_bf, xf, norm_ffn_g[layer], w_gate_bf, w_up_bf, w_down_bf,
                       final_g=norm_final_g if layer == depth - 1 else None)
    return xf.reshape(batch, seq, d)
```

```python
import functools
import math

import jax
import jax.numpy as jnp
import numpy as np
from jax import lax
from jax.experimental import pallas as pl
from jax.experimental.pallas import tpu as pltpu

F32 = jnp.float32
BF16 = jnp.bfloat16

HEAD_DIM = 128
N_SB = 4
N_DIFF = 4
N_DIL = 8
W_SB = N_SB * HEAD_DIM
W_DIFF = N_DIFF * HEAD_DIM
W_DIL = N_DIL * HEAD_DIM
DIFF_QK = HEAD_DIM // 2
ROPE_THETA = 500000.0
ROPE_FRACTION = 0.25
DILATIONS = (1, 4, 16)
DIL_SPAN = 128
NORM_EPS = 1e-6
LANES = 128
VMEM_LIMIT = 56 * 1024 * 1024

NEG_INF = float("-inf")
LOG2_E = math.log2(math.e)
SB_DEAD_MASS = 106.0


def _params(*sem):
    return pltpu.CompilerParams(dimension_semantics=sem, vmem_limit_bytes=VMEM_LIMIT)


def _dot(a, b):
    return jnp.dot(a, b, preferred_element_type=F32)


def _dot_nt(a, b):
    return lax.dot_general(a, b, (((1,), (1,)), ((), ())), preferred_element_type=F32)


def _tile(x, n):
    return x if n == 1 else jnp.concatenate([x] * n, axis=1)


def _rms(x, g):
    return x * lax.rsqrt(jnp.mean(x * x, axis=-1, keepdims=True) + NORM_EPS) * g


def _side_cast_specs(side, n_steps, step_index):
    in_specs, out_specs, out_shape = [], [], []
    for w, layer in side:
        _, r, c = w.shape
        rows = r // n_steps
        assert rows * n_steps == r and rows % 16 == 0
        in_specs.append(pl.BlockSpec((None, rows, c), lambda *g, layer=layer: (layer, step_index(*g), 0)))
        out_specs.append(pl.BlockSpec((rows, c), lambda *g: (step_index(*g), 0)))
        out_shape.append(jax.ShapeDtypeStruct((r, c), BF16))
    return in_specs, out_specs, out_shape


_ROPE_LAYOUTS = ((DIFF_QK, int(DIFF_QK * ROPE_FRACTION) // 2), (HEAD_DIM, int(HEAD_DIM * ROPE_FRACTION) // 2))


def _rope_lane_consts():
    freqs = [ROPE_THETA ** (-jnp.arange(half, dtype=F32) / half) for _, half in _ROPE_LAYOUTS]
    used = sum(half for _, half in _ROPE_LAYOUTS)
    row = jnp.concatenate(freqs + [jnp.zeros((LANES - used,), F32)])
    return jnp.concatenate([row[None, :], jnp.zeros((7, LANES), F32)])


def _rope_table_kernel(*refs, n_side):
    pos_ref, c_ref = refs[:2]
    side_in, o_ref, side_out = refs[2:2 + n_side], refs[2 + n_side], refs[3 + n_side:]
    for src, dst in zip(side_in, side_out):
        dst[...] = src[...].astype(BF16)
    ang = pos_ref[...].astype(F32) * c_ref[0:1, :]
    cos, sin = jnp.cos(ang), jnp.sin(ang)
    lane = lax.broadcasted_iota(jnp.int32, ang.shape, 1)
    src = 0
    for k, (chunk, half) in enumerate(_ROPE_LAYOUTS):
        c_tab = jnp.ones_like(ang)
        lo_tab = jnp.zeros_like(ang)
        hi_tab = jnp.zeros_like(ang)
        for start in range(0, LANES, chunk):
            for part, dst in enumerate((start, start + half)):
                here = (lane >= dst) & (lane < dst + half)
                shift = (dst - src) % LANES
                c_here = pltpu.roll(cos, shift, 1) if shift else cos
                s_here = pltpu.roll(sin, shift, 1) if shift else sin
                c_tab = jnp.where(here, c_here, c_tab)
                if part == 0:
                    lo_tab = jnp.where(here, -s_here, lo_tab)
                else:
                    hi_tab = jnp.where(here, s_here, hi_tab)
        o_ref[3 * k], o_ref[3 * k + 1], o_ref[3 * k + 2] = c_tab, lo_tab, hi_tab
        src += half


def rope_tables(positions, side=(), tm=512):
    m = positions.shape[0]
    side_in, side_out, side_shape = _side_cast_specs(side, m // tm, lambda i: i)
    out = pl.pallas_call(
        functools.partial(_rope_table_kernel, n_side=len(side)),
        grid=(m // tm,),
        in_specs=[pl.BlockSpec((tm, 1), lambda i: (i, 0)),
                  pl.BlockSpec((8, LANES), lambda i: (0, 0))] + side_in,
        out_specs=[pl.BlockSpec((6, tm, LANES), lambda i: (0, i, 0))] + side_out,
        out_shape=[jax.ShapeDtypeStruct((6, m, LANES), F32)] + side_shape,
        compiler_params=_params("parallel"),
        name="rope_tables",
    )(positions, _rope_lane_consts(), *[w for w, _ in side])
    return out[0], out[1:]


ROPE_NONE, ROPE_DIFF, ROPE_DIL = 0, 1, 2
_ROPE_SHIFT = {ROPE_DIFF: _ROPE_LAYOUTS[0][1], ROPE_DIL: _ROPE_LAYOUTS[1][1]}
W_A = 3 * (W_SB + W_DIFF)
PROJ_TN = 512
_PROJ_KINDS = (ROPE_NONE,) * 3 + (ROPE_DIFF,) * 2 + (ROPE_NONE,) + (ROPE_DIL,) * 4 + (ROPE_NONE,) * 2


def _resident(shape):
    return pl.BlockSpec(shape, lambda *_: (0,) * len(shape), pipeline_mode=pl.Buffered(1))


def _proj_kernel(x_ref, g_ref, w_ref, tab_ref, oa_ref, od_ref, h_ref):
    h_ref[...] = _rms(x_ref[...], g_ref[...]).astype(BF16)
    tn = PROJ_TN
    for idx, kind in enumerate(_PROJ_KINDS):
        acc = _dot(h_ref[...], w_ref[:, idx * tn:(idx + 1) * tn])
        o_ref, col = (oa_ref, idx * tn) if idx * tn < W_A else (od_ref, idx * tn - W_A)
        if kind == ROPE_NONE:
            o_ref[:, col:col + tn] = acc.astype(o_ref.dtype)
            continue
        base = 3 * (kind - 1)
        shift = _ROPE_SHIFT[kind]
        c, lo, hi = tab_ref[base], tab_ref[base + 1], tab_ref[base + 2]
        for q in range(tn // LANES):
            xk = acc[:, q * LANES:(q + 1) * LANES]
            rot = xk * c + pltpu.roll(xk, LANES - shift, 1) * lo + pltpu.roll(xk, shift, 1) * hi
            o_ref[:, col + q * LANES:col + (q + 1) * LANES] = rot.astype(o_ref.dtype)


def in_proj(x, g, w, tabs, tm=256):
    m, d = x.shape
    n = w.shape[1]
    assert n == len(_PROJ_KINDS) * PROJ_TN
    return pl.pallas_call(
        _proj_kernel,
        grid=(m // tm,),
        in_specs=[pl.BlockSpec((tm, d), lambda i: (i, 0)),
                  _resident((1, d)),
                  _resident((d, n)),
                  pl.BlockSpec((6, tm, LANES), lambda i: (0, i, 0))],
        out_specs=[pl.BlockSpec((tm, W_A), lambda i: (i, 0)),
                   pl.BlockSpec((tm, n - W_A), lambda i: (i, 0))],
        out_shape=[jax.ShapeDtypeStruct((m, W_A), BF16),
                   jax.ShapeDtypeStruct((m, n - W_A), F32)],
        scratch_shapes=[pltpu.VMEM((tm, d), BF16)],
        compiler_params=_params("parallel"),
        name="in_proj",
    )(x, g.reshape(1, d), w, tabs)


def _mlp_kernel(*refs, final_norm):
    if final_norm:
        sb_ref, df_ref, dl_ref, wo_ref, x_ref, g_ref, wg_ref, wu_ref, wd_ref, gf_ref, o_ref, h_ref = refs
    else:
        sb_ref, df_ref, dl_ref, wo_ref, x_ref, g_ref, wg_ref, wu_ref, wd_ref, o_ref, h_ref = refs
    j = pl.program_id(1)

    @pl.when(j == 0)
    def _():
        mixed = (_dot(sb_ref[...], wo_ref[0:W_SB, :])
                 + _dot(df_ref[...], wo_ref[W_SB:W_SB + W_DIFF, :])
                 + _dot(dl_ref[...], wo_ref[W_SB + W_DIFF:, :]))
        xn = x_ref[...] + mixed
        o_ref[...] = xn
        h_ref[...] = _rms(xn, g_ref[...]).astype(BF16)

    h = h_ref[...]
    gate = _dot(h, wg_ref[...])
    up = _dot(h, wu_ref[...])
    act = (gate * jax.nn.sigmoid(gate)) * up
    o_ref[...] += _dot(act.astype(BF16), wd_ref[...])

    if final_norm:
        @pl.when(j == pl.num_programs(1) - 1)
        def _():
            o_ref[...] = _rms(o_ref[...], gf_ref[...])


def mlp_block(o_sb, o_df, o_dl, w_out, x, g, wg, wu, wd, final_g=None, tm=512, tf=512):
    m, d = x.shape
    f = wg.shape[1]
    final_norm = final_g is not None
    in_specs = [pl.BlockSpec((tm, W_SB), lambda i, j: (i, 0)),
                pl.BlockSpec((tm, W_DIFF), lambda i, j: (i, 0)),
                pl.BlockSpec((tm, W_DIL), lambda i, j: (i, 0)),
                _resident(w_out.shape),
                pl.BlockSpec((tm, d), lambda i, j: (i, 0)),
                _resident((1, d)),
                pl.BlockSpec((d, tf), lambda i, j: (0, j)),
                pl.BlockSpec((d, tf), lambda i, j: (0, j)),
                pl.BlockSpec((tf, d), lambda i, j: (j, 0))]
    args = [o_sb, o_df, o_dl, w_out, x, g.reshape(1, d), wg, wu, wd]
    if final_norm:
        in_specs.append(_resident((1, d)))
        args.append(final_g.reshape(1, d))
    return pl.pallas_call(
        functools.partial(_mlp_kernel, final_norm=final_norm),
        grid=(m // tm, f // tf),
        in_specs=in_specs,
        out_specs=pl.BlockSpec((tm, d), lambda i, j: (i, 0)),
        out_shape=jax.ShapeDtypeStruct((m, d), F32),
        scratch_shapes=[pltpu.VMEM((tm, d), BF16)],
        compiler_params=_params("parallel", "arbitrary"),
        name="mlp_block",
    )(*args)


def _causal_sweep(i, n_chain, step, keys_per_iter, wide=False, exhausted=None):
    assert n_chain % keys_per_iter == 0
    base = n_chain * i
    if wide:
        _interleave([step(c, base + c, True) for c in range(n_chain)])
        _interleave([step(c, base, False, c) for c in range(1, n_chain)])
    else:
        pairs = [(c, d) for d in range(n_chain - 1, -1, -1) for c in range(d, n_chain)]
        _interleave([step(c, base + d, c == d) for c, d in pairs if c - d <= 1])
        far = [(c, d) for c, d in pairs if c - d > 1]

        def far_blocks():
            _interleave([step(c, base + d, False) for c, d in far])

        if far and exhausted is not None:
            pl.when(jnp.logical_not(exhausted(2)))(far_blocks)
        elif far:
            far_blocks()

    def body(t, _):
        first = base - 1 - t * keys_per_iter
        if wide:
            _interleave([step(c, first - (keys_per_iter - 1), False, keys_per_iter) for c in range(n_chain)])
        else:
            _interleave([step(c, first - u, False) for u in range(keys_per_iter) for c in range(n_chain)])
        return 0

    trips = (n_chain // keys_per_iter) * i
    if exhausted is None:
        lax.fori_loop(0, trips, body, 0)
        return

    def more(state):
        t, done = state
        return (t < trips) & jnp.logical_not(done)

    def advance(state):
        body(state[0], 0)
        return state[0] + 1, exhausted(0)

    lax.while_loop(more, advance, (0, exhausted(0)))


def _interleave(steps):
    steps = list(steps)
    while steps:
        alive = []
        for g in steps:
            try:
                next(g)
                alive.append(g)
            except StopIteration:
                pass
        steps = alive


def _from_key_matrix(blk):
    tri = np.tril(np.ones((blk, blk), np.float32))
    return jnp.asarray(np.concatenate([tri, tri], axis=0), BF16)


def _sb_kernel(q_ref, k_ref, v_ref, later_ref, g_ref, o_ref, qs_ref, acc_ref, carry_ref, *,
               blk, n_chain, keys_per_iter):
    i = pl.program_id(2)
    scale = HEAD_DIM ** -0.5
    qs_ref[...] = (q_ref[...].astype(F32) * scale).astype(BF16)
    acc_ref[...] = jnp.zeros_like(acc_ref)
    carry_ref[...] = jnp.zeros_like(carry_ref)
    row = lax.broadcasted_iota(jnp.int32, (blk, blk), 0)
    col = lax.broadcasted_iota(jnp.int32, (blk, blk), 1)
    strict = col < row

    def step(c, j, diag):
        rows = slice(c * blk, (c + 1) * blk)
        start = pl.multiple_of(j * blk, blk)
        k = k_ref[pl.ds(start, blk), :]
        v = v_ref[pl.ds(start, blk), :]
        z = _dot_nt(qs_ref[rows, :], k)
        yield
        sp = jnp.maximum(z, 0.0) + jnp.log(1.0 + jnp.exp2(jnp.abs(z) * -LOG2_E))
        spm = jnp.where(strict, sp, 0.0) if diag else sp
        hi = spm.astype(BF16)
        lo = (spm - hi.astype(F32)).astype(BF16)
        from_key = _dot(jnp.concatenate([hi, lo], axis=1), later_ref[...])
        yield
        loga = z - (from_key + _tile(carry_ref[rows, :], blk // LANES))
        if diag:
            loga = jnp.where(strict, loga, NEG_INF)
        pv = _dot(jnp.exp(loga).astype(BF16), v)
        carry_ref[rows, :] += from_key[:, 0:1]
        yield
        acc_ref[rows, :] += pv

    _causal_sweep(i, n_chain, step, keys_per_iter,
                  exhausted=lambda chain: jnp.min(carry_ref[chain * blk:, :]) >= SB_DEAD_MASS)
    o_ref[...] = _rms(acc_ref[...], g_ref[...]).astype(o_ref.dtype)


def sb_attention(pa, g, blk=256, n_chain=4, keys_per_iter=1):
    b, s, _ = pa.shape
    bq = blk * n_chain
    return pl.pallas_call(
        functools.partial(_sb_kernel, blk=blk, n_chain=n_chain, keys_per_iter=keys_per_iter),
        grid=(b, N_SB, s // bq),
        in_specs=[pl.BlockSpec((None, bq, HEAD_DIM), lambda b_, h, i: (b_, i, h)),
                  pl.BlockSpec((None, s, HEAD_DIM), lambda b_, h, i: (b_, 0, N_SB + h)),
                  pl.BlockSpec((None, s, HEAD_DIM), lambda b_, h, i: (b_, 0, 2 * N_SB + h)),
                  pl.BlockSpec((2 * blk, blk), lambda b_, h, i: (0, 0)),
                  pl.BlockSpec((1, HEAD_DIM), lambda b_, h, i: (0, 0))],
        out_specs=pl.BlockSpec((None, bq, HEAD_DIM), lambda b_, h, i: (b_, i, h)),
        out_shape=jax.ShapeDtypeStruct((b, s, W_SB), BF16),
        scratch_shapes=[pltpu.VMEM((bq, HEAD_DIM), BF16),
                        pltpu.VMEM((bq, HEAD_DIM), F32),
                        pltpu.VMEM((bq, LANES), F32)],
        compiler_params=_params("parallel", "parallel", "arbitrary"),
        name="sb_attention",
    )(pa, pa, pa, _from_key_matrix(blk), g.reshape(1, HEAD_DIM))


def _diff_kernel(*refs, blk, n_chain, keys_per_iter, lambda_init, n_side):
    q_ref, k_ref, v_ref, lam_ref, g_ref = refs[:5]
    side_in, o_ref = refs[5:5 + n_side], refs[5 + n_side]
    side_out = refs[6 + n_side:6 + 2 * n_side]
    qs_ref, vx_ref, m_ref, acc_ref = refs[6 + 2 * n_side:]
    for src, dst in zip(side_in, side_out):
        dst[...] = src[...].astype(BF16)
    i = pl.program_id(2)
    seq = v_ref.shape[0]

    @pl.when(i == 0)
    def _():
        vx_ref[:, 0:HEAD_DIM] = v_ref[...]
        vx_ref[:, HEAD_DIM:] = jnp.ones((seq, HEAD_DIM), BF16)

    scale = DIFF_QK ** -0.5
    q = q_ref[...].astype(F32) * scale
    lc = lax.broadcasted_iota(jnp.int32, (blk, HEAD_DIM), 1)
    for c in range(n_chain):
        qc = q[c * blk:(c + 1) * blk]
        qs_ref[(2 * c) * blk:(2 * c + 1) * blk, :] = jnp.where(lc < DIFF_QK, qc, 0.0).astype(BF16)
        qs_ref[(2 * c + 1) * blk:(2 * c + 2) * blk, :] = jnp.where(lc >= DIFF_QK, qc, 0.0).astype(BF16)
    m_ref[...] = jnp.full(m_ref.shape, NEG_INF, F32)
    acc_ref[...] = jnp.zeros_like(acc_ref)
    row = lax.broadcasted_iota(jnp.int32, (blk, blk), 0)
    col = lax.broadcasted_iota(jnp.int32, (blk, blk), 1)
    causal = col <= row

    def step(c, j, diag, width=1):
        start = pl.multiple_of(j * blk, blk)
        k = k_ref[pl.ds(start, width * blk), :]
        vx = vx_ref[pl.ds(start, width * blk), :]
        rows = [slice((2 * c + e) * blk, (2 * c + e + 1) * blk) for e in range(2)]
        s = [_dot_nt(qs_ref[r, :], k) for r in rows]
        yield
        alpha, pv = [], []
        for e, r in enumerate(rows):
            se = jnp.where(causal, s[e], NEG_INF) if diag else s[e]
            m_prev = m_ref[r, :]
            m_new = jnp.maximum(m_prev, jnp.max(se, axis=-1, keepdims=True))
            alpha.append(jnp.exp(m_prev - m_new))
            p = jnp.exp((se - _tile(m_new, width * blk // LANES)).astype(BF16))
            m_ref[r, :] = m_new
            pv.append(_dot(p, vx))
        yield
        for e, r in enumerate(rows):
            acc_ref[r, :] = _tile(alpha[e], 2) * acc_ref[r, :] + pv[e]

    _causal_sweep(i, n_chain, step, keys_per_iter, wide=True)
    lq = lam_ref[...]
    lam = (jnp.exp(jnp.sum(lq[0:1] * lq[1:2], axis=-1, keepdims=True))
           - jnp.exp(jnp.sum(lq[2:3] * lq[3:4], axis=-1, keepdims=True)) + lambda_init)
    g = g_ref[...]
    for c in range(n_chain):
        o = []
        for e in range(2):
            acc = acc_ref[(2 * c + e) * blk:(2 * c + e + 1) * blk, :]
            o.append(acc[:, :HEAD_DIM] / acc[:, HEAD_DIM:])
        out = o[0] - lam * o[1]
        o_ref[c * blk:(c + 1) * blk, :] = (_rms(out, g) * (1.0 - lambda_init)).astype(o_ref.dtype)


def diff_attention(pa, lam_vecs, g, lambda_init, side=(), blk=256, n_chain=4, keys_per_iter=4):
    b, s, _ = pa.shape
    c0 = 3 * N_SB
    bq = blk * n_chain
    n_i = s // bq
    side_in, side_out, side_shape = _side_cast_specs(
        side, b * N_DIFF * n_i, lambda b_, h, i: (b_ * N_DIFF + h) * n_i + i)
    out = pl.pallas_call(
        functools.partial(_diff_kernel, blk=blk, n_chain=n_chain, keys_per_iter=keys_per_iter,
                          lambda_init=lambda_init, n_side=len(side)),
        grid=(b, N_DIFF, n_i),
        in_specs=[pl.BlockSpec((None, bq, HEAD_DIM), lambda b_, h, i: (b_, i, c0 + h)),
                  pl.BlockSpec((None, s, HEAD_DIM), lambda b_, h, i: (b_, 0, c0 + N_DIFF + h)),
                  pl.BlockSpec((None, s, HEAD_DIM), lambda b_, h, i: (b_, 0, c0 + 2 * N_DIFF + h)),
                  pl.BlockSpec((4, DIFF_QK), lambda b_, h, i: (0, 0)),
                  pl.BlockSpec((1, HEAD_DIM), lambda b_, h, i: (0, 0))] + side_in,
        out_specs=[pl.BlockSpec((None, bq, HEAD_DIM), lambda b_, h, i: (b_, i, h))] + side_out,
        out_shape=[jax.ShapeDtypeStruct((b, s, W_DIFF), BF16)] + side_shape,
        scratch_shapes=[pltpu.VMEM((2 * bq, HEAD_DIM), BF16),
                        pltpu.VMEM((s, 2 * HEAD_DIM), BF16),
                        pltpu.VMEM((2 * bq, LANES), F32),
                        pltpu.VMEM((2 * bq, 2 * HEAD_DIM), F32)],
        compiler_params=_params("parallel", "parallel", "arbitrary"),
        name="diff_attention",
    )(pa, pa, pa, lam_vecs, g.reshape(1, HEAD_DIM), *[w for w, _ in side])
    return out[0], out[1:]


def _dil_kernel(q_ref, k_ref, v_ref, g_ref, o_ref, qd_ref, kd_ref, vd_ref, q4_ref, k4_ref, v4_ref,
                bias_ref, op_ref, lse_ref, *, seq, group, unroll):
    blk = DIL_SPAN
    scale = HEAD_DIM ** -0.5
    n_blocks = seq // blk
    kd_ref[0:blk, :] = jnp.zeros((blk, HEAD_DIM), BF16)
    vd_ref[0:blk, 0:HEAD_DIM] = jnp.zeros((blk, HEAD_DIM), BF16)
    vd_ref[:, HEAD_DIM:] = jnp.ones((seq + blk, HEAD_DIM), BF16)
    ii = lax.broadcasted_iota(jnp.int32, (blk, 2 * blk), 0)
    jj = lax.broadcasted_iota(jnp.int32, (blk, 2 * blk), 1)
    dist = blk + ii - jj
    in_window = (dist >= 0) & (dist <= DIL_SPAN)
    bias_ref[0] = jnp.where(in_window & (jj >= blk), 0.0, NEG_INF)
    bias_ref[1] = jnp.where(in_window, 0.0, NEG_INF)
    quarter = seq // 4

    for p, r in enumerate(DILATIONS):
        per_seq = n_blocks // r
        shift = per_seq.bit_length() - 1

        def block_of(n, per_seq=per_seq, shift=shift):
            return lax.shift_right_logical(n, shift), n & (per_seq - 1)

        def natural_rows(n, r=r, block_of=block_of):
            c, i = block_of(n)
            src = c + r * blk * i
            if r == 1:
                return pl.ds(pl.multiple_of(src, blk), blk)
            return pl.ds(src, blk, stride=r)

        def gather(n, _, r=r, block_of=block_of, natural_rows=natural_rows):
            here = pl.ds(pl.multiple_of(blk * n, blk), blk)
            if r == 16:
                c, i = block_of(n)
                rows = pl.ds((c & 3) * quarter + lax.shift_right_logical(c, 2) + 4 * blk * i, blk, stride=4)
                q, k, v = q4_ref[rows, :], k4_ref[rows, :], v4_ref[rows, :]
            else:
                rows = natural_rows(n)
                q, k, v = q_ref[rows, :], k_ref[rows, :], v_ref[rows, :]
            if r == 4:
                q4_ref[here, :], k4_ref[here, :], v4_ref[here, :] = q, k, v
            dst = pl.ds(pl.multiple_of(blk * (n + 1), blk), blk)
            kd_ref[dst, :] = k.astype(BF16)
            vd_ref[dst, 0:HEAD_DIM] = v.astype(BF16)
            qd_ref[here, :] = (q * scale).astype(BF16)
            return 0

        lax.fori_loop(0, n_blocks, gather, 0, unroll=unroll)

        def attend(n, p=p, block_of=block_of, natural_rows=natural_rows):
            _, i = block_of(n)
            rows = natural_rows(n)
            q = qd_ref[pl.ds(pl.multiple_of(blk * n, blk), blk), :]
            win = pl.ds(pl.multiple_of(blk * n, blk), 2 * blk)
            s = _dot_nt(q, kd_ref[win, :])
            yield
            s = s + bias_ref[jnp.where(i == 0, 0, 1)]
            m = jnp.max(s, axis=-1, keepdims=True)
            pv = _dot(jnp.exp((s - m).astype(BF16)), vd_ref[win, :])
            yield
            l = pv[:, HEAD_DIM:]
            op_ref[p, rows, :] = pv[:, :HEAD_DIM] / l
            lse_ref[p, rows, :] = m + jnp.log(l)

        def attend_group(t, _, attend=attend):
            _interleave([attend(t * group + u) for u in range(group)])
            return 0

        lax.fori_loop(0, n_blocks // group, attend_group, 0)

    g = g_ref[...]

    def merge(n, _):
        rows = pl.ds(pl.multiple_of(n * blk, blk), blk)
        lses = [lse_ref[p, rows, :] for p in range(len(DILATIONS))]
        top = functools.reduce(jnp.maximum, lses)
        ws = [jnp.exp(x - top) for x in lses]
        den = functools.reduce(lambda a, b_: a + b_, ws)
        num = functools.reduce(lambda a, b_: a + b_,
                               [op_ref[p, rows, :] * ws[p] for p in range(len(DILATIONS))])
        o_ref[rows, :] = _rms(num * (1.0 / den), g).astype(o_ref.dtype)
        return 0

    lax.fori_loop(0, n_blocks, merge, 0, unroll=2 * unroll)


def dil_attention(pd, g, group=32, unroll=4):
    b, s, _ = pd.shape
    n_pat = len(DILATIONS)
    assert DILATIONS == (1, 4, 16) and (s // DIL_SPAN) % group == 0
    return pl.pallas_call(
        functools.partial(_dil_kernel, seq=s, group=group, unroll=unroll),
        grid=(b, N_DIL),
        in_specs=[pl.BlockSpec((None, s, HEAD_DIM), lambda b_, h: (b_, 0, h)),
                  pl.BlockSpec((None, s, HEAD_DIM), lambda b_, h: (b_, 0, N_DIL + h)),
                  pl.BlockSpec((None, s, HEAD_DIM), lambda b_, h: (b_, 0, 2 * N_DIL + h)),
                  pl.BlockSpec((1, HEAD_DIM), lambda b_, h: (0, 0))],
        out_specs=pl.BlockSpec((None, s, HEAD_DIM), lambda b_, h: (b_, 0, h)),
        out_shape=jax.ShapeDtypeStruct((b, s, W_DIL), BF16),
        scratch_shapes=[pltpu.VMEM((s, HEAD_DIM), BF16),
                        pltpu.VMEM((s + DIL_SPAN, HEAD_DIM), BF16),
                        pltpu.VMEM((s + DIL_SPAN, 2 * HEAD_DIM), BF16),
                        pltpu.VMEM((s, HEAD_DIM), F32),
                        pltpu.VMEM((s, HEAD_DIM), F32),
                        pltpu.VMEM((s, HEAD_DIM), F32),
                        pltpu.VMEM((2, DIL_SPAN, 2 * DIL_SPAN), F32),
                        pltpu.VMEM((n_pat, s, HEAD_DIM), F32),
                        pltpu.VMEM((n_pat, s, HEAD_DIM), F32)],
        compiler_params=_params("parallel", "arbitrary"),
        name="dil_attention",
    )(pd, pd, pd, g.reshape(1, HEAD_DIM))


def kernel(x, positions, norm_mix_g, w_in, lambda_q1, lambda_k1, lambda_q2, lambda_k2, g_sb_out, g_diff_out, g_dil_out, w_out, norm_ffn_g, w_gate, w_up, w_down, norm_final_g):
    batch, seq, d = x.shape
    m = batch * seq
    depth = w_in.shape[0]
    xf = x.reshape(m, d)
    tabs, (w_in_bf,) = rope_tables(positions.reshape(m, 1), [(w_in, 0)])
    for layer in range(depth):
        pa, pd = in_proj(xf, norm_mix_g[layer], w_in_bf, tabs)
        pa = pa.reshape(batch, seq, W_A)
        pd = pd.reshape(batch, seq, 3 * W_DIL)
        lambda_init = 0.8 - 0.6 * math.exp(-0.3 * layer)
        lam_vecs = jnp.stack([lambda_q1[layer], lambda_k1[layer],
                              lambda_q2[layer], lambda_k2[layer]]).astype(F32)
        o_sb = sb_attention(pa, g_sb_out[layer]).reshape(m, W_SB)
        side = [(w, layer) for w in (w_out, w_gate, w_up, w_down)]
        side += [(w_in, layer + 1)] if layer + 1 < depth else []
        o_df, converted = diff_attention(pa, lam_vecs, g_diff_out[layer], lambda_init, side)
        o_df = o_df.reshape(m, W_DIFF)
        w_out_bf, w_gate_bf, w_up_bf, w_down_bf = converted[:4]
        w_in_bf = converted[4] if layer + 1 < depth else None
        o_dl = dil_attention(pd, g_dil_out[layer]).reshape(m, W_DIL)
        xf = mlp_block(o_sb, o_df, o_dl, w_out_bf, xf, norm_ffn_g[layer], w_gate_bf, w_up_bf, w_down_bf,
                       final_g=norm_final_g if layer == depth - 1 else None)
    return xf.reshape(batch, seq, d)
```

```python
import functools
import math

import jax
import jax.numpy as jnp
import numpy as np
from jax import lax
from jax.experimental import pallas as pl
from jax.experimental.pallas import tpu as pltpu

F32 = jnp.float32
BF16 = jnp.bfloat16

HEAD_DIM = 128
N_SB = 4
N_DIFF = 4
N_DIL = 8
W_SB = N_SB * HEAD_DIM
W_DIFF = N_DIFF * HEAD_DIM
W_DIL = N_DIL * HEAD_DIM
DIFF_QK = HEAD_DIM // 2
ROPE_THETA = 500000.0
ROPE_FRACTION = 0.25
DILATIONS = (1, 4, 16)
DIL_SPAN = 128
NORM_EPS = 1e-6
LANES = 128
VMEM_LIMIT = 56 * 1024 * 1024

NEG_INF = float("-inf")
LOG2_E = math.log2(math.e)
SB_DEAD_MASS = 106.0


def _params(*sem):
    return pltpu.CompilerParams(dimension_semantics=sem, vmem_limit_bytes=VMEM_LIMIT)


def _dot(a, b):
    return jnp.dot(a, b, preferred_element_type=F32)


def _dot_nt(a, b):
    return lax.dot_general(a, b, (((1,), (1,)), ((), ())), preferred_element_type=F32)


def _tile(x, n):
    return x if n == 1 else jnp.concatenate([x] * n, axis=1)


def _rms(x, g):
    return x * lax.rsqrt(jnp.mean(x * x, axis=-1, keepdims=True) + NORM_EPS) * g


def _side_cast_specs(side, n_steps, step_index):
    in_specs, out_specs, out_shape = [], [], []
    for w, layer in side:
        _, r, c = w.shape
        rows = r // n_steps
        assert rows * n_steps == r and rows % 16 == 0
        in_specs.append(pl.BlockSpec((None, rows, c), lambda *g, layer=layer: (layer, step_index(*g), 0)))
        out_specs.append(pl.BlockSpec((rows, c), lambda *g: (step_index(*g), 0)))
        out_shape.append(jax.ShapeDtypeStruct((r, c), BF16))
    return in_specs, out_specs, out_shape


_ROPE_LAYOUTS = ((DIFF_QK, int(DIFF_QK * ROPE_FRACTION) // 2), (HEAD_DIM, int(HEAD_DIM * ROPE_FRACTION) // 2))


def _rope_lane_consts():
    freqs = [ROPE_THETA ** (-jnp.arange(half, dtype=F32) / half) for _, half in _ROPE_LAYOUTS]
    used = sum(half for _, half in _ROPE_LAYOUTS)
    row = jnp.concatenate(freqs + [jnp.zeros((LANES - used,), F32)])
    return jnp.concatenate([row[None, :], jnp.zeros((7, LANES), F32)])


def _rope_table_kernel(*refs, n_side):
    pos_ref, c_ref = refs[:2]
    side_in, o_ref, side_out = refs[2:2 + n_side], refs[2 + n_side], refs[3 + n_side:]
    for src, dst in zip(side_in, side_out):
        dst[...] = src[...].astype(BF16)
    ang = pos_ref[...].astype(F32) * c_ref[0:1, :]
    cos, sin = jnp.cos(ang), jnp.sin(ang)
    lane = lax.broadcasted_iota(jnp.int32, ang.shape, 1)
    src = 0
    for k, (chunk, half) in enumerate(_ROPE_LAYOUTS):
        c_tab = jnp.ones_like(ang)
        lo_tab = jnp.zeros_like(ang)
        hi_tab = jnp.zeros_like(ang)
        for start in range(0, LANES, chunk):
            for part, dst in enumerate((start, start + half)):
                here = (lane >= dst) & (lane < dst + half)
                shift = (dst - src) % LANES
                c_here = pltpu.roll(cos, shift, 1) if shift else cos
                s_here = pltpu.roll(sin, shift, 1) if shift else sin
                c_tab = jnp.where(here, c_here, c_tab)
                if part == 0:
                    lo_tab = jnp.where(here, -s_here, lo_tab)
                else:
                    hi_tab = jnp.where(here, s_here, hi_tab)
        o_ref[3 * k], o_ref[3 * k + 1], o_ref[3 * k + 2] = c_tab, lo_tab, hi_tab
        src += half


def rope_tables(positions, side=(), tm=512):
    m = positions.shape[0]
    side_in, side_out, side_shape = _side_cast_specs(side, m // tm, lambda i: i)
    out = pl.pallas_call(
        functools.partial(_rope_table_kernel, n_side=len(side)),
        grid=(m // tm,),
        in_specs=[pl.BlockSpec((tm, 1), lambda i: (i, 0)),
                  pl.BlockSpec((8, LANES), lambda i: (0, 0))] + side_in,
        out_specs=[pl.BlockSpec((6, tm, LANES), lambda i: (0, i, 0))] + side_out,
        out_shape=[jax.ShapeDtypeStruct((6, m, LANES), F32)] + side_shape,
        compiler_params=_params("parallel"),
        name="rope_tables",
    )(positions, _rope_lane_consts(), *[w for w, _ in side])
    return out[0], out[1:]


ROPE_NONE, ROPE_DIFF, ROPE_DIL = 0, 1, 2
_ROPE_SHIFT = {ROPE_DIFF: _ROPE_LAYOUTS[0][1], ROPE_DIL: _ROPE_LAYOUTS[1][1]}
W_A = 3 * (W_SB + W_DIFF)
PROJ_TN = 512
_PROJ_KINDS = (ROPE_NONE,) * 3 + (ROPE_DIFF,) * 2 + (ROPE_NONE,) + (ROPE_DIL,) * 4 + (ROPE_NONE,) * 2


def _resident(shape):
    return pl.BlockSpec(shape, lambda *_: (0,) * len(shape), pipeline_mode=pl.Buffered(1))


def _proj_kernel(x_ref, g_ref, w_ref, tab_ref, oa_ref, od_ref, h_ref):
    h_ref[...] = _rms(x_ref[...], g_ref[...]).astype(BF16)
    tn = PROJ_TN
    for idx, kind in enumerate(_PROJ_KINDS):
        acc = _dot(h_ref[...], w_ref[:, idx * tn:(idx + 1) * tn])
        o_ref, col = (oa_ref, idx * tn) if idx * tn < W_A else (od_ref, idx * tn - W_A)
        if kind == ROPE_NONE:
            o_ref[:, col:col + tn] = acc.astype(o_ref.dtype)
            continue
        base = 3 * (kind - 1)
        shift = _ROPE_SHIFT[kind]
        c, lo, hi = tab_ref[base], tab_ref[base + 1], tab_ref[base + 2]
        for q in range(tn // LANES):
            xk = acc[:, q * LANES:(q + 1) * LANES]
            rot = xk * c + pltpu.roll(xk, LANES - shift, 1) * lo + pltpu.roll(xk, shift, 1) * hi
            o_ref[:, col + q * LANES:col + (q + 1) * LANES] = rot.astype(o_ref.dtype)


def in_proj(x, g, w, tabs, tm=256):
    m, d = x.shape
    n = w.shape[1]
    assert n == len(_PROJ_KINDS) * PROJ_TN
    return pl.pallas_call(
        _proj_kernel,
        grid=(m // tm,),
        in_specs=[pl.BlockSpec((tm, d), lambda i: (i, 0)),
                  _resident((1, d)),
                  _resident((d, n)),
                  pl.BlockSpec((6, tm, LANES), lambda i: (0, i, 0))],
        out_specs=[pl.BlockSpec((tm, W_A), lambda i: (i, 0)),
                   pl.BlockSpec((tm, n - W_A), lambda i: (i, 0))],
        out_shape=[jax.ShapeDtypeStruct((m, W_A), BF16),
                   jax.ShapeDtypeStruct((m, n - W_A), F32)],
        scratch_shapes=[pltpu.VMEM((tm, d), BF16)],
        compiler_params=_params("parallel"),
        name="in_proj",
    )(x, g.reshape(1, d), w, tabs)


def _mlp_kernel(*refs, final_norm):
    if final_norm:
        sb_ref, df_ref, dl_ref, wo_ref, x_ref, g_ref, wg_ref, wu_ref, wd_ref, gf_ref, o_ref, h_ref = refs
    else:
        sb_ref, df_ref, dl_ref, wo_ref, x_ref, g_ref, wg_ref, wu_ref, wd_ref, o_ref, h_ref = refs
    j = pl.program_id(1)

    @pl.when(j == 0)
    def _():
        mixed = (_dot(sb_ref[...], wo_ref[0:W_SB, :])
                 + _dot(df_ref[...], wo_ref[W_SB:W_SB + W_DIFF, :])
                 + _dot(dl_ref[...], wo_ref[W_SB + W_DIFF:, :]))
        xn = x_ref[...] + mixed
        o_ref[...] = xn
        h_ref[...] = _rms(xn, g_ref[...]).astype(BF16)

    h = h_ref[...]
    gate = _dot(h, wg_ref[...])
    up = _dot(h, wu_ref[...])
    act = (gate * jax.nn.sigmoid(gate)) * up
    o_ref[...] += _dot(act.astype(BF16), wd_ref[...])

    if final_norm:
        @pl.when(j == pl.num_programs(1) - 1)
        def _():
            o_ref[...] = _rms(o_ref[...], gf_ref[...])


def mlp_block(o_sb, o_df, o_dl, w_out, x, g, wg, wu, wd, final_g=None, tm=512, tf=512):
    m, d = x.shape
    f = wg.shape[1]
    final_norm = final_g is not None
    in_specs = [pl.BlockSpec((tm, W_SB), lambda i, j: (i, 0)),
                pl.BlockSpec((tm, W_DIFF), lambda i, j: (i, 0)),
                pl.BlockSpec((tm, W_DIL), lambda i, j: (i, 0)),
                _resident(w_out.shape),
                pl.BlockSpec((tm, d), lambda i, j: (i, 0)),
                _resident((1, d)),
                pl.BlockSpec((d, tf), lambda i, j: (0, j)),
                pl.BlockSpec((d, tf), lambda i, j: (0, j)),
                pl.BlockSpec((tf, d), lambda i, j: (j, 0))]
    args = [o_sb, o_df, o_dl, w_out, x, g.reshape(1, d), wg, wu, wd]
    if final_norm:
        in_specs.append(_resident((1, d)))
        args.append(final_g.reshape(1, d))
    return pl.pallas_call(
        functools.partial(_mlp_kernel, final_norm=final_norm),
        grid=(m // tm, f // tf),
        in_specs=in_specs,
        out_specs=pl.BlockSpec((tm, d), lambda i, j: (i, 0)),
        out_shape=jax.ShapeDtypeStruct((m, d), F32),
        scratch_shapes=[pltpu.VMEM((tm, d), BF16)],
        compiler_params=_params("parallel", "arbitrary"),
        name="mlp_block",
    )(*args)


def _causal_sweep(i, n_chain, step, keys_per_iter, wide=False, exhausted=None, ahead=False):
    assert n_chain % keys_per_iter == 0 and not (ahead and (wide or keys_per_iter != 1))
    base = n_chain * i
    if wide:
        _interleave([step(c, base + c, True) for c in range(n_chain)])
        _interleave([step(c, base, False, c) for c in range(1, n_chain)])
    else:
        pairs = [(c, d) for d in range(n_chain - 1, -1, -1) for c in range(d, n_chain)]
        near = [step(c, base + d, c == d) for c, d in pairs if c - d <= 1]
        if ahead:
            near.append(step(0, jnp.maximum(base - 1, 0), False, i > 0))
        _interleave(near)
        far = [(c, d) for c, d in pairs if c - d > 1]

        def far_blocks():
            _interleave([step(c, base + d, False) for c, d in far])

        if far and exhausted is not None:
            pl.when(jnp.logical_not(exhausted(2)))(far_blocks)
        elif far:
            far_blocks()

    def body(t, _):
        first = base - 1 - t * keys_per_iter
        if wide:
            _interleave([step(c, first - (keys_per_iter - 1), False, keys_per_iter) for c in range(n_chain)])
        elif ahead:
            _interleave([step(0, jnp.maximum(first - 1, 0), False, first >= 1)]
                        + [step(c, first, False) for c in range(1, n_chain)])
        else:
            _interleave([step(c, first - u, False) for u in range(keys_per_iter) for c in range(n_chain)])
        return 0

    trips = (n_chain // keys_per_iter) * i
    if exhausted is None:
        lax.fori_loop(0, trips, body, 0)
        return

    def more(state):
        t, done = state
        return (t < trips) & jnp.logical_not(done)

    def advance(state):
        body(state[0], 0)
        return state[0] + 1, exhausted(0)

    lax.while_loop(more, advance, (0, exhausted(0)))


def _interleave(steps):
    steps = list(steps)
    while steps:
        alive = []
        for g in steps:
            try:
                next(g)
                alive.append(g)
            except StopIteration:
                pass
        steps = alive


def _from_key_matrix(blk):
    tri = np.tril(np.ones((blk, blk), np.float32))
    return jnp.asarray(np.concatenate([tri, tri], axis=0), BF16)


def _sb_kernel(q_ref, k_ref, v_ref, later_ref, g_ref, o_ref, qs_ref, acc_ref, carry_ref, *,
               blk, n_chain, keys_per_iter):
    i = pl.program_id(2)
    scale = HEAD_DIM ** -0.5
    qs_ref[...] = (q_ref[...].astype(F32) * scale).astype(BF16)
    acc_ref[...] = jnp.zeros_like(acc_ref)
    carry_ref[...] = jnp.zeros_like(carry_ref)
    row = lax.broadcasted_iota(jnp.int32, (blk, blk), 0)
    col = lax.broadcasted_iota(jnp.int32, (blk, blk), 1)
    strict = col < row

    def step(c, j, diag, live=None):
        rows = slice(c * blk, (c + 1) * blk)
        start = pl.multiple_of(j * blk, blk)
        k = k_ref[pl.ds(start, blk), :]
        v = v_ref[pl.ds(start, blk), :]
        z = _dot_nt(qs_ref[rows, :], k)
        yield
        sp = jnp.maximum(z, 0.0) + jnp.log(1.0 + jnp.exp2(jnp.abs(z) * -LOG2_E))
        spm = jnp.where(strict, sp, 0.0) if diag else sp
        if live is not None:
            spm = jnp.where(live, spm, 0.0)
        hi = spm.astype(BF16)
        lo = (spm - hi.astype(F32)).astype(BF16)
        from_key = _dot(jnp.concatenate([hi, lo], axis=1), later_ref[...])
        yield
        loga = z - (from_key + _tile(carry_ref[rows, :], blk // LANES))
        if diag:
            loga = jnp.where(strict, loga, NEG_INF)
        if live is not None:
            loga = jnp.where(live, loga, NEG_INF)
        pv = _dot(jnp.exp(loga).astype(BF16), v)
        carry_ref[rows, :] += from_key[:, 0:1]
        yield
        acc_ref[rows, :] += pv

    _causal_sweep(i, n_chain, step, keys_per_iter, ahead=True,
                  exhausted=lambda chain: jnp.min(carry_ref[chain * blk:, :]) >= SB_DEAD_MASS)
    o_ref[...] = _rms(acc_ref[...], g_ref[...]).astype(o_ref.dtype)


def sb_attention(pa, g, blk=256, n_chain=4, keys_per_iter=1):
    b, s, _ = pa.shape
    bq = blk * n_chain
    return pl.pallas_call(
        functools.partial(_sb_kernel, blk=blk, n_chain=n_chain, keys_per_iter=keys_per_iter),
        grid=(b, N_SB, s // bq),
        in_specs=[pl.BlockSpec((None, bq, HEAD_DIM), lambda b_, h, i: (b_, i, h)),
                  pl.BlockSpec((None, s, HEAD_DIM), lambda b_, h, i: (b_, 0, N_SB + h)),
                  pl.BlockSpec((None, s, HEAD_DIM), lambda b_, h, i: (b_, 0, 2 * N_SB + h)),
                  pl.BlockSpec((2 * blk, blk), lambda b_, h, i: (0, 0)),
                  pl.BlockSpec((1, HEAD_DIM), lambda b_, h, i: (0, 0))],
        out_specs=pl.BlockSpec((None, bq, HEAD_DIM), lambda b_, h, i: (b_, i, h)),
        out_shape=jax.ShapeDtypeStruct((b, s, W_SB), BF16),
        scratch_shapes=[pltpu.VMEM((bq, HEAD_DIM), BF16),
                        pltpu.VMEM((bq, HEAD_DIM), F32),
                        pltpu.VMEM((bq, LANES), F32)],
        compiler_params=_params("parallel", "parallel", "arbitrary"),
        name="sb_attention",
    )(pa, pa, pa, _from_key_matrix(blk), g.reshape(1, HEAD_DIM))


def _diff_kernel(*refs, blk, n_chain, keys_per_iter, lambda_init, n_side):
    q_ref, k_ref, v_ref, lam_ref, g_ref = refs[:5]
    side_in, o_ref = refs[5:5 + n_side], refs[5 + n_side]
    side_out = refs[6 + n_side:6 + 2 * n_side]
    qs_ref, vx_ref, m_ref, acc_ref = refs[6 + 2 * n_side:]
    for src, dst in zip(side_in, side_out):
        dst[...] = src[...].astype(BF16)
    i = pl.program_id(2)
    seq = v_ref.shape[0]

    @pl.when(i == 0)
    def _():
        vx_ref[:, 0:HEAD_DIM] = v_ref[...]
        vx_ref[:, HEAD_DIM:] = jnp.ones((seq, HEAD_DIM), BF16)

    scale = DIFF_QK ** -0.5
    q = q_ref[...].astype(F32) * scale
    lc = lax.broadcasted_iota(jnp.int32, (blk, HEAD_DIM), 1)
    for c in range(n_chain):
        qc = q[c * blk:(c + 1) * blk]
        qs_ref[(2 * c) * blk:(2 * c + 1) * blk, :] = jnp.where(lc < DIFF_QK, qc, 0.0).astype(BF16)
        qs_ref[(2 * c + 1) * blk:(2 * c + 2) * blk, :] = jnp.where(lc >= DIFF_QK, qc, 0.0).astype(BF16)
    m_ref[...] = jnp.full(m_ref.shape, NEG_INF, F32)
    acc_ref[...] = jnp.zeros_like(acc_ref)
    row = lax.broadcasted_iota(jnp.int32, (blk, blk), 0)
    col = lax.broadcasted_iota(jnp.int32, (blk, blk), 1)
    causal = col <= row

    def step(c, j, diag, width=1):
        start = pl.multiple_of(j * blk, blk)
        k = k_ref[pl.ds(start, width * blk), :]
        vx = vx_ref[pl.ds(start, width * blk), :]
        rows = [slice((2 * c + e) * blk, (2 * c + e + 1) * blk) for e in range(2)]
        s = [_dot_nt(qs_ref[r, :], k) for r in rows]
        yield
        alpha, pv = [], []
        for e, r in enumerate(rows):
            se = jnp.where(causal, s[e], NEG_INF) if diag else s[e]
            m_prev = m_ref[r, :]
            m_new = jnp.maximum(m_prev, jnp.max(se, axis=-1, keepdims=True))
            alpha.append(jnp.exp(m_prev - m_new))
            p = jnp.exp((se - _tile(m_new, width * blk // LANES)).astype(BF16))
            m_ref[r, :] = m_new
            pv.append(_dot(p, vx))
        yield
        for e, r in enumerate(rows):
            acc_ref[r, :] = _tile(alpha[e], 2) * acc_ref[r, :] + pv[e]

    _causal_sweep(i, n_chain, step, keys_per_iter, wide=True)
    lq = lam_ref[...]
    lam = (jnp.exp(jnp.sum(lq[0:1] * lq[1:2], axis=-1, keepdims=True))
           - jnp.exp(jnp.sum(lq[2:3] * lq[3:4], axis=-1, keepdims=True)) + lambda_init)
    g = g_ref[...]
    for c in range(n_chain):
        o = []
        for e in range(2):
            acc = acc_ref[(2 * c + e) * blk:(2 * c + e + 1) * blk, :]
            o.append(acc[:, :HEAD_DIM] / acc[:, HEAD_DIM:])
        out = o[0] - lam * o[1]
        o_ref[c * blk:(c + 1) * blk, :] = (_rms(out, g) * (1.0 - lambda_init)).astype(o_ref.dtype)


def diff_attention(pa, lam_vecs, g, lambda_init, side=(), blk=256, n_chain=4, keys_per_iter=4):
    b, s, _ = pa.shape
    c0 = 3 * N_SB
    bq = blk * n_chain
    n_i = s // bq
    side_in, side_out, side_shape = _side_cast_specs(
        side, b * N_DIFF * n_i, lambda b_, h, i: (b_ * N_DIFF + h) * n_i + i)
    out = pl.pallas_call(
        functools.partial(_diff_kernel, blk=blk, n_chain=n_chain, keys_per_iter=keys_per_iter,
                          lambda_init=lambda_init, n_side=len(side)),
        grid=(b, N_DIFF, n_i),
        in_specs=[pl.BlockSpec((None, bq, HEAD_DIM), lambda b_, h, i: (b_, i, c0 + h)),
                  pl.BlockSpec((None, s, HEAD_DIM), lambda b_, h, i: (b_, 0, c0 + N_DIFF + h)),
                  pl.BlockSpec((None, s, HEAD_DIM), lambda b_, h, i: (b_, 0, c0 + 2 * N_DIFF + h)),
                  pl.BlockSpec((4, DIFF_QK), lambda b_, h, i: (0, 0)),
                  pl.BlockSpec((1, HEAD_DIM), lambda b_, h, i: (0, 0))] + side_in,
        out_specs=[pl.BlockSpec((None, bq, HEAD_DIM), lambda b_, h, i: (b_, i, h))] + side_out,
        out_shape=[jax.ShapeDtypeStruct((b, s, W_DIFF), BF16)] + side_shape,
        scratch_shapes=[pltpu.VMEM((2 * bq, HEAD_DIM), BF16),
                        pltpu.VMEM((s, 2 * HEAD_DIM), BF16),
                        pltpu.VMEM((2 * bq, LANES), F32),
                        pltpu.VMEM((2 * bq, 2 * HEAD_DIM), F32)],
        compiler_params=_params("parallel", "parallel", "arbitrary"),
        name="diff_attention",
    )(pa, pa, pa, lam_vecs, g.reshape(1, HEAD_DIM), *[w for w, _ in side])
    return out[0], out[1:]


def _dil_kernel(q_ref, k_ref, v_ref, g_ref, o_ref, qd_ref, kd_ref, vd_ref, q4_ref, k4_ref, v4_ref,
                bias_ref, op_ref, lse_ref, *, seq, group, unroll):
    blk = DIL_SPAN
    scale = HEAD_DIM ** -0.5
    n_blocks = seq // blk
    kd_ref[0:blk, :] = jnp.zeros((blk, HEAD_DIM), BF16)
    vd_ref[0:blk, 0:HEAD_DIM] = jnp.zeros((blk, HEAD_DIM), BF16)
    vd_ref[:, HEAD_DIM:] = jnp.ones((seq + blk, HEAD_DIM), BF16)
    ii = lax.broadcasted_iota(jnp.int32, (blk, 2 * blk), 0)
    jj = lax.broadcasted_iota(jnp.int32, (blk, 2 * blk), 1)
    dist = blk + ii - jj
    in_window = (dist >= 0) & (dist <= DIL_SPAN)
    bias_ref[0] = jnp.where(in_window & (jj >= blk), 0.0, NEG_INF)
    bias_ref[1] = jnp.where(in_window, 0.0, NEG_INF)
    quarter = seq // 4

    for p, r in enumerate(DILATIONS):
        per_seq = n_blocks // r
        shift = per_seq.bit_length() - 1

        def block_of(n, per_seq=per_seq, shift=shift):
            return lax.shift_right_logical(n, shift), n & (per_seq - 1)

        def natural_rows(n, r=r, block_of=block_of):
            c, i = block_of(n)
            src = c + r * blk * i
            if r == 1:
                return pl.ds(pl.multiple_of(src, blk), blk)
            return pl.ds(src, blk, stride=r)

        def gather(n, _, r=r, block_of=block_of, natural_rows=natural_rows):
            here = pl.ds(pl.multiple_of(blk * n, blk), blk)
            if r == 16:
                c, i = block_of(n)
                rows = pl.ds((c & 3) * quarter + lax.shift_right_logical(c, 2) + 4 * blk * i, blk, stride=4)
                q, k, v = q4_ref[rows, :], k4_ref[rows, :], v4_ref[rows, :]
            else:
                rows = natural_rows(n)
                q, k, v = q_ref[rows, :], k_ref[rows, :], v_ref[rows, :]
            if r == 4:
                q4_ref[here, :], k4_ref[here, :], v4_ref[here, :] = q, k, v
            dst = pl.ds(pl.multiple_of(blk * (n + 1), blk), blk)
            kd_ref[dst, :] = k.astype(BF16)
            vd_ref[dst, 0:HEAD_DIM] = v.astype(BF16)
            qd_ref[here, :] = (q * scale).astype(BF16)
            return 0

        lax.fori_loop(0, n_blocks, gather, 0, unroll=unroll)

        def attend(n, p=p, block_of=block_of, natural_rows=natural_rows):
            _, i = block_of(n)
            rows = natural_rows(n)
            q = qd_ref[pl.ds(pl.multiple_of(blk * n, blk), blk), :]
            win = pl.ds(pl.multiple_of(blk * n, blk), 2 * blk)
            s = _dot_nt(q, kd_ref[win, :])
            yield
            s = s + bias_ref[jnp.where(i == 0, 0, 1)]
            m = jnp.max(s, axis=-1, keepdims=True)
            pv = _dot(jnp.exp((s - m).astype(BF16)), vd_ref[win, :])
            yield
            l = pv[:, HEAD_DIM:]
            op_ref[p, rows, :] = pv[:, :HEAD_DIM] / l
            lse_ref[p, rows, :] = m + jnp.log(l)

        def attend_group(t, _, attend=attend):
            _interleave([attend(t * group + u) for u in range(group)])
            return 0

        lax.fori_loop(0, n_blocks // group, attend_group, 0)

    g = g_ref[...]

    def merge(n, _):
        rows = pl.ds(pl.multiple_of(n * blk, blk), blk)
        lses = [lse_ref[p, rows, :] for p in range(len(DILATIONS))]
        top = functools.reduce(jnp.maximum, lses)
        ws = [jnp.exp(x - top) for x in lses]
        den = functools.reduce(lambda a, b_: a + b_, ws)
        num = functools.reduce(lambda a, b_: a + b_,
                               [op_ref[p, rows, :] * ws[p] for p in range(len(DILATIONS))])
        o_ref[rows, :] = _rms(num * (1.0 / den), g).astype(o_ref.dtype)
        return 0

    lax.fori_loop(0, n_blocks, merge, 0, unroll=2 * unroll)


def dil_attention(pd, g, group=32, unroll=4):
    b, s, _ = pd.shape
    n_pat = len(DILATIONS)
    assert DILATIONS == (1, 4, 16) and (s // DIL_SPAN) % group == 0
    return pl.pallas_call(
        functools.partial(_dil_kernel, seq=s, group=group, unroll=unroll),
        grid=(b, N_DIL),
        in_specs=[pl.BlockSpec((None, s, HEAD_DIM), lambda b_, h: (b_, 0, h)),
                  pl.BlockSpec((None, s, HEAD_DIM), lambda b_, h: (b_, 0, N_DIL + h)),
                  pl.BlockSpec((None, s, HEAD_DIM), lambda b_, h: (b_, 0, 2 * N_DIL + h)),
                  pl.BlockSpec((1, HEAD_DIM), lambda b_, h: (0, 0))],
        out_specs=pl.BlockSpec((None, s, HEAD_DIM), lambda b_, h: (b_, 0, h)),
        out_shape=jax.ShapeDtypeStruct((b, s, W_DIL), BF16),
        scratch_shapes=[pltpu.VMEM((s, HEAD_DIM), BF16),
                        pltpu.VMEM((s + DIL_SPAN, HEAD_DIM), BF16),
                        pltpu.VMEM((s + DIL_SPAN, 2 * HEAD_DIM), BF16),
                        pltpu.VMEM((s, HEAD_DIM), F32),
                        pltpu.VMEM((s, HEAD_DIM), F32),
                        pltpu.VMEM((s, HEAD_DIM), F32),
                        pltpu.VMEM((2, DIL_SPAN, 2 * DIL_SPAN), F32),
                        pltpu.VMEM((n_pat, s, HEAD_DIM), F32),
                        pltpu.VMEM((n_pat, s, HEAD_DIM), F32)],
        compiler_params=_params("parallel", "arbitrary"),
        name="dil_attention",
    )(pd, pd, pd, g.reshape(1, HEAD_DIM))


def kernel(x, positions, norm_mix_g, w_in, lambda_q1, lambda_k1, lambda_q2, lambda_k2, g_sb_out, g_diff_out, g_dil_out, w_out, norm_ffn_g, w_gate, w_up, w_down, norm_final_g):
    batch, seq, d = x.shape
    m = batch * seq
    depth = w_in.shape[0]
    xf = x.reshape(m, d)
    tabs, (w_in_bf,) = rope_tables(positions.reshape(m, 1), [(w_in, 0)])
    for layer in range(depth):
        pa, pd = in_proj(xf, norm_mix_g[layer], w_in_bf, tabs)
        pa = pa.reshape(batch, seq, W_A)
        pd = pd.reshape(batch, seq, 3 * W_DIL)
        lambda_init = 0.8 - 0.6 * math.exp(-0.3 * layer)
        lam_vecs = jnp.stack([lambda_q1[layer], lambda_k1[layer],
                              lambda_q2[layer], lambda_k2[layer]]).astype(F32)
        o_sb = sb_attention(pa, g_sb_out[layer]).reshape(m, W_SB)
        side = [(w, layer) for w in (w_out, w_gate, w_up, w_down)]
        side += [(w_in, layer + 1)] if layer + 1 < depth else []
        o_df, converted = diff_attention(pa, lam_vecs, g_diff_out[layer], lambda_init, side)
        o_df = o_df.reshape(m, W_DIFF)
        w_out_bf, w_gate_bf, w_up_bf, w_down_bf = converted[:4]
        w_in_bf = converted[4] if layer + 1 < depth else None
        o_dl = dil_attention(pd, g_dil_out[layer]).reshape(m, W_DIL)
        xf = mlp_block(o_sb, o_df, o_dl, w_out_bf, xf, norm_ffn_g[layer], w_gate_bf, w_up_bf, w_down_bf,
                       final_g=norm_final_g if layer == depth - 1 else None)
    return xf.reshape(batch, seq, d)
```

```python
import functools
import math

import jax
import jax.numpy as jnp
import numpy as np
from jax import lax
from jax.experimental import pallas as pl
from jax.experimental.pallas import tpu as pltpu

F32 = jnp.float32
BF16 = jnp.bfloat16

HEAD_DIM = 128
N_SB = 4
N_DIFF = 4
N_DIL = 8
W_SB = N_SB * HEAD_DIM
W_DIFF = N_DIFF * HEAD_DIM
W_DIL = N_DIL * HEAD_DIM
DIFF_QK = HEAD_DIM // 2
ROPE_THETA = 500000.0
ROPE_FRACTION = 0.25
DILATIONS = (1, 4, 16)
DIL_SPAN = 128
NORM_EPS = 1e-6
LANES = 128
VMEM_LIMIT = 56 * 1024 * 1024

NEG_INF = float("-inf")
LOG2_E = math.log2(math.e)
SB_DEAD_MASS = 106.0


def _params(*sem):
    return pltpu.CompilerParams(dimension_semantics=sem, vmem_limit_bytes=VMEM_LIMIT)


def _dot(a, b):
    return jnp.dot(a, b, preferred_element_type=F32)


def _dot_nt(a, b):
    return lax.dot_general(a, b, (((1,), (1,)), ((), ())), preferred_element_type=F32)


def _tile(x, n):
    return x if n == 1 else jnp.concatenate([x] * n, axis=1)


def _rms(x, g):
    return x * lax.rsqrt(jnp.mean(x * x, axis=-1, keepdims=True) + NORM_EPS) * g


def _side_cast_specs(side, n_steps, step_index):
    in_specs, out_specs, out_shape = [], [], []
    for w, layer in side:
        _, r, c = w.shape
        rows = r // n_steps
        assert rows * n_steps == r and rows % 16 == 0
        in_specs.append(pl.BlockSpec((None, rows, c), lambda *g, layer=layer: (layer, step_index(*g), 0)))
        out_specs.append(pl.BlockSpec((rows, c), lambda *g: (step_index(*g), 0)))
        out_shape.append(jax.ShapeDtypeStruct((r, c), BF16))
    return in_specs, out_specs, out_shape


_ROPE_LAYOUTS = ((DIFF_QK, int(DIFF_QK * ROPE_FRACTION) // 2), (HEAD_DIM, int(HEAD_DIM * ROPE_FRACTION) // 2))


def _rope_lane_consts():
    freqs = [ROPE_THETA ** (-jnp.arange(half, dtype=F32) / half) for _, half in _ROPE_LAYOUTS]
    used = sum(half for _, half in _ROPE_LAYOUTS)
    row = jnp.concatenate(freqs + [jnp.zeros((LANES - used,), F32)])
    return jnp.concatenate([row[None, :], jnp.zeros((7, LANES), F32)])


def _rope_table_kernel(*refs, n_side):
    pos_ref, c_ref = refs[:2]
    side_in, o_ref, side_out = refs[2:2 + n_side], refs[2 + n_side], refs[3 + n_side:]
    for src, dst in zip(side_in, side_out):
        dst[...] = src[...].astype(BF16)
    ang = pos_ref[...].astype(F32) * c_ref[0:1, :]
    cos, sin = jnp.cos(ang), jnp.sin(ang)
    lane = lax.broadcasted_iota(jnp.int32, ang.shape, 1)
    src = 0
    for k, (chunk, half) in enumerate(_ROPE_LAYOUTS):
        c_tab = jnp.ones_like(ang)
        lo_tab = jnp.zeros_like(ang)
        hi_tab = jnp.zeros_like(ang)
        for start in range(0, LANES, chunk):
            for part, dst in enumerate((start, start + half)):
                here = (lane >= dst) & (lane < dst + half)
                shift = (dst - src) % LANES
                c_here = pltpu.roll(cos, shift, 1) if shift else cos
                s_here = pltpu.roll(sin, shift, 1) if shift else sin
                c_tab = jnp.where(here, c_here, c_tab)
                if part == 0:
                    lo_tab = jnp.where(here, -s_here, lo_tab)
                else:
                    hi_tab = jnp.where(here, s_here, hi_tab)
        o_ref[3 * k], o_ref[3 * k + 1], o_ref[3 * k + 2] = c_tab, lo_tab, hi_tab
        src += half


def rope_tables(positions, side=(), tm=512):
    m = positions.shape[0]
    side_in, side_out, side_shape = _side_cast_specs(side, m // tm, lambda i: i)
    out = pl.pallas_call(
        functools.partial(_rope_table_kernel, n_side=len(side)),
        grid=(m // tm,),
        in_specs=[pl.BlockSpec((tm, 1), lambda i: (i, 0)),
                  pl.BlockSpec((8, LANES), lambda i: (0, 0))] + side_in,
        out_specs=[pl.BlockSpec((6, tm, LANES), lambda i: (0, i, 0))] + side_out,
        out_shape=[jax.ShapeDtypeStruct((6, m, LANES), F32)] + side_shape,
        compiler_params=_params("parallel"),
        name="rope_tables",
    )(positions, _rope_lane_consts(), *[w for w, _ in side])
    return out[0], out[1:]


ROPE_NONE, ROPE_DIFF, ROPE_DIL = 0, 1, 2
_ROPE_SHIFT = {ROPE_DIFF: _ROPE_LAYOUTS[0][1], ROPE_DIL: _ROPE_LAYOUTS[1][1]}
W_A = 3 * (W_SB + W_DIFF)
PROJ_TN = 512
_PROJ_KINDS = (ROPE_NONE,) * 3 + (ROPE_DIFF,) * 2 + (ROPE_NONE,) + (ROPE_DIL,) * 4 + (ROPE_NONE,) * 2


def _resident(shape):
    return pl.BlockSpec(shape, lambda *_: (0,) * len(shape), pipeline_mode=pl.Buffered(1))


def _proj_kernel(x_ref, g_ref, w_ref, tab_ref, oa_ref, od_ref, h_ref):
    h_ref[...] = _rms(x_ref[...], g_ref[...]).astype(BF16)
    tn = PROJ_TN
    for idx, kind in enumerate(_PROJ_KINDS):
        acc = _dot(h_ref[...], w_ref[:, idx * tn:(idx + 1) * tn])
        o_ref, col = (oa_ref, idx * tn) if idx * tn < W_A else (od_ref, idx * tn - W_A)
        if kind == ROPE_NONE:
            o_ref[:, col:col + tn] = acc.astype(o_ref.dtype)
            continue
        base = 3 * (kind - 1)
        shift = _ROPE_SHIFT[kind]
        c, lo, hi = tab_ref[base], tab_ref[base + 1], tab_ref[base + 2]
        for q in range(tn // LANES):
            xk = acc[:, q * LANES:(q + 1) * LANES]
            rot = xk * c + pltpu.roll(xk, LANES - shift, 1) * lo + pltpu.roll(xk, shift, 1) * hi
            o_ref[:, col + q * LANES:col + (q + 1) * LANES] = rot.astype(o_ref.dtype)


def in_proj(x, g, w, tabs, tm=256):
    m, d = x.shape
    n = w.shape[1]
    assert n == len(_PROJ_KINDS) * PROJ_TN
    return pl.pallas_call(
        _proj_kernel,
        grid=(m // tm,),
        in_specs=[pl.BlockSpec((tm, d), lambda i: (i, 0)),
                  _resident((1, d)),
                  _resident((d, n)),
                  pl.BlockSpec((6, tm, LANES), lambda i: (0, i, 0))],
        out_specs=[pl.BlockSpec((tm, W_A), lambda i: (i, 0)),
                   pl.BlockSpec((tm, n - W_A), lambda i: (i, 0))],
        out_shape=[jax.ShapeDtypeStruct((m, W_A), BF16),
                   jax.ShapeDtypeStruct((m, n - W_A), F32)],
        scratch_shapes=[pltpu.VMEM((tm, d), BF16)],
        compiler_params=_params("parallel"),
        name="in_proj",
    )(x, g.reshape(1, d), w, tabs)


def _mlp_kernel(*refs, final_norm):
    if final_norm:
        sb_ref, df_ref, dl_ref, wo_ref, x_ref, g_ref, wg_ref, wu_ref, wd_ref, gf_ref, o_ref, h_ref = refs
    else:
        sb_ref, df_ref, dl_ref, wo_ref, x_ref, g_ref, wg_ref, wu_ref, wd_ref, o_ref, h_ref = refs
    j = pl.program_id(1)

    @pl.when(j == 0)
    def _():
        mixed = (_dot(sb_ref[...], wo_ref[0:W_SB, :])
                 + _dot(df_ref[...], wo_ref[W_SB:W_SB + W_DIFF, :])
                 + _dot(dl_ref[...], wo_ref[W_SB + W_DIFF:, :]))
        xn = x_ref[...] + mixed
        o_ref[...] = xn
        h_ref[...] = _rms(xn, g_ref[...]).astype(BF16)

    h = h_ref[...]
    gate = _dot(h, wg_ref[...])
    up = _dot(h, wu_ref[...])
    act = (gate * jax.nn.sigmoid(gate)) * up
    o_ref[...] += _dot(act.astype(BF16), wd_ref[...])

    if final_norm:
        @pl.when(j == pl.num_programs(1) - 1)
        def _():
            o_ref[...] = _rms(o_ref[...], gf_ref[...])


def mlp_block(o_sb, o_df, o_dl, w_out, x, g, wg, wu, wd, final_g=None, tm=512, tf=512):
    m, d = x.shape
    f = wg.shape[1]
    final_norm = final_g is not None
    in_specs = [pl.BlockSpec((tm, W_SB), lambda i, j: (i, 0)),
                pl.BlockSpec((tm, W_DIFF), lambda i, j: (i, 0)),
                pl.BlockSpec((tm, W_DIL), lambda i, j: (i, 0)),
                _resident(w_out.shape),
                pl.BlockSpec((tm, d), lambda i, j: (i, 0)),
                _resident((1, d)),
                pl.BlockSpec((d, tf), lambda i, j: (0, j)),
                pl.BlockSpec((d, tf), lambda i, j: (0, j)),
                pl.BlockSpec((tf, d), lambda i, j: (j, 0))]
    args = [o_sb, o_df, o_dl, w_out, x, g.reshape(1, d), wg, wu, wd]
    if final_norm:
        in_specs.append(_resident((1, d)))
        args.append(final_g.reshape(1, d))
    return pl.pallas_call(
        functools.partial(_mlp_kernel, final_norm=final_norm),
        grid=(m // tm, f // tf),
        in_specs=in_specs,
        out_specs=pl.BlockSpec((tm, d), lambda i, j: (i, 0)),
        out_shape=jax.ShapeDtypeStruct((m, d), F32),
        scratch_shapes=[pltpu.VMEM((tm, d), BF16)],
        compiler_params=_params("parallel", "arbitrary"),
        name="mlp_block",
    )(*args)


def _causal_sweep(i, n_chain, step, keys_per_iter, wide=False, exhausted=None, ahead=False):
    assert n_chain % keys_per_iter == 0 and not (ahead and (wide or keys_per_iter != 1))
    base = n_chain * i
    if wide:
        _interleave([step(c, base, True, c + 1) for c in range(n_chain)])
    else:
        pairs = [(c, d) for d in range(n_chain - 1, -1, -1) for c in range(d, n_chain)]
        near = [step(c, base + d, c == d) for c, d in pairs if c - d <= 1]
        if ahead:
            near.append(step(0, jnp.maximum(base - 1, 0), False, i > 0))
        _interleave(near)
        far = [(c, d) for c, d in pairs if c - d > 1]

        def far_blocks():
            _interleave([step(c, base + d, False) for c, d in far])

        if far and exhausted is not None:
            pl.when(jnp.logical_not(exhausted(2)))(far_blocks)
        elif far:
            far_blocks()

    def body(t, _):
        first = base - 1 - t * keys_per_iter
        if wide:
            _interleave([step(c, first - (keys_per_iter - 1), False, keys_per_iter) for c in range(n_chain)])
        elif ahead:
            _interleave([step(0, jnp.maximum(first - 1, 0), False, first >= 1)]
                        + [step(c, first, False) for c in range(1, n_chain)])
        else:
            _interleave([step(c, first - u, False) for u in range(keys_per_iter) for c in range(n_chain)])
        return 0

    trips = (n_chain // keys_per_iter) * i
    if exhausted is None:
        lax.fori_loop(0, trips, body, 0)
        return

    def more(state):
        t, done = state
        return (t < trips) & jnp.logical_not(done)

    def advance(state):
        body(state[0], 0)
        return state[0] + 1, exhausted(0)

    lax.while_loop(more, advance, (0, exhausted(0)))


def _interleave(steps):
    steps = list(steps)
    while steps:
        alive = []
        for g in steps:
            try:
                next(g)
                alive.append(g)
            except StopIteration:
                pass
        steps = alive


def _from_key_matrix(blk):
    tri = np.tril(np.ones((blk, blk), np.float32))
    return jnp.asarray(np.concatenate([tri, tri], axis=0), BF16)


def _sb_kernel(q_ref, k_ref, v_ref, later_ref, g_ref, o_ref, qs_ref, acc_ref, carry_ref, *,
               blk, n_chain, keys_per_iter):
    i = pl.program_id(2)
    scale = HEAD_DIM ** -0.5
    qs_ref[...] = (q_ref[...].astype(F32) * scale).astype(BF16)
    acc_ref[...] = jnp.zeros_like(acc_ref)
    carry_ref[...] = jnp.zeros_like(carry_ref)
    row = lax.broadcasted_iota(jnp.int32, (blk, blk), 0)
    col = lax.broadcasted_iota(jnp.int32, (blk, blk), 1)
    strict = col < row

    def step(c, j, diag, live=None):
        rows = slice(c * blk, (c + 1) * blk)
        start = pl.multiple_of(j * blk, blk)
        k = k_ref[pl.ds(start, blk), :]
        v = v_ref[pl.ds(start, blk), :]
        z = _dot_nt(qs_ref[rows, :], k)
        yield
        sp = jnp.maximum(z, 0.0) + jnp.log(1.0 + jnp.exp2(jnp.abs(z) * -LOG2_E))
        spm = jnp.where(strict, sp, 0.0) if diag else sp
        if live is not None:
            spm = jnp.where(live, spm, 0.0)
        hi = spm.astype(BF16)
        lo = (spm - hi.astype(F32)).astype(BF16)
        from_key = _dot(jnp.concatenate([hi, lo], axis=1), later_ref[...])
        yield
        loga = z - (from_key + _tile(carry_ref[rows, :], blk // LANES))
        if diag:
            loga = jnp.where(strict, loga, NEG_INF)
        if live is not None:
            loga = jnp.where(live, loga, NEG_INF)
        pv = _dot(jnp.exp(loga).astype(BF16), v)
        carry_ref[rows, :] += from_key[:, 0:1]
        yield
        acc_ref[rows, :] += pv

    _causal_sweep(i, n_chain, step, keys_per_iter, ahead=True,
                  exhausted=lambda chain: jnp.min(carry_ref[chain * blk:, :]) >= SB_DEAD_MASS)
    o_ref[...] = _rms(acc_ref[...], g_ref[...]).astype(o_ref.dtype)


def sb_attention(pa, g, blk=256, n_chain=4, keys_per_iter=1):
    b, s, _ = pa.shape
    bq = blk * n_chain
    return pl.pallas_call(
        functools.partial(_sb_kernel, blk=blk, n_chain=n_chain, keys_per_iter=keys_per_iter),
        grid=(b, N_SB, s // bq),
        in_specs=[pl.BlockSpec((None, bq, HEAD_DIM), lambda b_, h, i: (b_, i, h)),
                  pl.BlockSpec((None, s, HEAD_DIM), lambda b_, h, i: (b_, 0, N_SB + h)),
                  pl.BlockSpec((None, s, HEAD_DIM), lambda b_, h, i: (b_, 0, 2 * N_SB + h)),
                  pl.BlockSpec((2 * blk, blk), lambda b_, h, i: (0, 0)),
                  pl.BlockSpec((1, HEAD_DIM), lambda b_, h, i: (0, 0))],
        out_specs=pl.BlockSpec((None, bq, HEAD_DIM), lambda b_, h, i: (b_, i, h)),
        out_shape=jax.ShapeDtypeStruct((b, s, W_SB), BF16),
        scratch_shapes=[pltpu.VMEM((bq, HEAD_DIM), BF16),
                        pltpu.VMEM((bq, HEAD_DIM), F32),
                        pltpu.VMEM((bq, LANES), F32)],
        compiler_params=_params("parallel", "parallel", "arbitrary"),
        name="sb_attention",
    )(pa, pa, pa, _from_key_matrix(blk), g.reshape(1, HEAD_DIM))


def _diff_kernel(*refs, blk, n_chain, keys_per_iter, lambda_init, n_side):
    q_ref, k_ref, v_ref, lam_ref, g_ref = refs[:5]
    side_in, o_ref = refs[5:5 + n_side], refs[5 + n_side]
    side_out = refs[6 + n_side:6 + 2 * n_side]
    qs_ref, vx_ref, m_ref, acc_ref = refs[6 + 2 * n_side:]
    for src, dst in zip(side_in, side_out):
        dst[...] = src[...].astype(BF16)
    i = pl.program_id(2)
    seq = v_ref.shape[0]

    @pl.when(i == 0)
    def _():
        vx_ref[:, 0:HEAD_DIM] = v_ref[...]
        vx_ref[:, HEAD_DIM:] = jnp.ones((seq, HEAD_DIM), BF16)

    scale = DIFF_QK ** -0.5
    q = q_ref[...].astype(F32) * scale
    lc = lax.broadcasted_iota(jnp.int32, (blk, HEAD_DIM), 1)
    for c in range(n_chain):
        qc = q[c * blk:(c + 1) * blk]
        qs_ref[(2 * c) * blk:(2 * c + 1) * blk, :] = jnp.where(lc < DIFF_QK, qc, 0.0).astype(BF16)
        qs_ref[(2 * c + 1) * blk:(2 * c + 2) * blk, :] = jnp.where(lc >= DIFF_QK, qc, 0.0).astype(BF16)
    row = lax.broadcasted_iota(jnp.int32, (blk, blk), 0)
    col = lax.broadcasted_iota(jnp.int32, (blk, blk), 1)
    causal = col <= row

    def step(c, j, diag, width=1):
        start = pl.multiple_of(j * blk, blk)
        k = k_ref[pl.ds(start, width * blk), :]
        vx = vx_ref[pl.ds(start, width * blk), :]
        rows = [slice((2 * c + e) * blk, (2 * c + e + 1) * blk) for e in range(2)]
        s = [_dot_nt(qs_ref[r, :], k) for r in rows]
        yield
        alpha, pv = [], []
        for e, r in enumerate(rows):
            se = s[e]
            if diag:
                own = jnp.where(causal, se[:, (width - 1) * blk:], NEG_INF)
                se = own if width == 1 else jnp.concatenate([se[:, :(width - 1) * blk], own], axis=1)
            top = jnp.max(se, axis=-1, keepdims=True)
            if diag:
                m_new = jnp.broadcast_to(top, (blk, LANES))
            else:
                m_prev = m_ref[r, :]
                m_new = jnp.maximum(m_prev, top)
                alpha.append(jnp.exp(m_prev - m_new))
            p = jnp.exp((se - _tile(m_new, width * blk // LANES)).astype(BF16))
            m_ref[r, :] = m_new
            pv.append(_dot(p, vx))
        yield
        for e, r in enumerate(rows):
            acc_ref[r, :] = pv[e] if diag else _tile(alpha[e], 2) * acc_ref[r, :] + pv[e]

    _causal_sweep(i, n_chain, step, keys_per_iter, wide=True)
    lq = lam_ref[...]
    lam = (jnp.exp(jnp.sum(lq[0:1] * lq[1:2], axis=-1, keepdims=True))
           - jnp.exp(jnp.sum(lq[2:3] * lq[3:4], axis=-1, keepdims=True)) + lambda_init)
    g = g_ref[...]
    for c in range(n_chain):
        o = []
        for e in range(2):
            acc = acc_ref[(2 * c + e) * blk:(2 * c + e + 1) * blk, :]
            o.append(acc[:, :HEAD_DIM] / acc[:, HEAD_DIM:])
        out = o[0] - lam * o[1]
        o_ref[c * blk:(c + 1) * blk, :] = (_rms(out, g) * (1.0 - lambda_init)).astype(o_ref.dtype)


def diff_attention(pa, lam_vecs, g, lambda_init, side=(), blk=256, n_chain=4, keys_per_iter=4):
    b, s, _ = pa.shape
    c0 = 3 * N_SB
    bq = blk * n_chain
    n_i = s // bq
    side_in, side_out, side_shape = _side_cast_specs(
        side, b * N_DIFF * n_i, lambda b_, h, i: (b_ * N_DIFF + h) * n_i + i)
    out = pl.pallas_call(
        functools.partial(_diff_kernel, blk=blk, n_chain=n_chain, keys_per_iter=keys_per_iter,
                          lambda_init=lambda_init, n_side=len(side)),
        grid=(b, N_DIFF, n_i),
        in_specs=[pl.BlockSpec((None, bq, HEAD_DIM), lambda b_, h, i: (b_, i, c0 + h)),
                  pl.BlockSpec((None, s, HEAD_DIM), lambda b_, h, i: (b_, 0, c0 + N_DIFF + h)),
                  pl.BlockSpec((None, s, HEAD_DIM), lambda b_, h, i: (b_, 0, c0 + 2 * N_DIFF + h)),
                  pl.BlockSpec((4, DIFF_QK), lambda b_, h, i: (0, 0)),
                  pl.BlockSpec((1, HEAD_DIM), lambda b_, h, i: (0, 0))] + side_in,
        out_specs=[pl.BlockSpec((None, bq, HEAD_DIM), lambda b_, h, i: (b_, i, h))] + side_out,
        out_shape=[jax.ShapeDtypeStruct((b, s, W_DIFF), BF16)] + side_shape,
        scratch_shapes=[pltpu.VMEM((2 * bq, HEAD_DIM), BF16),
                        pltpu.VMEM((s, 2 * HEAD_DIM), BF16),
                        pltpu.VMEM((2 * bq, LANES), F32),
                        pltpu.VMEM((2 * bq, 2 * HEAD_DIM), F32)],
        compiler_params=_params("parallel", "parallel", "arbitrary"),
        name="diff_attention",
    )(pa, pa, pa, lam_vecs, g.reshape(1, HEAD_DIM), *[w for w, _ in side])
    return out[0], out[1:]


def _dil_kernel(q_ref, k_ref, v_ref, g_ref, o_ref, qd_ref, kd_ref, vd_ref, q4_ref, k4_ref, v4_ref,
                bias_ref, op_ref, lse_ref, *, seq, group, unroll):
    blk = DIL_SPAN
    scale = HEAD_DIM ** -0.5
    n_blocks = seq // blk
    kd_ref[0:blk, :] = jnp.zeros((blk, HEAD_DIM), BF16)
    vd_ref[0:blk, 0:HEAD_DIM] = jnp.zeros((blk, HEAD_DIM), BF16)
    vd_ref[:, HEAD_DIM:] = jnp.ones((seq + blk, HEAD_DIM), BF16)
    ii = lax.broadcasted_iota(jnp.int32, (blk, 2 * blk), 0)
    jj = lax.broadcasted_iota(jnp.int32, (blk, 2 * blk), 1)
    dist = blk + ii - jj
    in_window = (dist >= 0) & (dist <= DIL_SPAN)
    bias_ref[0] = jnp.where(in_window & (jj >= blk), 0.0, NEG_INF)
    bias_ref[1] = jnp.where(in_window, 0.0, NEG_INF)
    quarter = seq // 4

    for p, r in enumerate(DILATIONS):
        per_seq = n_blocks // r
        shift = per_seq.bit_length() - 1

        def block_of(n, per_seq=per_seq, shift=shift):
            return lax.shift_right_logical(n, shift), n & (per_seq - 1)

        def natural_rows(n, r=r, block_of=block_of):
            c, i = block_of(n)
            src = c + r * blk * i
            if r == 1:
                return pl.ds(pl.multiple_of(src, blk), blk)
            return pl.ds(src, blk, stride=r)

        def gather(n, _, r=r, block_of=block_of, natural_rows=natural_rows):
            here = pl.ds(pl.multiple_of(blk * n, blk), blk)
            if r == 16:
                c, i = block_of(n)
                rows = pl.ds((c & 3) * quarter + lax.shift_right_logical(c, 2) + 4 * blk * i, blk, stride=4)
                q, k, v = q4_ref[rows, :], k4_ref[rows, :], v4_ref[rows, :]
            else:
                rows = natural_rows(n)
                q, k, v = q_ref[rows, :], k_ref[rows, :], v_ref[rows, :]
            if r == 4:
                q4_ref[here, :], k4_ref[here, :], v4_ref[here, :] = q, k, v
            dst = pl.ds(pl.multiple_of(blk * (n + 1), blk), blk)
            kd_ref[dst, :] = k.astype(BF16)
            vd_ref[dst, 0:HEAD_DIM] = v.astype(BF16)
            qd_ref[here, :] = (q * scale).astype(BF16)
            return 0

        lax.fori_loop(0, n_blocks, gather, 0, unroll=unroll)

        def attend(n, p=p, block_of=block_of, natural_rows=natural_rows):
            _, i = block_of(n)
            rows = natural_rows(n)
            q = qd_ref[pl.ds(pl.multiple_of(blk * n, blk), blk), :]
            win = pl.ds(pl.multiple_of(blk * n, blk), 2 * blk)
            s = _dot_nt(q, kd_ref[win, :])
            yield
            s = s + bias_ref[jnp.where(i == 0, 0, 1)]
            m = jnp.max(s, axis=-1, keepdims=True)
            pv = _dot(jnp.exp((s - m).astype(BF16)), vd_ref[win, :])
            yield
            l = pv[:, HEAD_DIM:]
            op_ref[p, rows, :] = pv[:, :HEAD_DIM] / l
            lse_ref[p, rows, :] = m + jnp.log(l)

        def attend_group(t, _, attend=attend):
            _interleave([attend(t * group + u) for u in range(group)])
            return 0

        lax.fori_loop(0, n_blocks // group, attend_group, 0)

    g = g_ref[...]

    def merge(n, _):
        rows = pl.ds(pl.multiple_of(n * blk, blk), blk)
        lses = [lse_ref[p, rows, :] for p in range(len(DILATIONS))]
        top = functools.reduce(jnp.maximum, lses)
        ws = [jnp.exp(x - top) for x in lses]
        den = functools.reduce(lambda a, b_: a + b_, ws)
        num = functools.reduce(lambda a, b_: a + b_,
                               [op_ref[p, rows, :] * ws[p] for p in range(len(DILATIONS))])
        o_ref[rows, :] = _rms(num * (1.0 / den), g).astype(o_ref.dtype)
        return 0

    lax.fori_loop(0, n_blocks, merge, 0, unroll=2 * unroll)


def dil_attention(pd, g, group=32, unroll=4):
    b, s, _ = pd.shape
    n_pat = len(DILATIONS)
    assert DILATIONS == (1, 4, 16) and (s // DIL_SPAN) % group == 0
    return pl.pallas_call(
        functools.partial(_dil_kernel, seq=s, group=group, unroll=unroll),
        grid=(b, N_DIL),
        in_specs=[pl.BlockSpec((None, s, HEAD_DIM), lambda b_, h: (b_, 0, h)),
                  pl.BlockSpec((None, s, HEAD_DIM), lambda b_, h: (b_, 0, N_DIL + h)),
                  pl.BlockSpec((None, s, HEAD_DIM), lambda b_, h: (b_, 0, 2 * N_DIL + h)),
                  pl.BlockSpec((1, HEAD_DIM), lambda b_, h: (0, 0))],
        out_specs=pl.BlockSpec((None, s, HEAD_DIM), lambda b_, h: (b_, 0, h)),
        out_shape=jax.ShapeDtypeStruct((b, s, W_DIL), BF16),
        scratch_shapes=[pltpu.VMEM((s, HEAD_DIM), BF16),
                        pltpu.VMEM((s + DIL_SPAN, HEAD_DIM), BF16),
                        pltpu.VMEM((s + DIL_SPAN, 2 * HEAD_DIM), BF16),
                        pltpu.VMEM((s, HEAD_DIM), F32),
                        pltpu.VMEM((s, HEAD_DIM), F32),
                        pltpu.VMEM((s, HEAD_DIM), F32),
                        pltpu.VMEM((2, DIL_SPAN, 2 * DIL_SPAN), F32),
                        pltpu.VMEM((n_pat, s, HEAD_DIM), F32),
                        pltpu.VMEM((n_pat, s, HEAD_DIM), F32)],
        compiler_params=_params("parallel", "arbitrary"),
        name="dil_attention",
    )(pd, pd, pd, g.reshape(1, HEAD_DIM))


def kernel(x, positions, norm_mix_g, w_in, lambda_q1, lambda_k1, lambda_q2, lambda_k2, g_sb_out, g_diff_out, g_dil_out, w_out, norm_ffn_g, w_gate, w_up, w_down, norm_final_g):
    batch, seq, d = x.shape
    m = batch * seq
    depth = w_in.shape[0]
    xf = x.reshape(m, d)
    tabs, (w_in_bf,) = rope_tables(positions.reshape(m, 1), [(w_in, 0)])
    for layer in range(depth):
        pa, pd = in_proj(xf, norm_mix_g[layer], w_in_bf, tabs)
        pa = pa.reshape(batch, seq, W_A)
        pd = pd.reshape(batch, seq, 3 * W_DIL)
        lambda_init = 0.8 - 0.6 * math.exp(-0.3 * layer)
        lam_vecs = jnp.stack([lambda_q1[layer], lambda_k1[layer],
                              lambda_q2[layer], lambda_k2[layer]]).astype(F32)
        o_sb = sb_attention(pa, g_sb_out[layer]).reshape(m, W_SB)
        side = [(w, layer) for w in (w_out, w_gate, w_up, w_down)]
        side += [(w_in, layer + 1)] if layer + 1 < depth else []
        o_df, converted = diff_attention(pa, lam_vecs, g_diff_out[layer], lambda_init, side)
        o_df = o_df.reshape(m, W_DIFF)
        w_out_bf, w_gate_bf, w_up_bf, w_down_bf = converted[:4]
        w_in_bf = converted[4] if layer + 1 < depth else None
        o_dl = dil_attention(pd, g_dil_out[layer]).reshape(m, W_DIL)
        xf = mlp_block(o_sb, o_df, o_dl, w_out_bf, xf, norm_ffn_g[layer], w_gate_bf, w_up_bf, w_down_bf,
                       final_g=norm_final_g if layer == depth - 1 else None)
    return xf.reshape(batch, seq, d)
```

```python
import functools
import math

import jax
import jax.numpy as jnp
import numpy as np
from jax import lax
from jax.experimental import pallas as pl
from jax.experimental.pallas import tpu as pltpu

F32 = jnp.float32
BF16 = jnp.bfloat16

HEAD_DIM = 128
N_SB = 4
N_DIFF = 4
N_DIL = 8
W_SB = N_SB * HEAD_DIM
W_DIFF = N_DIFF * HEAD_DIM
W_DIL = N_DIL * HEAD_DIM
DIFF_QK = HEAD_DIM // 2
ROPE_THETA = 500000.0
ROPE_FRACTION = 0.25
DILATIONS = (1, 4, 16)
DIL_SPAN = 128
NORM_EPS = 1e-6
LANES = 128
VMEM_LIMIT = 56 * 1024 * 1024

NEG_INF = float("-inf")
LOG2_E = math.log2(math.e)
SB_DEAD_MASS = 106.0


def _params(*sem):
    return pltpu.CompilerParams(dimension_semantics=sem, vmem_limit_bytes=VMEM_LIMIT)


def _dot(a, b):
    return jnp.dot(a, b, preferred_element_type=F32)


def _dot_nt(a, b):
    return lax.dot_general(a, b, (((1,), (1,)), ((), ())), preferred_element_type=F32)


def _tile(x, n):
    return x if n == 1 else jnp.concatenate([x] * n, axis=1)


def _rms(x, g):
    return x * lax.rsqrt(jnp.mean(x * x, axis=-1, keepdims=True) + NORM_EPS) * g


def _side_cast_specs(side, n_steps, step_index):
    in_specs, out_specs, out_shape = [], [], []
    for w, layer in side:
        _, r, c = w.shape
        rows = r // n_steps
        assert rows * n_steps == r and rows % 16 == 0
        in_specs.append(pl.BlockSpec((None, rows, c), lambda *g, layer=layer: (layer, step_index(*g), 0)))
        out_specs.append(pl.BlockSpec((rows, c), lambda *g: (step_index(*g), 0)))
        out_shape.append(jax.ShapeDtypeStruct((r, c), BF16))
    return in_specs, out_specs, out_shape


_ROPE_LAYOUTS = ((DIFF_QK, int(DIFF_QK * ROPE_FRACTION) // 2), (HEAD_DIM, int(HEAD_DIM * ROPE_FRACTION) // 2))


def _rope_lane_consts():
    freqs = [ROPE_THETA ** (-jnp.arange(half, dtype=F32) / half) for _, half in _ROPE_LAYOUTS]
    used = sum(half for _, half in _ROPE_LAYOUTS)
    row = jnp.concatenate(freqs + [jnp.zeros((LANES - used,), F32)])
    return jnp.concatenate([row[None, :], jnp.zeros((7, LANES), F32)])


def _rope_table_kernel(*refs, n_side):
    pos_ref, c_ref = refs[:2]
    side_in, o_ref, side_out = refs[2:2 + n_side], refs[2 + n_side], refs[3 + n_side:]
    for src, dst in zip(side_in, side_out):
        dst[...] = src[...].astype(BF16)
    ang = pos_ref[...].astype(F32) * c_ref[0:1, :]
    cos, sin = jnp.cos(ang), jnp.sin(ang)
    lane = lax.broadcasted_iota(jnp.int32, ang.shape, 1)
    src = 0
    for k, (chunk, half) in enumerate(_ROPE_LAYOUTS):
        c_tab = jnp.ones_like(ang)
        lo_tab = jnp.zeros_like(ang)
        hi_tab = jnp.zeros_like(ang)
        for start in range(0, LANES, chunk):
            for part, dst in enumerate((start, start + half)):
                here = (lane >= dst) & (lane < dst + half)
                shift = (dst - src) % LANES
                c_here = pltpu.roll(cos, shift, 1) if shift else cos
                s_here = pltpu.roll(sin, shift, 1) if shift else sin
                c_tab = jnp.where(here, c_here, c_tab)
                if part == 0:
                    lo_tab = jnp.where(here, -s_here, lo_tab)
                else:
                    hi_tab = jnp.where(here, s_here, hi_tab)
        o_ref[3 * k], o_ref[3 * k + 1], o_ref[3 * k + 2] = c_tab, lo_tab, hi_tab
        src += half


def rope_tables(positions, side=(), tm=512):
    m = positions.shape[0]
    side_in, side_out, side_shape = _side_cast_specs(side, m // tm, lambda i: i)
    out = pl.pallas_call(
        functools.partial(_rope_table_kernel, n_side=len(side)),
        grid=(m // tm,),
        in_specs=[pl.BlockSpec((tm, 1), lambda i: (i, 0)),
                  pl.BlockSpec((8, LANES), lambda i: (0, 0))] + side_in,
        out_specs=[pl.BlockSpec((6, tm, LANES), lambda i: (0, i, 0))] + side_out,
        out_shape=[jax.ShapeDtypeStruct((6, m, LANES), F32)] + side_shape,
        compiler_params=_params("parallel"),
        name="rope_tables",
    )(positions, _rope_lane_consts(), *[w for w, _ in side])
    return out[0], out[1:]


ROPE_NONE, ROPE_DIFF, ROPE_DIL = 0, 1, 2
_ROPE_SHIFT = {ROPE_DIFF: _ROPE_LAYOUTS[0][1], ROPE_DIL: _ROPE_LAYOUTS[1][1]}
W_A = 3 * (W_SB + W_DIFF)
PROJ_TN = 512
_PROJ_KINDS = (ROPE_NONE,) * 3 + (ROPE_DIFF,) * 2 + (ROPE_NONE,) + (ROPE_DIL,) * 4 + (ROPE_NONE,) * 2


def _resident(shape):
    return pl.BlockSpec(shape, lambda *_: (0,) * len(shape), pipeline_mode=pl.Buffered(1))


def _proj_kernel(x_ref, g_ref, w_ref, tab_ref, oa_ref, od_ref, h_ref):
    h_ref[...] = _rms(x_ref[...], g_ref[...]).astype(BF16)
    tn = PROJ_TN
    for idx, kind in enumerate(_PROJ_KINDS):
        acc = _dot(h_ref[...], w_ref[:, idx * tn:(idx + 1) * tn])
        o_ref, col = (oa_ref, idx * tn) if idx * tn < W_A else (od_ref, idx * tn - W_A)
        if kind == ROPE_NONE:
            o_ref[:, col:col + tn] = acc.astype(o_ref.dtype)
            continue
        base = 3 * (kind - 1)
        shift = _ROPE_SHIFT[kind]
        c, lo, hi = tab_ref[base], tab_ref[base + 1], tab_ref[base + 2]
        for q in range(tn // LANES):
            xk = acc[:, q * LANES:(q + 1) * LANES]
            rot = xk * c + pltpu.roll(xk, LANES - shift, 1) * lo + pltpu.roll(xk, shift, 1) * hi
            o_ref[:, col + q * LANES:col + (q + 1) * LANES] = rot.astype(o_ref.dtype)


def in_proj(x, g, w, tabs, tm=256):
    m, d = x.shape
    n = w.shape[1]
    assert n == len(_PROJ_KINDS) * PROJ_TN
    return pl.pallas_call(
        _proj_kernel,
        grid=(m // tm,),
        in_specs=[pl.BlockSpec((tm, d), lambda i: (i, 0)),
                  _resident((1, d)),
                  _resident((d, n)),
                  pl.BlockSpec((6, tm, LANES), lambda i: (0, i, 0))],
        out_specs=[pl.BlockSpec((tm, W_A), lambda i: (i, 0)),
                   pl.BlockSpec((tm, n - W_A), lambda i: (i, 0))],
        out_shape=[jax.ShapeDtypeStruct((m, W_A), BF16),
                   jax.ShapeDtypeStruct((m, n - W_A), F32)],
        scratch_shapes=[pltpu.VMEM((tm, d), BF16)],
        compiler_params=_params("parallel"),
        name="in_proj",
    )(x, g.reshape(1, d), w, tabs)


def _mlp_kernel(*refs, final_norm):
    if final_norm:
        sb_ref, df_ref, dl_ref, wo_ref, x_ref, g_ref, wg_ref, wu_ref, wd_ref, gf_ref, o_ref, h_ref = refs
    else:
        sb_ref, df_ref, dl_ref, wo_ref, x_ref, g_ref, wg_ref, wu_ref, wd_ref, o_ref, h_ref = refs
    j = pl.program_id(1)

    @pl.when(j == 0)
    def _():
        mixed = (_dot(sb_ref[...], wo_ref[0:W_SB, :])
                 + _dot(df_ref[...], wo_ref[W_SB:W_SB + W_DIFF, :])
                 + _dot(dl_ref[...], wo_ref[W_SB + W_DIFF:, :]))
        xn = x_ref[...] + mixed
        o_ref[...] = xn
        h_ref[...] = _rms(xn, g_ref[...]).astype(BF16)

    h = h_ref[...]
    gate = _dot(h, wg_ref[...])
    up = _dot(h, wu_ref[...])
    act = (gate * jax.nn.sigmoid(gate)) * up
    o_ref[...] += _dot(act.astype(BF16), wd_ref[...])

    if final_norm:
        @pl.when(j == pl.num_programs(1) - 1)
        def _():
            o_ref[...] = _rms(o_ref[...], gf_ref[...])


def mlp_block(o_sb, o_df, o_dl, w_out, x, g, wg, wu, wd, final_g=None, tm=512, tf=512):
    m, d = x.shape
    f = wg.shape[1]
    final_norm = final_g is not None
    in_specs = [pl.BlockSpec((tm, W_SB), lambda i, j: (i, 0)),
                pl.BlockSpec((tm, W_DIFF), lambda i, j: (i, 0)),
                pl.BlockSpec((tm, W_DIL), lambda i, j: (i, 0)),
                _resident(w_out.shape),
                pl.BlockSpec((tm, d), lambda i, j: (i, 0)),
                _resident((1, d)),
                pl.BlockSpec((d, tf), lambda i, j: (0, j)),
                pl.BlockSpec((d, tf), lambda i, j: (0, j)),
                pl.BlockSpec((tf, d), lambda i, j: (j, 0))]
    args = [o_sb, o_df, o_dl, w_out, x, g.reshape(1, d), wg, wu, wd]
    if final_norm:
        in_specs.append(_resident((1, d)))
        args.append(final_g.reshape(1, d))
    return pl.pallas_call(
        functools.partial(_mlp_kernel, final_norm=final_norm),
        grid=(m // tm, f // tf),
        in_specs=in_specs,
        out_specs=pl.BlockSpec((tm, d), lambda i, j: (i, 0)),
        out_shape=jax.ShapeDtypeStruct((m, d), F32),
        scratch_shapes=[pltpu.VMEM((tm, d), BF16)],
        compiler_params=_params("parallel", "arbitrary"),
        name="mlp_block",
    )(*args)


def _causal_sweep(i, n_chain, step, keys_per_iter, wide=False, exhausted=None, ahead=False):
    assert n_chain % keys_per_iter == 0 and not (ahead and (wide or keys_per_iter != 1))
    base = n_chain * i
    if wide:
        _interleave([step(c, base, True, c + 1) for c in range(n_chain)])
    else:
        pairs = [(c, d) for d in range(n_chain - 1, -1, -1) for c in range(d, n_chain)]
        near = [step(c, base + d, c == d) for c, d in pairs if c - d <= 1]
        if ahead:
            near.append(step(0, jnp.maximum(base - 1, 0), False, i > 0))
        _interleave(near)
        far = [(c, d) for c, d in pairs if c - d > 1]

        def far_blocks():
            _interleave([step(c, base + d, False) for c, d in far])

        if far and exhausted is not None:
            pl.when(jnp.logical_not(exhausted(2)))(far_blocks)
        elif far:
            far_blocks()

    def body(t, _):
        first = base - 1 - t * keys_per_iter
        if wide:
            _interleave([step(c, first - (keys_per_iter - 1), False, keys_per_iter) for c in range(n_chain)])
        elif ahead:
            _interleave([step(0, jnp.maximum(first - 1, 0), False, first >= 1)]
                        + [step(c, first, False) for c in range(1, n_chain)])
        else:
            _interleave([step(c, first - u, False) for u in range(keys_per_iter) for c in range(n_chain)])
        return 0

    trips = (n_chain // keys_per_iter) * i
    if exhausted is None:
        lax.fori_loop(0, trips, body, 0)
        return

    def more(state):
        t, done = state
        return (t < trips) & jnp.logical_not(done)

    def advance(state):
        body(state[0], 0)
        return state[0] + 1, exhausted(0)

    lax.while_loop(more, advance, (0, exhausted(0)))


def _interleave(steps):
    steps = list(steps)
    while steps:
        alive = []
        for g in steps:
            try:
                next(g)
                alive.append(g)
            except StopIteration:
                pass
        steps = alive


def _from_key_matrix(blk):
    tri = np.tril(np.ones((blk, blk), np.float32))
    return jnp.asarray(np.concatenate([tri, tri], axis=0), BF16)


def _sb_kernel(q_ref, k_ref, v_ref, later_ref, g_ref, o_ref, qs_ref, acc_ref, carry_ref, *,
               blk, n_chain, keys_per_iter):
    i = pl.program_id(2)
    scale = HEAD_DIM ** -0.5
    qs_ref[...] = (q_ref[...].astype(F32) * scale).astype(BF16)
    acc_ref[...] = jnp.zeros_like(acc_ref)
    carry_ref[...] = jnp.zeros_like(carry_ref)
    row = lax.broadcasted_iota(jnp.int32, (blk, blk), 0)
    col = lax.broadcasted_iota(jnp.int32, (blk, blk), 1)
    strict = col < row

    def step(c, j, diag, live=None):
        rows = slice(c * blk, (c + 1) * blk)
        start = pl.multiple_of(j * blk, blk)
        k = k_ref[pl.ds(start, blk), :]
        v = v_ref[pl.ds(start, blk), :]
        z = _dot_nt(qs_ref[rows, :], k)
        yield
        sp = jnp.maximum(z, 0.0) + jnp.log(1.0 + jnp.exp2(jnp.abs(z) * -LOG2_E))
        spm = jnp.where(strict, sp, 0.0) if diag else sp
        if live is not None:
            spm = jnp.where(live, spm, 0.0)
        hi = spm.astype(BF16)
        lo = (spm - hi.astype(F32)).astype(BF16)
        from_key = _dot(jnp.concatenate([hi, lo], axis=1), later_ref[...])
        yield
        loga = z - (from_key + _tile(carry_ref[rows, :], blk // LANES))
        if diag:
            loga = jnp.where(strict, loga, NEG_INF)
        if live is not None:
            loga = jnp.where(live, loga, NEG_INF)
        pv = _dot(jnp.exp(loga).astype(BF16), v)
        carry_ref[rows, :] += from_key[:, 0:1]
        yield
        acc_ref[rows, :] += pv

    _causal_sweep(i, n_chain, step, keys_per_iter, ahead=True,
                  exhausted=lambda chain: jnp.min(carry_ref[chain * blk:, :]) >= SB_DEAD_MASS)
    o_ref[...] = _rms(acc_ref[...], g_ref[...]).astype(o_ref.dtype)


def sb_attention(pa, g, blk=256, n_chain=4, keys_per_iter=1):
    b, s, _ = pa.shape
    bq = blk * n_chain
    return pl.pallas_call(
        functools.partial(_sb_kernel, blk=blk, n_chain=n_chain, keys_per_iter=keys_per_iter),
        grid=(b, N_SB, s // bq),
        in_specs=[pl.BlockSpec((None, bq, HEAD_DIM), lambda b_, h, i: (b_, i, h)),
                  pl.BlockSpec((None, s, HEAD_DIM), lambda b_, h, i: (b_, 0, N_SB + h)),
                  pl.BlockSpec((None, s, HEAD_DIM), lambda b_, h, i: (b_, 0, 2 * N_SB + h)),
                  pl.BlockSpec((2 * blk, blk), lambda b_, h, i: (0, 0)),
                  pl.BlockSpec((1, HEAD_DIM), lambda b_, h, i: (0, 0))],
        out_specs=pl.BlockSpec((None, bq, HEAD_DIM), lambda b_, h, i: (b_, i, h)),
        out_shape=jax.ShapeDtypeStruct((b, s, W_SB), BF16),
        scratch_shapes=[pltpu.VMEM((bq, HEAD_DIM), BF16),
                        pltpu.VMEM((bq, HEAD_DIM), F32),
                        pltpu.VMEM((bq, LANES), F32)],
        compiler_params=_params("parallel", "parallel", "arbitrary"),
        name="sb_attention",
    )(pa, pa, pa, _from_key_matrix(blk), g.reshape(1, HEAD_DIM))


def _diff_kernel(*refs, blk, n_chain, keys_per_iter, lambda_init, n_side):
    q_ref, k_ref, v_ref, lam_ref, g_ref = refs[:5]
    side_in, o_ref = refs[5:5 + n_side], refs[5 + n_side]
    side_out = refs[6 + n_side:6 + 2 * n_side]
    qs_ref, vx_ref, m_ref, acc_ref = refs[6 + 2 * n_side:]
    for src, dst in zip(side_in, side_out):
        dst[...] = src[...].astype(BF16)
    i = pl.program_id(2)
    seq = v_ref.shape[0]

    @pl.when(i == 0)
    def _():
        vx_ref[:, 0:HEAD_DIM] = v_ref[...]
        vx_ref[:, HEAD_DIM:] = jnp.ones((seq, HEAD_DIM), BF16)

    scale = DIFF_QK ** -0.5
    q = q_ref[...].astype(F32) * scale
    lc = lax.broadcasted_iota(jnp.int32, (blk, HEAD_DIM), 1)
    for c in range(n_chain):
        qc = q[c * blk:(c + 1) * blk]
        qs_ref[(2 * c) * blk:(2 * c + 1) * blk, :] = jnp.where(lc < DIFF_QK, qc, 0.0).astype(BF16)
        qs_ref[(2 * c + 1) * blk:(2 * c + 2) * blk, :] = jnp.where(lc >= DIFF_QK, qc, 0.0).astype(BF16)
    row = lax.broadcasted_iota(jnp.int32, (blk, blk), 0)
    col = lax.broadcasted_iota(jnp.int32, (blk, blk), 1)
    causal = col <= row

    def step(c, j, diag, width=1):
        start = pl.multiple_of(j * blk, blk)
        k = k_ref[pl.ds(start, width * blk), :]
        vx = vx_ref[pl.ds(start, width * blk), :]
        rows = [slice((2 * c + e) * blk, (2 * c + e + 1) * blk) for e in range(2)]
        s = [_dot_nt(qs_ref[r, :], k) for r in rows]
        yield
        alpha, pv = [], []
        for e, r in enumerate(rows):
            se = s[e]
            if diag:
                own = jnp.where(causal, se[:, (width - 1) * blk:], NEG_INF)
                se = own if width == 1 else jnp.concatenate([se[:, :(width - 1) * blk], own], axis=1)
            top = jnp.max(se, axis=-1, keepdims=True)
            if diag:
                m_new = jnp.broadcast_to(top, (blk, LANES))
            else:
                m_prev = m_ref[r, :]
                m_new = jnp.maximum(m_prev, top)
                alpha.append(jnp.exp(m_prev - m_new))
            p = jnp.exp((se - _tile(m_new, width * blk // LANES)).astype(BF16))
            m_ref[r, :] = m_new
            pv.append(_dot(p, vx))
        yield
        for e, r in enumerate(rows):
            acc_ref[r, :] = pv[e] if diag else _tile(alpha[e], 2) * acc_ref[r, :] + pv[e]

    _causal_sweep(i, n_chain, step, keys_per_iter, wide=True)
    lq = lam_ref[...]
    lam = (jnp.exp(jnp.sum(lq[0:1] * lq[1:2], axis=-1, keepdims=True))
           - jnp.exp(jnp.sum(lq[2:3] * lq[3:4], axis=-1, keepdims=True)) + lambda_init)
    g = g_ref[...]
    for c in range(n_chain):
        o = []
        for e in range(2):
            acc = acc_ref[(2 * c + e) * blk:(2 * c + e + 1) * blk, :]
            o.append(acc[:, :HEAD_DIM] / acc[:, HEAD_DIM:])
        out = o[0] - lam * o[1]
        o_ref[c * blk:(c + 1) * blk, :] = (_rms(out, g) * (1.0 - lambda_init)).astype(o_ref.dtype)


def diff_attention(pa, lam_vecs, g, lambda_init, side=(), blk=256, n_chain=4, keys_per_iter=4):
    b, s, _ = pa.shape
    c0 = 3 * N_SB
    bq = blk * n_chain
    n_i = s // bq
    side_in, side_out, side_shape = _side_cast_specs(
        side, b * N_DIFF * n_i, lambda b_, h, i: (b_ * N_DIFF + h) * n_i + i)
    out = pl.pallas_call(
        functools.partial(_diff_kernel, blk=blk, n_chain=n_chain, keys_per_iter=keys_per_iter,
                          lambda_init=lambda_init, n_side=len(side)),
        grid=(b, N_DIFF, n_i),
        in_specs=[pl.BlockSpec((None, bq, HEAD_DIM), lambda b_, h, i: (b_, i, c0 + h)),
                  pl.BlockSpec((None, s, HEAD_DIM), lambda b_, h, i: (b_, 0, c0 + N_DIFF + h)),
                  pl.BlockSpec((None, s, HEAD_DIM), lambda b_, h, i: (b_, 0, c0 + 2 * N_DIFF + h)),
                  pl.BlockSpec((4, DIFF_QK), lambda b_, h, i: (0, 0)),
                  pl.BlockSpec((1, HEAD_DIM), lambda b_, h, i: (0, 0))] + side_in,
        out_specs=[pl.BlockSpec((None, bq, HEAD_DIM), lambda b_, h, i: (b_, i, h))] + side_out,
        out_shape=[jax.ShapeDtypeStruct((b, s, W_DIFF), BF16)] + side_shape,
        scratch_shapes=[pltpu.VMEM((2 * bq, HEAD_DIM), BF16),
                        pltpu.VMEM((s, 2 * HEAD_DIM), BF16),
                        pltpu.VMEM((2 * bq, LANES), F32),
                        pltpu.VMEM((2 * bq, 2 * HEAD_DIM), F32)],
        compiler_params=_params("parallel", "parallel", "arbitrary"),
        name="diff_attention",
    )(pa, pa, pa, lam_vecs, g.reshape(1, HEAD_DIM), *[w for w, _ in side])
    return out[0], out[1:]


def _dil_kernel(q_ref, k_ref, v_ref, g_ref, o_ref, qd_ref, kd_ref, vd_ref, q4_ref, k4_ref, v4_ref,
                bias_ref, op_ref, lse_ref, *, seq, group, unroll):
    blk = DIL_SPAN
    scale = HEAD_DIM ** -0.5
    n_blocks = seq // blk
    for buf in range(2):
        kd_ref[buf, 0:blk, :] = jnp.zeros((blk, HEAD_DIM), BF16)
        vd_ref[buf, 0:blk, 0:HEAD_DIM] = jnp.zeros((blk, HEAD_DIM), BF16)
        vd_ref[buf, :, HEAD_DIM:] = jnp.ones((seq + blk, HEAD_DIM), BF16)
    ii = lax.broadcasted_iota(jnp.int32, (blk, 2 * blk), 0)
    jj = lax.broadcasted_iota(jnp.int32, (blk, 2 * blk), 1)
    dist = blk + ii - jj
    in_window = (dist >= 0) & (dist <= DIL_SPAN)
    bias_ref[0] = jnp.where(in_window & (jj >= blk), 0.0, NEG_INF)
    bias_ref[1] = jnp.where(in_window, 0.0, NEG_INF)
    quarter = seq // 4

    gathers, attends = [], []
    for p, r in enumerate(DILATIONS):
        per_seq = n_blocks // r
        shift = per_seq.bit_length() - 1
        buf = p % 2

        def block_of(n, per_seq=per_seq, shift=shift):
            return lax.shift_right_logical(n, shift), n & (per_seq - 1)

        def natural_rows(n, r=r, block_of=block_of):
            c, i = block_of(n)
            src = c + r * blk * i
            if r == 1:
                return pl.ds(pl.multiple_of(src, blk), blk)
            return pl.ds(src, blk, stride=r)

        def gather(n, r=r, buf=buf, block_of=block_of, natural_rows=natural_rows):
            here = pl.ds(pl.multiple_of(blk * n, blk), blk)
            if r == 16:
                c, i = block_of(n)
                rows = pl.ds((c & 3) * quarter + lax.shift_right_logical(c, 2) + 4 * blk * i, blk, stride=4)
                q, k, v = q4_ref[rows, :], k4_ref[rows, :], v4_ref[rows, :]
            else:
                rows = natural_rows(n)
                q, k, v = q_ref[rows, :], k_ref[rows, :], v_ref[rows, :]
            if r == 4:
                q4_ref[here, :], k4_ref[here, :], v4_ref[here, :] = q, k, v
            dst = pl.ds(pl.multiple_of(blk * (n + 1), blk), blk)
            kd_ref[buf, dst, :] = k.astype(BF16)
            vd_ref[buf, dst, 0:HEAD_DIM] = v.astype(BF16)
            qd_ref[buf, here, :] = (q * scale).astype(BF16)

        def attend(n, p=p, buf=buf, block_of=block_of, natural_rows=natural_rows):
            _, i = block_of(n)
            rows = natural_rows(n)
            q = qd_ref[buf, pl.ds(pl.multiple_of(blk * n, blk), blk), :]
            win = pl.ds(pl.multiple_of(blk * n, blk), 2 * blk)
            s = _dot_nt(q, kd_ref[buf, win, :])
            yield
            s = s + bias_ref[jnp.where(i == 0, 0, 1)]
            m = jnp.max(s, axis=-1, keepdims=True)
            pv = _dot(jnp.exp((s - m).astype(BF16)), vd_ref[buf, win, :])
            yield
            l = pv[:, HEAD_DIM:]
            op_ref[p, rows, :] = pv[:, :HEAD_DIM] / l
            lse_ref[p, rows, :] = m + jnp.log(l)

        gathers.append(gather)
        attends.append(attend)

    def one_phase(fn, n):
        fn(n)
        yield

    def first_gather(n, _):
        gathers[0](n)
        return 0

    lax.fori_loop(0, n_blocks, first_gather, 0, unroll=unroll)
    for p in range(len(DILATIONS)):
        prefetch = gathers[p + 1] if p + 1 < len(DILATIONS) else None

        def attend_group(t, _, p=p, prefetch=prefetch):
            steps = []
            for u in range(group):
                steps.append(attends[p](t * group + u))
                if prefetch is not None:
                    steps.append(one_phase(prefetch, t * group + u))
            _interleave(steps)
            return 0

        lax.fori_loop(0, n_blocks // group, attend_group, 0)

    g = g_ref[...]

    def merge(n, _):
        rows = pl.ds(pl.multiple_of(n * blk, blk), blk)
        lses = [lse_ref[p, rows, :] for p in range(len(DILATIONS))]
        top = functools.reduce(jnp.maximum, lses)
        ws = [jnp.exp(x - top) for x in lses]
        den = functools.reduce(lambda a, b_: a + b_, ws)
        num = functools.reduce(lambda a, b_: a + b_,
                               [op_ref[p, rows, :] * ws[p] for p in range(len(DILATIONS))])
        o_ref[rows, :] = _rms(num * (1.0 / den), g).astype(o_ref.dtype)
        return 0

    lax.fori_loop(0, n_blocks, merge, 0, unroll=2 * unroll)


def dil_attention(pd, g, group=32, unroll=4):
    b, s, _ = pd.shape
    n_pat = len(DILATIONS)
    assert DILATIONS == (1, 4, 16) and (s // DIL_SPAN) % group == 0
    return pl.pallas_call(
        functools.partial(_dil_kernel, seq=s, group=group, unroll=unroll),
        grid=(b, N_DIL),
        in_specs=[pl.BlockSpec((None, s, HEAD_DIM), lambda b_, h: (b_, 0, h)),
                  pl.BlockSpec((None, s, HEAD_DIM), lambda b_, h: (b_, 0, N_DIL + h)),
                  pl.BlockSpec((None, s, HEAD_DIM), lambda b_, h: (b_, 0, 2 * N_DIL + h)),
                  pl.BlockSpec((1, HEAD_DIM), lambda b_, h: (0, 0))],
        out_specs=pl.BlockSpec((None, s, HEAD_DIM), lambda b_, h: (b_, 0, h)),
        out_shape=jax.ShapeDtypeStruct((b, s, W_DIL), BF16),
        scratch_shapes=[pltpu.VMEM((2, s, HEAD_DIM), BF16),
                        pltpu.VMEM((2, s + DIL_SPAN, HEAD_DIM), BF16),
                        pltpu.VMEM((2, s + DIL_SPAN, 2 * HEAD_DIM), BF16),
                        pltpu.VMEM((s, HEAD_DIM), F32),
                        pltpu.VMEM((s, HEAD_DIM), F32),
                        pltpu.VMEM((s, HEAD_DIM), F32),
                        pltpu.VMEM((2, DIL_SPAN, 2 * DIL_SPAN), F32),
                        pltpu.VMEM((n_pat, s, HEAD_DIM), F32),
                        pltpu.VMEM((n_pat, s, HEAD_DIM), F32)],
        compiler_params=_params("parallel", "arbitrary"),
        name="dil_attention",
    )(pd, pd, pd, g.reshape(1, HEAD_DIM))


def kernel(x, positions, norm_mix_g, w_in, lambda_q1, lambda_k1, lambda_q2, lambda_k2, g_sb_out, g_diff_out, g_dil_out, w_out, norm_ffn_g, w_gate, w_up, w_down, norm_final_g):
    batch, seq, d = x.shape
    m = batch * seq
    depth = w_in.shape[0]
    xf = x.reshape(m, d)
    tabs, (w_in_bf,) = rope_tables(positions.reshape(m, 1), [(w_in, 0)])
    for layer in range(depth):
        pa, pd = in_proj(xf, norm_mix_g[layer], w_in_bf, tabs)
        pa = pa.reshape(batch, seq, W_A)
        pd = pd.reshape(batch, seq, 3 * W_DIL)
        lambda_init = 0.8 - 0.6 * math.exp(-0.3 * layer)
        lam_vecs = jnp.stack([lambda_q1[layer], lambda_k1[layer],
                              lambda_q2[layer], lambda_k2[layer]]).astype(F32)
        o_sb = sb_attention(pa, g_sb_out[layer]).reshape(m, W_SB)
        side = [(w, layer) for w in (w_out, w_gate, w_up, w_down)]
        side += [(w_in, layer + 1)] if layer + 1 < depth else []
        o_df, converted = diff_attention(pa, lam_vecs, g_diff_out[layer], lambda_init, side)
        o_df = o_df.reshape(m, W_DIFF)
        w_out_bf, w_gate_bf, w_up_bf, w_down_bf = converted[:4]
        w_in_bf = converted[4] if layer + 1 < depth else None
        o_dl = dil_attention(pd, g_dil_out[layer]).reshape(m, W_DIL)
        xf = mlp_block(o_sb, o_df, o_dl, w_out_bf, xf, norm_ffn_g[layer], w_gate_bf, w_up_bf, w_down_bf,
                       final_g=norm_final_g if layer == depth - 1 else None)
    return xf.reshape(batch, seq, d)
```

```python
import functools
import math

import jax
import jax.numpy as jnp
import numpy as np
from jax import lax
from jax.experimental import pallas as pl
from jax.experimental.pallas import tpu as pltpu

F32 = jnp.float32
BF16 = jnp.bfloat16

HEAD_DIM = 128
N_SB = 4
N_DIFF = 4
N_DIL = 8
W_SB = N_SB * HEAD_DIM
W_DIFF = N_DIFF * HEAD_DIM
W_DIL = N_DIL * HEAD_DIM
DIFF_QK = HEAD_DIM // 2
ROPE_THETA = 500000.0
ROPE_FRACTION = 0.25
DILATIONS = (1, 4, 16)
DIL_SPAN = 128
NORM_EPS = 1e-6
LANES = 128
VMEM_LIMIT = 56 * 1024 * 1024

NEG_INF = float("-inf")
LOG2_E = math.log2(math.e)
SB_DEAD_MASS = 106.0


def _params(*sem):
    return pltpu.CompilerParams(dimension_semantics=sem, vmem_limit_bytes=VMEM_LIMIT)


def _dot(a, b):
    return jnp.dot(a, b, preferred_element_type=F32)


def _dot_nt(a, b):
    return lax.dot_general(a, b, (((1,), (1,)), ((), ())), preferred_element_type=F32)


def _tile(x, n):
    return x if n == 1 else jnp.concatenate([x] * n, axis=1)


def _rms(x, g):
    return x * lax.rsqrt(jnp.mean(x * x, axis=-1, keepdims=True) + NORM_EPS) * g


def _side_cast_specs(side, n_steps, step_index):
    in_specs, out_specs, out_shape = [], [], []
    for w, layer in side:
        _, r, c = w.shape
        rows = r // n_steps
        assert rows * n_steps == r and rows % 16 == 0
        in_specs.append(pl.BlockSpec((None, rows, c), lambda *g, layer=layer: (layer, step_index(*g), 0)))
        out_specs.append(pl.BlockSpec((rows, c), lambda *g: (step_index(*g), 0)))
        out_shape.append(jax.ShapeDtypeStruct((r, c), BF16))
    return in_specs, out_specs, out_shape


_ROPE_LAYOUTS = ((DIFF_QK, int(DIFF_QK * ROPE_FRACTION) // 2), (HEAD_DIM, int(HEAD_DIM * ROPE_FRACTION) // 2))


def _rope_lane_consts():
    freqs = [ROPE_THETA ** (-jnp.arange(half, dtype=F32) / half) for _, half in _ROPE_LAYOUTS]
    used = sum(half for _, half in _ROPE_LAYOUTS)
    row = jnp.concatenate(freqs + [jnp.zeros((LANES - used,), F32)])
    return jnp.concatenate([row[None, :], jnp.zeros((7, LANES), F32)])


def _rope_table_kernel(*refs, n_side):
    pos_ref, c_ref = refs[:2]
    side_in, o_ref, side_out = refs[2:2 + n_side], refs[2 + n_side], refs[3 + n_side:]
    for src, dst in zip(side_in, side_out):
        dst[...] = src[...].astype(BF16)
    ang = pos_ref[...].astype(F32) * c_ref[0:1, :]
    cos, sin = jnp.cos(ang), jnp.sin(ang)
    lane = lax.broadcasted_iota(jnp.int32, ang.shape, 1)
    src = 0
    for k, (chunk, half) in enumerate(_ROPE_LAYOUTS):
        c_tab = jnp.ones_like(ang)
        lo_tab = jnp.zeros_like(ang)
        hi_tab = jnp.zeros_like(ang)
        for start in range(0, LANES, chunk):
            for part, dst in enumerate((start, start + half)):
                here = (lane >= dst) & (lane < dst + half)
                shift = (dst - src) % LANES
                c_here = pltpu.roll(cos, shift, 1) if shift else cos
                s_here = pltpu.roll(sin, shift, 1) if shift else sin
                c_tab = jnp.where(here, c_here, c_tab)
                if part == 0:
                    lo_tab = jnp.where(here, -s_here, lo_tab)
                else:
                    hi_tab = jnp.where(here, s_here, hi_tab)
        o_ref[3 * k], o_ref[3 * k + 1], o_ref[3 * k + 2] = c_tab, lo_tab, hi_tab
        src += half


def rope_tables(positions, side=(), tm=512):
    m = positions.shape[0]
    side_in, side_out, side_shape = _side_cast_specs(side, m // tm, lambda i: i)
    out = pl.pallas_call(
        functools.partial(_rope_table_kernel, n_side=len(side)),
        grid=(m // tm,),
        in_specs=[pl.BlockSpec((tm, 1), lambda i: (i, 0)),
                  pl.BlockSpec((8, LANES), lambda i: (0, 0))] + side_in,
        out_specs=[pl.BlockSpec((6, tm, LANES), lambda i: (0, i, 0))] + side_out,
        out_shape=[jax.ShapeDtypeStruct((6, m, LANES), F32)] + side_shape,
        compiler_params=_params("parallel"),
        name="rope_tables",
    )(positions, _rope_lane_consts(), *[w for w, _ in side])
    return out[0], out[1:]


ROPE_NONE, ROPE_DIFF, ROPE_DIL = 0, 1, 2
_ROPE_SHIFT = {ROPE_DIFF: _ROPE_LAYOUTS[0][1], ROPE_DIL: _ROPE_LAYOUTS[1][1]}
W_A = 3 * (W_SB + W_DIFF)
PROJ_TN = 512
_PROJ_KINDS = (ROPE_NONE,) * 3 + (ROPE_DIFF,) * 2 + (ROPE_NONE,) + (ROPE_DIL,) * 4 + (ROPE_NONE,) * 2


def _resident(shape):
    return pl.BlockSpec(shape, lambda *_: (0,) * len(shape), pipeline_mode=pl.Buffered(1))


def _proj_kernel(x_ref, g_ref, w_ref, tab_ref, oa_ref, od_ref, h_ref):
    h_ref[...] = _rms(x_ref[...], g_ref[...]).astype(BF16)
    tn = PROJ_TN
    for idx, kind in enumerate(_PROJ_KINDS):
        acc = _dot(h_ref[...], w_ref[:, idx * tn:(idx + 1) * tn])
        o_ref, col = (oa_ref, idx * tn) if idx * tn < W_A else (od_ref, idx * tn - W_A)
        if kind == ROPE_NONE:
            o_ref[:, col:col + tn] = acc.astype(o_ref.dtype)
            continue
        base = 3 * (kind - 1)
        shift = _ROPE_SHIFT[kind]
        c, lo, hi = tab_ref[base], tab_ref[base + 1], tab_ref[base + 2]
        for q in range(tn // LANES):
            xk = acc[:, q * LANES:(q + 1) * LANES]
            rot = xk * c + pltpu.roll(xk, LANES - shift, 1) * lo + pltpu.roll(xk, shift, 1) * hi
            o_ref[:, col + q * LANES:col + (q + 1) * LANES] = rot.astype(o_ref.dtype)


def in_proj(x, g, w, tabs, tm=256):
    m, d = x.shape
    n = w.shape[1]
    assert n == len(_PROJ_KINDS) * PROJ_TN
    return pl.pallas_call(
        _proj_kernel,
        grid=(m // tm,),
        in_specs=[pl.BlockSpec((tm, d), lambda i: (i, 0)),
                  _resident((1, d)),
                  _resident((d, n)),
                  pl.BlockSpec((6, tm, LANES), lambda i: (0, i, 0))],
        out_specs=[pl.BlockSpec((tm, W_A), lambda i: (i, 0)),
                   pl.BlockSpec((tm, n - W_A), lambda i: (i, 0))],
        out_shape=[jax.ShapeDtypeStruct((m, W_A), BF16),
                   jax.ShapeDtypeStruct((m, n - W_A), F32)],
        scratch_shapes=[pltpu.VMEM((tm, d), BF16)],
        compiler_params=_params("parallel"),
        name="in_proj",
    )(x, g.reshape(1, d), w, tabs)


def _mlp_kernel(*refs, final_norm):
    if final_norm:
        (sb_hbm, df_hbm, dl_hbm, wo_ref, x_hbm, g_ref, wg_ref, wu_ref, wd_ref, gf_ref, o_ref,
         h_ref, sb_ref, df_ref, dl_ref, x_ref, sem) = refs
    else:
        (sb_hbm, df_hbm, dl_hbm, wo_ref, x_hbm, g_ref, wg_ref, wu_ref, wd_ref, o_ref,
         h_ref, sb_ref, df_ref, dl_ref, x_ref, sem) = refs
    i, j = pl.program_id(0), pl.program_id(1)
    tm = o_ref.shape[0]

    def row_copies(tile):
        slot = tile % 2
        rows = pl.ds(pl.multiple_of(tile * tm, tm), tm)
        pairs = ((sb_hbm, sb_ref), (df_hbm, df_ref), (dl_hbm, dl_ref), (x_hbm, x_ref))
        return [pltpu.make_async_copy(src.at[rows, :], dst.at[slot], sem.at[slot, n])
                for n, (src, dst) in enumerate(pairs)]

    @pl.when((i == 0) & (j == 0))
    def _():
        for copy in row_copies(0):
            copy.start()

    @pl.when((j == 1) & (i + 1 < pl.num_programs(0)))
    def _():
        for copy in row_copies(i + 1):
            copy.start()

    @pl.when(j == 0)
    def _():
        for copy in row_copies(i):
            copy.wait()
        slot = i % 2
        mixed = (_dot(sb_ref[slot], wo_ref[0:W_SB, :])
                 + _dot(df_ref[slot], wo_ref[W_SB:W_SB + W_DIFF, :])
                 + _dot(dl_ref[slot], wo_ref[W_SB + W_DIFF:, :]))
        xn = x_ref[slot] + mixed
        o_ref[...] = xn
        h_ref[...] = _rms(xn, g_ref[...]).astype(BF16)

    h = h_ref[...]
    gate = _dot(h, wg_ref[...])
    up = _dot(h, wu_ref[...])
    act = (gate * jax.nn.sigmoid(gate)) * up
    o_ref[...] += _dot(act.astype(BF16), wd_ref[...])

    if final_norm:
        @pl.when(j == pl.num_programs(1) - 1)
        def _():
            o_ref[...] = _rms(o_ref[...], gf_ref[...])


def mlp_block(o_sb, o_df, o_dl, w_out, x, g, wg, wu, wd, final_g=None, tm=512, tf=512):
    m, d = x.shape
    f = wg.shape[1]
    final_norm = final_g is not None
    assert f // tf >= 2 and m % tm == 0
    in_hbm = pl.BlockSpec(memory_space=pl.ANY)
    in_specs = [in_hbm, in_hbm, in_hbm,
                _resident(w_out.shape),
                in_hbm,
                _resident((1, d)),
                pl.BlockSpec((d, tf), lambda i, j: (0, j)),
                pl.BlockSpec((d, tf), lambda i, j: (0, j)),
                pl.BlockSpec((tf, d), lambda i, j: (j, 0))]
    args = [o_sb, o_df, o_dl, w_out, x, g.reshape(1, d), wg, wu, wd]
    if final_norm:
        in_specs.append(_resident((1, d)))
        args.append(final_g.reshape(1, d))
    return pl.pallas_call(
        functools.partial(_mlp_kernel, final_norm=final_norm),
        grid=(m // tm, f // tf),
        in_specs=in_specs,
        out_specs=pl.BlockSpec((tm, d), lambda i, j: (i, 0)),
        out_shape=jax.ShapeDtypeStruct((m, d), F32),
        scratch_shapes=[pltpu.VMEM((tm, d), BF16),
                        pltpu.VMEM((2, tm, W_SB), BF16),
                        pltpu.VMEM((2, tm, W_DIFF), BF16),
                        pltpu.VMEM((2, tm, W_DIL), BF16),
                        pltpu.VMEM((2, tm, d), F32),
                        pltpu.SemaphoreType.DMA((2, 4))],
        compiler_params=_params("arbitrary", "arbitrary"),
        name="mlp_block",
    )(*args)


def _causal_sweep(i, n_chain, step, keys_per_iter, wide=False, exhausted=None, ahead=False):
    assert n_chain % keys_per_iter == 0 and not (ahead and (wide or keys_per_iter != 1))
    base = n_chain * i
    if wide:
        _interleave([step(c, base, True, c + 1) for c in range(n_chain)])
    else:
        pairs = [(c, d) for d in range(n_chain - 1, -1, -1) for c in range(d, n_chain)]
        near = [step(c, base + d, c == d) for c, d in pairs if c - d <= 1]
        if ahead:
            near.append(step(0, jnp.maximum(base - 1, 0), False, i > 0))
        _interleave(near)
        far = [(c, d) for c, d in pairs if c - d > 1]

        def far_blocks():
            _interleave([step(c, base + d, False) for c, d in far])

        if far and exhausted is not None:
            pl.when(jnp.logical_not(exhausted(2)))(far_blocks)
        elif far:
            far_blocks()

    def body(t, _):
        first = base - 1 - t * keys_per_iter
        if wide:
            _interleave([step(c, first - (keys_per_iter - 1), False, keys_per_iter) for c in range(n_chain)])
        elif ahead:
            _interleave([step(0, jnp.maximum(first - 1, 0), False, first >= 1)]
                        + [step(c, first, False) for c in range(1, n_chain)])
        else:
            _interleave([step(c, first - u, False) for u in range(keys_per_iter) for c in range(n_chain)])
        return 0

    trips = (n_chain // keys_per_iter) * i
    if exhausted is None:
        lax.fori_loop(0, trips, body, 0)
        return

    def more(state):
        t, done = state
        return (t < trips) & jnp.logical_not(done)

    def advance(state):
        body(state[0], 0)
        return state[0] + 1, exhausted(0)

    lax.while_loop(more, advance, (0, exhausted(0)))


def _interleave(steps):
    steps = list(steps)
    while steps:
        alive = []
        for g in steps:
            try:
                next(g)
                alive.append(g)
            except StopIteration:
                pass
        steps = alive


def _from_key_matrix(blk):
    tri = np.tril(np.ones((blk, blk), np.float32))
    return jnp.asarray(np.concatenate([tri, tri], axis=0), BF16)


def _sb_kernel(q_ref, k_ref, v_ref, later_ref, g_ref, o_ref, qs_ref, acc_ref, carry_ref, *,
               blk, n_chain, keys_per_iter):
    i = pl.program_id(2)
    scale = HEAD_DIM ** -0.5
    qs_ref[...] = (q_ref[...].astype(F32) * scale).astype(BF16)
    acc_ref[...] = jnp.zeros_like(acc_ref)
    carry_ref[...] = jnp.zeros_like(carry_ref)
    row = lax.broadcasted_iota(jnp.int32, (blk, blk), 0)
    col = lax.broadcasted_iota(jnp.int32, (blk, blk), 1)
    strict = col < row

    def step(c, j, diag, live=None):
        rows = slice(c * blk, (c + 1) * blk)
        start = pl.multiple_of(j * blk, blk)
        k = k_ref[pl.ds(start, blk), :]
        v = v_ref[pl.ds(start, blk), :]
        z = _dot_nt(qs_ref[rows, :], k)
        yield
        sp = jnp.maximum(z, 0.0) + jnp.log(1.0 + jnp.exp2(jnp.abs(z) * -LOG2_E))
        spm = jnp.where(strict, sp, 0.0) if diag else sp
        if live is not None:
            spm = jnp.where(live, spm, 0.0)
        hi = spm.astype(BF16)
        lo = (spm - hi.astype(F32)).astype(BF16)
        from_key = _dot(jnp.concatenate([hi, lo], axis=1), later_ref[...])
        yield
        loga = z - (from_key + _tile(carry_ref[rows, :], blk // LANES))
        if diag:
            loga = jnp.where(strict, loga, NEG_INF)
        if live is not None:
            loga = jnp.where(live, loga, NEG_INF)
        pv = _dot(jnp.exp(loga).astype(BF16), v)
        carry_ref[rows, :] += from_key[:, 0:1]
        yield
        acc_ref[rows, :] += pv

    _causal_sweep(i, n_chain, step, keys_per_iter, ahead=True,
                  exhausted=lambda chain: jnp.min(carry_ref[chain * blk:, :]) >= SB_DEAD_MASS)
    o_ref[...] = _rms(acc_ref[...], g_ref[...]).astype(o_ref.dtype)


def sb_attention(pa, g, blk=256, n_chain=4, keys_per_iter=1):
    b, s, _ = pa.shape
    bq = blk * n_chain
    return pl.pallas_call(
        functools.partial(_sb_kernel, blk=blk, n_chain=n_chain, keys_per_iter=keys_per_iter),
        grid=(b, N_SB, s // bq),
        in_specs=[pl.BlockSpec((None, bq, HEAD_DIM), lambda b_, h, i: (b_, i, h)),
                  pl.BlockSpec((None, s, HEAD_DIM), lambda b_, h, i: (b_, 0, N_SB + h)),
                  pl.BlockSpec((None, s, HEAD_DIM), lambda b_, h, i: (b_, 0, 2 * N_SB + h)),
                  pl.BlockSpec((2 * blk, blk), lambda b_, h, i: (0, 0)),
                  pl.BlockSpec((1, HEAD_DIM), lambda b_, h, i: (0, 0))],
        out_specs=pl.BlockSpec((None, bq, HEAD_DIM), lambda b_, h, i: (b_, i, h)),
        out_shape=jax.ShapeDtypeStruct((b, s, W_SB), BF16),
        scratch_shapes=[pltpu.VMEM((bq, HEAD_DIM), BF16),
                        pltpu.VMEM((bq, HEAD_DIM), F32),
                        pltpu.VMEM((bq, LANES), F32)],
        compiler_params=_params("parallel", "parallel", "arbitrary"),
        name="sb_attention",
    )(pa, pa, pa, _from_key_matrix(blk), g.reshape(1, HEAD_DIM))


def _diff_kernel(*refs, blk, n_chain, keys_per_iter, lambda_init, n_side):
    q_ref, k_ref, v_ref, lam_ref, g_ref = refs[:5]
    side_in, o_ref = refs[5:5 + n_side], refs[5 + n_side]
    side_out = refs[6 + n_side:6 + 2 * n_side]
    qs_ref, vx_ref, m_ref, acc_ref = refs[6 + 2 * n_side:]
    for src, dst in zip(side_in, side_out):
        dst[...] = src[...].astype(BF16)
    i = pl.program_id(2)
    seq = v_ref.shape[0]

    @pl.when(i == 0)
    def _():
        vx_ref[:, 0:HEAD_DIM] = v_ref[...]
        vx_ref[:, HEAD_DIM:] = jnp.ones((seq, HEAD_DIM), BF16)

    scale = DIFF_QK ** -0.5
    q = q_ref[...].astype(F32) * scale
    lc = lax.broadcasted_iota(jnp.int32, (blk, HEAD_DIM), 1)
    for c in range(n_chain):
        qc = q[c * blk:(c + 1) * blk]
        qs_ref[(2 * c) * blk:(2 * c + 1) * blk, :] = jnp.where(lc < DIFF_QK, qc, 0.0).astype(BF16)
        qs_ref[(2 * c + 1) * blk:(2 * c + 2) * blk, :] = jnp.where(lc >= DIFF_QK, qc, 0.0).astype(BF16)
    row = lax.broadcasted_iota(jnp.int32, (blk, blk), 0)
    col = lax.broadcasted_iota(jnp.int32, (blk, blk), 1)
    causal = col <= row

    def step(c, j, diag, width=1):
        start = pl.multiple_of(j * blk, blk)
        k = k_ref[pl.ds(start, width * blk), :]
        vx = vx_ref[pl.ds(start, width * blk), :]
        rows = [slice((2 * c + e) * blk, (2 * c + e + 1) * blk) for e in range(2)]
        s = [_dot_nt(qs_ref[r, :], k) for r in rows]
        yield
        alpha, pv = [], []
        for e, r in enumerate(rows):
            se = s[e]
            if diag:
                own = jnp.where(causal, se[:, (width - 1) * blk:], NEG_INF)
                se = own if width == 1 else jnp.concatenate([se[:, :(width - 1) * blk], own], axis=1)
            top = jnp.max(se, axis=-1, keepdims=True)
            if diag:
                m_new = jnp.broadcast_to(top, (blk, LANES))
            else:
                m_prev = m_ref[r, :]
                m_new = jnp.maximum(m_prev, top)
                alpha.append(jnp.exp(m_prev - m_new))
            p = jnp.exp((se - _tile(m_new, width * blk // LANES)).astype(BF16))
            m_ref[r, :] = m_new
            pv.append(_dot(p, vx))
        yield
        for e, r in enumerate(rows):
            acc_ref[r, :] = pv[e] if diag else _tile(alpha[e], 2) * acc_ref[r, :] + pv[e]

    _causal_sweep(i, n_chain, step, keys_per_iter, wide=True)
    lq = lam_ref[...]
    lam = (jnp.exp(jnp.sum(lq[0:1] * lq[1:2], axis=-1, keepdims=True))
           - jnp.exp(jnp.sum(lq[2:3] * lq[3:4], axis=-1, keepdims=True)) + lambda_init)
    g = g_ref[...]
    for c in range(n_chain):
        o = []
        for e in range(2):
            acc = acc_ref[(2 * c + e) * blk:(2 * c + e + 1) * blk, :]
            o.append(acc[:, :HEAD_DIM] / acc[:, HEAD_DIM:])
        out = o[0] - lam * o[1]
        o_ref[c * blk:(c + 1) * blk, :] = (_rms(out, g) * (1.0 - lambda_init)).astype(o_ref.dtype)


def diff_attention(pa, lam_vecs, g, lambda_init, side=(), blk=256, n_chain=4, keys_per_iter=4):
    b, s, _ = pa.shape
    c0 = 3 * N_SB
    bq = blk * n_chain
    n_i = s // bq
    side_in, side_out, side_shape = _side_cast_specs(
        side, b * N_DIFF * n_i, lambda b_, h, i: (b_ * N_DIFF + h) * n_i + i)
    out = pl.pallas_call(
        functools.partial(_diff_kernel, blk=blk, n_chain=n_chain, keys_per_iter=keys_per_iter,
                          lambda_init=lambda_init, n_side=len(side)),
        grid=(b, N_DIFF, n_i),
        in_specs=[pl.BlockSpec((None, bq, HEAD_DIM), lambda b_, h, i: (b_, i, c0 + h)),
                  pl.BlockSpec((None, s, HEAD_DIM), lambda b_, h, i: (b_, 0, c0 + N_DIFF + h)),
                  pl.BlockSpec((None, s, HEAD_DIM), lambda b_, h, i: (b_, 0, c0 + 2 * N_DIFF + h)),
                  pl.BlockSpec((4, DIFF_QK), lambda b_, h, i: (0, 0)),
                  pl.BlockSpec((1, HEAD_DIM), lambda b_, h, i: (0, 0))] + side_in,
        out_specs=[pl.BlockSpec((None, bq, HEAD_DIM), lambda b_, h, i: (b_, i, h))] + side_out,
        out_shape=[jax.ShapeDtypeStruct((b, s, W_DIFF), BF16)] + side_shape,
        scratch_shapes=[pltpu.VMEM((2 * bq, HEAD_DIM), BF16),
                        pltpu.VMEM((s, 2 * HEAD_DIM), BF16),
                        pltpu.VMEM((2 * bq, LANES), F32),
                        pltpu.VMEM((2 * bq, 2 * HEAD_DIM), F32)],
        compiler_params=_params("parallel", "parallel", "arbitrary"),
        name="diff_attention",
    )(pa, pa, pa, lam_vecs, g.reshape(1, HEAD_DIM), *[w for w, _ in side])
    return out[0], out[1:]


def _dil_kernel(q_ref, k_ref, v_ref, g_ref, o_ref, qd_ref, kd_ref, vd_ref, q4_ref, k4_ref, v4_ref,
                bias_ref, op_ref, lse_ref, *, seq, group, unroll):
    blk = DIL_SPAN
    scale = HEAD_DIM ** -0.5
    n_blocks = seq // blk
    for buf in range(2):
        kd_ref[buf, 0:blk, :] = jnp.zeros((blk, HEAD_DIM), BF16)
        vd_ref[buf, 0:blk, 0:HEAD_DIM] = jnp.zeros((blk, HEAD_DIM), BF16)
        vd_ref[buf, :, HEAD_DIM:] = jnp.ones((seq + blk, HEAD_DIM), BF16)
    ii = lax.broadcasted_iota(jnp.int32, (blk, 2 * blk), 0)
    jj = lax.broadcasted_iota(jnp.int32, (blk, 2 * blk), 1)
    dist = blk + ii - jj
    in_window = (dist >= 0) & (dist <= DIL_SPAN)
    bias_ref[0] = jnp.where(in_window & (jj >= blk), 0.0, NEG_INF)
    bias_ref[1] = jnp.where(in_window, 0.0, NEG_INF)
    quarter = seq // 4

    gathers, attends = [], []
    for p, r in enumerate(DILATIONS):
        per_seq = n_blocks // r
        shift = per_seq.bit_length() - 1
        buf = p % 2

        def block_of(n, per_seq=per_seq, shift=shift):
            return lax.shift_right_logical(n, shift), n & (per_seq - 1)

        def natural_rows(n, r=r, block_of=block_of):
            c, i = block_of(n)
            src = c + r * blk * i
            if r == 1:
                return pl.ds(pl.multiple_of(src, blk), blk)
            return pl.ds(src, blk, stride=r)

        def gather(n, r=r, buf=buf, block_of=block_of, natural_rows=natural_rows):
            here = pl.ds(pl.multiple_of(blk * n, blk), blk)
            if r == 16:
                c, i = block_of(n)
                rows = pl.ds((c & 3) * quarter + lax.shift_right_logical(c, 2) + 4 * blk * i, blk, stride=4)
                q, k, v = q4_ref[rows, :], k4_ref[rows, :], v4_ref[rows, :]
            else:
                rows = natural_rows(n)
                q, k, v = q_ref[rows, :], k_ref[rows, :], v_ref[rows, :]
            if r == 4:
                q4_ref[here, :], k4_ref[here, :], v4_ref[here, :] = q, k, v
            dst = pl.ds(pl.multiple_of(blk * (n + 1), blk), blk)
            kd_ref[buf, dst, :] = k.astype(BF16)
            vd_ref[buf, dst, 0:HEAD_DIM] = v.astype(BF16)
            qd_ref[buf, here, :] = (q * scale).astype(BF16)

        def attend(n, p=p, buf=buf, block_of=block_of, natural_rows=natural_rows):
            _, i = block_of(n)
            rows = natural_rows(n)
            q = qd_ref[buf, pl.ds(pl.multiple_of(blk * n, blk), blk), :]
            win = pl.ds(pl.multiple_of(blk * n, blk), 2 * blk)
            s = _dot_nt(q, kd_ref[buf, win, :])
            yield
            s = s + bias_ref[jnp.where(i == 0, 0, 1)]
            m = jnp.max(s, axis=-1, keepdims=True)
            pv = _dot(jnp.exp((s - m).astype(BF16)), vd_ref[buf, win, :])
            yield
            l = pv[:, HEAD_DIM:]
            op_ref[p, rows, :] = pv[:, :HEAD_DIM] / l
            lse_ref[p, rows, :] = m + jnp.log(l)

        gathers.append(gather)
        attends.append(attend)

    def one_phase(fn, n):
        fn(n)
        yield

    def first_gather(n, _):
        gathers[0](n)
        return 0

    lax.fori_loop(0, n_blocks, first_gather, 0, unroll=unroll)
    for p in range(len(DILATIONS)):
        prefetch = gathers[p + 1] if p + 1 < len(DILATIONS) else None

        def attend_group(t, _, p=p, prefetch=prefetch):
            steps = []
            for u in range(group):
                steps.append(attends[p](t * group + u))
                if prefetch is not None:
                    steps.append(one_phase(prefetch, t * group + u))
            _interleave(steps)
            return 0

        lax.fori_loop(0, n_blocks // group, attend_group, 0)

    g = g_ref[...]

    def merge(n, _):
        rows = pl.ds(pl.multiple_of(n * blk, blk), blk)
        lses = [lse_ref[p, rows, :] for p in range(len(DILATIONS))]
        top = functools.reduce(jnp.maximum, lses)
        ws = [jnp.exp(x - top) for x in lses]
        den = functools.reduce(lambda a, b_: a + b_, ws)
        num = functools.reduce(lambda a, b_: a + b_,
                               [op_ref[p, rows, :] * ws[p] for p in range(len(DILATIONS))])
        o_ref[rows, :] = _rms(num * (1.0 / den), g).astype(o_ref.dtype)
        return 0

    lax.fori_loop(0, n_blocks, merge, 0, unroll=2 * unroll)


def dil_attention(pd, g, group=32, unroll=4):
    b, s, _ = pd.shape
    n_pat = len(DILATIONS)
    assert DILATIONS == (1, 4, 16) and (s // DIL_SPAN) % group == 0
    return pl.pallas_call(
        functools.partial(_dil_kernel, seq=s, group=group, unroll=unroll),
        grid=(b, N_DIL),
        in_specs=[pl.BlockSpec((None, s, HEAD_DIM), lambda b_, h: (b_, 0, h)),
                  pl.BlockSpec((None, s, HEAD_DIM), lambda b_, h: (b_, 0, N_DIL + h)),
                  pl.BlockSpec((None, s, HEAD_DIM), lambda b_, h: (b_, 0, 2 * N_DIL + h)),
                  pl.BlockSpec((1, HEAD_DIM), lambda b_, h: (0, 0))],
        out_specs=pl.BlockSpec((None, s, HEAD_DIM), lambda b_, h: (b_, 0, h)),
        out_shape=jax.ShapeDtypeStruct((b, s, W_DIL), BF16),
        scratch_shapes=[pltpu.VMEM((2, s, HEAD_DIM), BF16),
                        pltpu.VMEM((2, s + DIL_SPAN, HEAD_DIM), BF16),
                        pltpu.VMEM((2, s + DIL_SPAN, 2 * HEAD_DIM), BF16),
                        pltpu.VMEM((s, HEAD_DIM), F32),
                        pltpu.VMEM((s, HEAD_DIM), F32),
                        pltpu.VMEM((s, HEAD_DIM), F32),
                        pltpu.VMEM((2, DIL_SPAN, 2 * DIL_SPAN), F32),
                        pltpu.VMEM((n_pat, s, HEAD_DIM), F32),
                        pltpu.VMEM((n_pat, s, HEAD_DIM), F32)],
        compiler_params=_params("parallel", "arbitrary"),
        name="dil_attention",
    )(pd, pd, pd, g.reshape(1, HEAD_DIM))


def kernel(x, positions, norm_mix_g, w_in, lambda_q1, lambda_k1, lambda_q2, lambda_k2, g_sb_out, g_diff_out, g_dil_out, w_out, norm_ffn_g, w_gate, w_up, w_down, norm_final_g):
    batch, seq, d = x.shape
    m = batch * seq
    depth = w_in.shape[0]
    xf = x.reshape(m, d)
    tabs, (w_in_bf,) = rope_tables(positions.reshape(m, 1), [(w_in, 0)])
    for layer in range(depth):
        pa, pd = in_proj(xf, norm_mix_g[layer], w_in_bf, tabs)
        pa = pa.reshape(batch, seq, W_A)
        pd = pd.reshape(batch, seq, 3 * W_DIL)
        lambda_init = 0.8 - 0.6 * math.exp(-0.3 * layer)
        lam_vecs = jnp.stack([lambda_q1[layer], lambda_k1[layer],
                              lambda_q2[layer], lambda_k2[layer]]).astype(F32)
        o_sb = sb_attention(pa, g_sb_out[layer]).reshape(m, W_SB)
        side = [(w, layer) for w in (w_out, w_gate, w_up, w_down)]
        side += [(w_in, layer + 1)] if layer + 1 < depth else []
        o_df, converted = diff_attention(pa, lam_vecs, g_diff_out[layer], lambda_init, side)
        o_df = o_df.reshape(m, W_DIFF)
        w_out_bf, w_gate_bf, w_up_bf, w_down_bf = converted[:4]
        w_in_bf = converted[4] if layer + 1 < depth else None
        o_dl = dil_attention(pd, g_dil_out[layer]).reshape(m, W_DIL)
        xf = mlp_block(o_sb, o_df, o_dl, w_out_bf, xf, norm_ffn_g[layer], w_gate_bf, w_up_bf, w_down_bf,
                       final_g=norm_final_g if layer == depth - 1 else None)
    return xf.reshape(batch, seq, d)
```

```python
import functools
import math

import jax
import jax.numpy as jnp
import numpy as np
from jax import lax
from jax.experimental import pallas as pl
from jax.experimental.pallas import tpu as pltpu

F32 = jnp.float32
BF16 = jnp.bfloat16

HEAD_DIM = 128
N_SB = 4
N_DIFF = 4
N_DIL = 8
W_SB = N_SB * HEAD_DIM
W_DIFF = N_DIFF * HEAD_DIM
W_DIL = N_DIL * HEAD_DIM
DIFF_QK = HEAD_DIM // 2
ROPE_THETA = 500000.0
ROPE_FRACTION = 0.25
DILATIONS = (1, 4, 16)
DIL_SPAN = 128
NORM_EPS = 1e-6
LANES = 128
VMEM_LIMIT = 56 * 1024 * 1024

NEG_INF = float("-inf")
LOG2_E = math.log2(math.e)
SB_DEAD_MASS = 106.0


def _params(*sem):
    return pltpu.CompilerParams(dimension_semantics=sem, vmem_limit_bytes=VMEM_LIMIT)


def _dot(a, b):
    return jnp.dot(a, b, preferred_element_type=F32)


def _dot_nt(a, b):
    return lax.dot_general(a, b, (((1,), (1,)), ((), ())), preferred_element_type=F32)


def _tile(x, n):
    return x if n == 1 else jnp.concatenate([x] * n, axis=1)


def _rms(x, g):
    return x * lax.rsqrt(jnp.mean(x * x, axis=-1, keepdims=True) + NORM_EPS) * g


def _side_cast_specs(side, n_steps, step_index):
    in_specs, out_specs, out_shape = [], [], []
    for w, layer in side:
        _, r, c = w.shape
        rows = r // n_steps
        assert rows * n_steps == r and rows % 16 == 0
        in_specs.append(pl.BlockSpec((None, rows, c), lambda *g, layer=layer: (layer, step_index(*g), 0)))
        out_specs.append(pl.BlockSpec((rows, c), lambda *g: (step_index(*g), 0)))
        out_shape.append(jax.ShapeDtypeStruct((r, c), BF16))
    return in_specs, out_specs, out_shape


_ROPE_LAYOUTS = ((DIFF_QK, int(DIFF_QK * ROPE_FRACTION) // 2), (HEAD_DIM, int(HEAD_DIM * ROPE_FRACTION) // 2))


def _rope_lane_consts():
    freqs = [ROPE_THETA ** (-jnp.arange(half, dtype=F32) / half) for _, half in _ROPE_LAYOUTS]
    used = sum(half for _, half in _ROPE_LAYOUTS)
    row = jnp.concatenate(freqs + [jnp.zeros((LANES - used,), F32)])
    return jnp.concatenate([row[None, :], jnp.zeros((7, LANES), F32)])


def _rope_table_kernel(*refs, n_side):
    pos_ref, c_ref = refs[:2]
    side_in, o_ref, side_out = refs[2:2 + n_side], refs[2 + n_side], refs[3 + n_side:]
    for src, dst in zip(side_in, side_out):
        dst[...] = src[...].astype(BF16)
    ang = pos_ref[...].astype(F32) * c_ref[0:1, :]
    cos, sin = jnp.cos(ang), jnp.sin(ang)
    lane = lax.broadcasted_iota(jnp.int32, ang.shape, 1)
    src = 0
    for k, (chunk, half) in enumerate(_ROPE_LAYOUTS):
        c_tab = jnp.ones_like(ang)
        lo_tab = jnp.zeros_like(ang)
        hi_tab = jnp.zeros_like(ang)
        for start in range(0, LANES, chunk):
            for part, dst in enumerate((start, start + half)):
                here = (lane >= dst) & (lane < dst + half)
                shift = (dst - src) % LANES
                c_here = pltpu.roll(cos, shift, 1) if shift else cos
                s_here = pltpu.roll(sin, shift, 1) if shift else sin
                c_tab = jnp.where(here, c_here, c_tab)
                if part == 0:
                    lo_tab = jnp.where(here, -s_here, lo_tab)
                else:
                    hi_tab = jnp.where(here, s_here, hi_tab)
        o_ref[3 * k], o_ref[3 * k + 1], o_ref[3 * k + 2] = c_tab, lo_tab, hi_tab
        src += half


def rope_tables(positions, side=(), tm=512):
    m = positions.shape[0]
    side_in, side_out, side_shape = _side_cast_specs(side, m // tm, lambda i: i)
    out = pl.pallas_call(
        functools.partial(_rope_table_kernel, n_side=len(side)),
        grid=(m // tm,),
        in_specs=[pl.BlockSpec((tm, 1), lambda i: (i, 0)),
                  pl.BlockSpec((8, LANES), lambda i: (0, 0))] + side_in,
        out_specs=[pl.BlockSpec((6, tm, LANES), lambda i: (0, i, 0))] + side_out,
        out_shape=[jax.ShapeDtypeStruct((6, m, LANES), F32)] + side_shape,
        compiler_params=_params("parallel"),
        name="rope_tables",
    )(positions, _rope_lane_consts(), *[w for w, _ in side])
    return out[0], out[1:]


ROPE_NONE, ROPE_DIFF, ROPE_DIL = 0, 1, 2
_ROPE_SHIFT = {ROPE_DIFF: _ROPE_LAYOUTS[0][1], ROPE_DIL: _ROPE_LAYOUTS[1][1]}
W_A = 3 * (W_SB + W_DIFF)
PROJ_TN = 512
_PROJ_KINDS = (ROPE_NONE,) * 3 + (ROPE_DIFF,) * 2 + (ROPE_NONE,) + (ROPE_DIL,) * 4 + (ROPE_NONE,) * 2


def _resident(shape):
    return pl.BlockSpec(shape, lambda *_: (0,) * len(shape), pipeline_mode=pl.Buffered(1))


def _proj_kernel(x_ref, g_ref, w_hbm, tab_ref, oa_ref, od_ref, h_ref, w_ref, sem):
    tn = PROJ_TN

    def chunk_copy(idx):
        cols = pl.ds(idx * tn, tn)
        return pltpu.make_async_copy(w_hbm.at[:, cols], w_ref.at[:, cols], sem.at[idx])

    first = pl.program_id(0) == 0

    @pl.when(first)
    def _():
        for idx in range(len(_PROJ_KINDS)):
            chunk_copy(idx).start()
        _proj_body(x_ref, g_ref, w_ref, tab_ref, oa_ref, od_ref, h_ref, lambda idx: chunk_copy(idx).wait())

    @pl.when(jnp.logical_not(first))
    def _():
        _proj_body(x_ref, g_ref, w_ref, tab_ref, oa_ref, od_ref, h_ref, lambda idx: None)


def _proj_body(x_ref, g_ref, w_ref, tab_ref, oa_ref, od_ref, h_ref, before_chunk):
    h_ref[...] = _rms(x_ref[...], g_ref[...]).astype(BF16)
    tn = PROJ_TN
    for idx, kind in enumerate(_PROJ_KINDS):
        before_chunk(idx)
        acc = _dot(h_ref[...], w_ref[:, idx * tn:(idx + 1) * tn])
        o_ref, col = (oa_ref, idx * tn) if idx * tn < W_A else (od_ref, idx * tn - W_A)
        if kind == ROPE_NONE:
            o_ref[:, col:col + tn] = acc.astype(o_ref.dtype)
            continue
        base = 3 * (kind - 1)
        shift = _ROPE_SHIFT[kind]
        c, lo, hi = tab_ref[base], tab_ref[base + 1], tab_ref[base + 2]
        for q in range(tn // LANES):
            xk = acc[:, q * LANES:(q + 1) * LANES]
            rot = xk * c + pltpu.roll(xk, LANES - shift, 1) * lo + pltpu.roll(xk, shift, 1) * hi
            o_ref[:, col + q * LANES:col + (q + 1) * LANES] = rot.astype(o_ref.dtype)


def in_proj(x, g, w, tabs, tm=256):
    m, d = x.shape
    n = w.shape[1]
    assert n == len(_PROJ_KINDS) * PROJ_TN
    return pl.pallas_call(
        _proj_kernel,
        grid=(m // tm,),
        in_specs=[pl.BlockSpec((tm, d), lambda i: (i, 0)),
                  _resident((1, d)),
                  pl.BlockSpec(memory_space=pl.ANY),
                  pl.BlockSpec((6, tm, LANES), lambda i: (0, i, 0))],
        out_specs=[pl.BlockSpec((tm, W_A), lambda i: (i, 0)),
                   pl.BlockSpec((tm, n - W_A), lambda i: (i, 0))],
        out_shape=[jax.ShapeDtypeStruct((m, W_A), BF16),
                   jax.ShapeDtypeStruct((m, n - W_A), F32)],
        scratch_shapes=[pltpu.VMEM((tm, d), BF16),
                        pltpu.VMEM((d, n), BF16),
                        pltpu.SemaphoreType.DMA((len(_PROJ_KINDS),))],
        compiler_params=_params("arbitrary"),
        name="in_proj",
    )(x, g.reshape(1, d), w, tabs)


def _mlp_kernel(*refs, final_norm):
    if final_norm:
        (sb_hbm, df_hbm, dl_hbm, wo_ref, x_hbm, g_ref, wg_ref, wu_ref, wd_ref, gf_ref, o_ref,
         h_ref, sb_ref, df_ref, dl_ref, x_ref, sem) = refs
    else:
        (sb_hbm, df_hbm, dl_hbm, wo_ref, x_hbm, g_ref, wg_ref, wu_ref, wd_ref, o_ref,
         h_ref, sb_ref, df_ref, dl_ref, x_ref, sem) = refs
    i, j = pl.program_id(0), pl.program_id(1)
    tm = o_ref.shape[0]

    def row_copies(tile):
        slot = tile % 2
        rows = pl.ds(pl.multiple_of(tile * tm, tm), tm)
        pairs = ((sb_hbm, sb_ref), (df_hbm, df_ref), (dl_hbm, dl_ref), (x_hbm, x_ref))
        return [pltpu.make_async_copy(src.at[rows, :], dst.at[slot], sem.at[slot, n])
                for n, (src, dst) in enumerate(pairs)]

    @pl.when((i == 0) & (j == 0))
    def _():
        for copy in row_copies(0):
            copy.start()

    @pl.when((j == 1) & (i + 1 < pl.num_programs(0)))
    def _():
        for copy in row_copies(i + 1):
            copy.start()

    @pl.when(j == 0)
    def _():
        for copy in row_copies(i):
            copy.wait()
        slot = i % 2
        mixed = (_dot(sb_ref[slot], wo_ref[0:W_SB, :])
                 + _dot(df_ref[slot], wo_ref[W_SB:W_SB + W_DIFF, :])
                 + _dot(dl_ref[slot], wo_ref[W_SB + W_DIFF:, :]))
        xn = x_ref[slot] + mixed
        o_ref[...] = xn
        h_ref[...] = _rms(xn, g_ref[...]).astype(BF16)

    h = h_ref[...]
    gate = _dot(h, wg_ref[...])
    up = _dot(h, wu_ref[...])
    act = (gate * jax.nn.sigmoid(gate)) * up
    o_ref[...] += _dot(act.astype(BF16), wd_ref[...])

    if final_norm:
        @pl.when(j == pl.num_programs(1) - 1)
        def _():
            o_ref[...] = _rms(o_ref[...], gf_ref[...])


def mlp_block(o_sb, o_df, o_dl, w_out, x, g, wg, wu, wd, final_g=None, tm=512, tf=512):
    m, d = x.shape
    f = wg.shape[1]
    final_norm = final_g is not None
    assert f // tf >= 2 and m % tm == 0
    in_hbm = pl.BlockSpec(memory_space=pl.ANY)
    in_specs = [in_hbm, in_hbm, in_hbm,
                _resident(w_out.shape),
                in_hbm,
                _resident((1, d)),
                pl.BlockSpec((d, tf), lambda i, j: (0, j)),
                pl.BlockSpec((d, tf), lambda i, j: (0, j)),
                pl.BlockSpec((tf, d), lambda i, j: (j, 0))]
    args = [o_sb, o_df, o_dl, w_out, x, g.reshape(1, d), wg, wu, wd]
    if final_norm:
        in_specs.append(_resident((1, d)))
        args.append(final_g.reshape(1, d))
    return pl.pallas_call(
        functools.partial(_mlp_kernel, final_norm=final_norm),
        grid=(m // tm, f // tf),
        in_specs=in_specs,
        out_specs=pl.BlockSpec((tm, d), lambda i, j: (i, 0)),
        out_shape=jax.ShapeDtypeStruct((m, d), F32),
        scratch_shapes=[pltpu.VMEM((tm, d), BF16),
                        pltpu.VMEM((2, tm, W_SB), BF16),
                        pltpu.VMEM((2, tm, W_DIFF), BF16),
                        pltpu.VMEM((2, tm, W_DIL), BF16),
                        pltpu.VMEM((2, tm, d), F32),
                        pltpu.SemaphoreType.DMA((2, 4))],
        compiler_params=_params("arbitrary", "arbitrary"),
        name="mlp_block",
    )(*args)


def _causal_sweep(i, n_chain, step, keys_per_iter, wide=False, exhausted=None, ahead=False):
    assert n_chain % keys_per_iter == 0 and not (ahead and (wide or keys_per_iter != 1))
    base = n_chain * i
    if wide:
        _interleave([step(c, base, True, c + 1) for c in range(n_chain)])
    else:
        pairs = [(c, d) for d in range(n_chain - 1, -1, -1) for c in range(d, n_chain)]
        near = [step(c, base + d, c == d) for c, d in pairs if c - d <= 1]
        if ahead:
            near.append(step(0, jnp.maximum(base - 1, 0), False, i > 0))
        _interleave(near)
        far = [(c, d) for c, d in pairs if c - d > 1]

        def far_blocks():
            _interleave([step(c, base + d, False) for c, d in far])

        if far and exhausted is not None:
            pl.when(jnp.logical_not(exhausted(2)))(far_blocks)
        elif far:
            far_blocks()

    def body(t, _):
        first = base - 1 - t * keys_per_iter
        if wide:
            _interleave([step(c, first - (keys_per_iter - 1), False, keys_per_iter) for c in range(n_chain)])
        elif ahead:
            _interleave([step(0, jnp.maximum(first - 1, 0), False, first >= 1)]
                        + [step(c, first, False) for c in range(1, n_chain)])
        else:
            _interleave([step(c, first - u, False) for u in range(keys_per_iter) for c in range(n_chain)])
        return 0

    trips = (n_chain // keys_per_iter) * i
    if exhausted is None:
        lax.fori_loop(0, trips, body, 0)
        return

    def more(state):
        t, done = state
        return (t < trips) & jnp.logical_not(done)

    def advance(state):
        body(state[0], 0)
        return state[0] + 1, exhausted(0)

    lax.while_loop(more, advance, (0, exhausted(0)))


def _interleave(steps):
    steps = list(steps)
    while steps:
        alive = []
        for g in steps:
            try:
                next(g)
                alive.append(g)
            except StopIteration:
                pass
        steps = alive


def _from_key_matrix(blk):
    tri = np.tril(np.ones((blk, blk), np.float32))
    return jnp.asarray(np.concatenate([tri, tri], axis=0), BF16)


def _sb_kernel(q_ref, k_ref, v_ref, later_ref, g_ref, o_ref, qs_ref, acc_ref, carry_ref, *,
               blk, n_chain, keys_per_iter):
    i = pl.program_id(2)
    scale = HEAD_DIM ** -0.5
    qs_ref[...] = (q_ref[...].astype(F32) * scale).astype(BF16)
    acc_ref[...] = jnp.zeros_like(acc_ref)
    carry_ref[...] = jnp.zeros_like(carry_ref)
    row = lax.broadcasted_iota(jnp.int32, (blk, blk), 0)
    col = lax.broadcasted_iota(jnp.int32, (blk, blk), 1)
    strict = col < row

    def step(c, j, diag, live=None):
        rows = slice(c * blk, (c + 1) * blk)
        start = pl.multiple_of(j * blk, blk)
        k = k_ref[pl.ds(start, blk), :]
        v = v_ref[pl.ds(start, blk), :]
        z = _dot_nt(qs_ref[rows, :], k)
        yield
        sp = jnp.maximum(z, 0.0) + jnp.log(1.0 + jnp.exp2(jnp.abs(z) * -LOG2_E))
        spm = jnp.where(strict, sp, 0.0) if diag else sp
        if live is not None:
            spm = jnp.where(live, spm, 0.0)
        hi = spm.astype(BF16)
        lo = (spm - hi.astype(F32)).astype(BF16)
        from_key = _dot(jnp.concatenate([hi, lo], axis=1), later_ref[...])
        yield
        loga = z - (from_key + _tile(carry_ref[rows, :], blk // LANES))
        if diag:
            loga = jnp.where(strict, loga, NEG_INF)
        if live is not None:
            loga = jnp.where(live, loga, NEG_INF)
        pv = _dot(jnp.exp(loga).astype(BF16), v)
        carry_ref[rows, :] += from_key[:, 0:1]
        yield
        acc_ref[rows, :] += pv

    _causal_sweep(i, n_chain, step, keys_per_iter, ahead=True,
                  exhausted=lambda chain: jnp.min(carry_ref[chain * blk:, :]) >= SB_DEAD_MASS)
    o_ref[...] = _rms(acc_ref[...], g_ref[...]).astype(o_ref.dtype)


def sb_attention(pa, g, blk=256, n_chain=4, keys_per_iter=1):
    b, s, _ = pa.shape
    bq = blk * n_chain
    return pl.pallas_call(
        functools.partial(_sb_kernel, blk=blk, n_chain=n_chain, keys_per_iter=keys_per_iter),
        grid=(b, N_SB, s // bq),
        in_specs=[pl.BlockSpec((None, bq, HEAD_DIM), lambda b_, h, i: (b_, i, h)),
                  pl.BlockSpec((None, s, HEAD_DIM), lambda b_, h, i: (b_, 0, N_SB + h)),
                  pl.BlockSpec((None, s, HEAD_DIM), lambda b_, h, i: (b_, 0, 2 * N_SB + h)),
                  pl.BlockSpec((2 * blk, blk), lambda b_, h, i: (0, 0)),
                  pl.BlockSpec((1, HEAD_DIM), lambda b_, h, i: (0, 0))],
        out_specs=pl.BlockSpec((None, bq, HEAD_DIM), lambda b_, h, i: (b_, i, h)),
        out_shape=jax.ShapeDtypeStruct((b, s, W_SB), BF16),
        scratch_shapes=[pltpu.VMEM((bq, HEAD_DIM), BF16),
                        pltpu.VMEM((bq, HEAD_DIM), F32),
                        pltpu.VMEM((bq, LANES), F32)],
        compiler_params=_params("parallel", "parallel", "arbitrary"),
        name="sb_attention",
    )(pa, pa, pa, _from_key_matrix(blk), g.reshape(1, HEAD_DIM))


def _diff_kernel(*refs, blk, n_chain, keys_per_iter, lambda_init, n_side):
    q_ref, k_ref, v_ref, lam_ref, g_ref = refs[:5]
    side_in, o_ref = refs[5:5 + n_side], refs[5 + n_side]
    side_out = refs[6 + n_side:6 + 2 * n_side]
    qs_ref, vx_ref, m_ref, acc_ref = refs[6 + 2 * n_side:]
    for src, dst in zip(side_in, side_out):
        dst[...] = src[...].astype(BF16)
    i = pl.program_id(2)
    seq = v_ref.shape[0]

    @pl.when(i == 0)
    def _():
        vx_ref[:, 0:HEAD_DIM] = v_ref[...]
        vx_ref[:, HEAD_DIM:] = jnp.ones((seq, HEAD_DIM), BF16)

    scale = DIFF_QK ** -0.5
    q = q_ref[...].astype(F32) * scale
    lc = lax.broadcasted_iota(jnp.int32, (blk, HEAD_DIM), 1)
    for c in range(n_chain):
        qc = q[c * blk:(c + 1) * blk]
        qs_ref[(2 * c) * blk:(2 * c + 1) * blk, :] = jnp.where(lc < DIFF_QK, qc, 0.0).astype(BF16)
        qs_ref[(2 * c + 1) * blk:(2 * c + 2) * blk, :] = jnp.where(lc >= DIFF_QK, qc, 0.0).astype(BF16)
    row = lax.broadcasted_iota(jnp.int32, (blk, blk), 0)
    col = lax.broadcasted_iota(jnp.int32, (blk, blk), 1)
    causal = col <= row

    def step(c, j, diag, width=1):
        start = pl.multiple_of(j * blk, blk)
        k = k_ref[pl.ds(start, width * blk), :]
        vx = vx_ref[pl.ds(start, width * blk), :]
        rows = [slice((2 * c + e) * blk, (2 * c + e + 1) * blk) for e in range(2)]
        s = [_dot_nt(qs_ref[r, :], k) for r in rows]
        yield
        alpha, pv = [], []
        for e, r in enumerate(rows):
            se = s[e]
            if diag:
                own = jnp.where(causal, se[:, (width - 1) * blk:], NEG_INF)
                se = own if width == 1 else jnp.concatenate([se[:, :(width - 1) * blk], own], axis=1)
            top = jnp.max(se, axis=-1, keepdims=True)
            if diag:
                m_new = jnp.broadcast_to(top, (blk, LANES))
            else:
                m_prev = m_ref[r, :]
                m_new = jnp.maximum(m_prev, top)
                alpha.append(jnp.exp(m_prev - m_new))
            p = jnp.exp((se - _tile(m_new, width * blk // LANES)).astype(BF16))
            m_ref[r, :] = m_new
            pv.append(_dot(p, vx))
        yield
        for e, r in enumerate(rows):
            acc_ref[r, :] = pv[e] if diag else _tile(alpha[e], 2) * acc_ref[r, :] + pv[e]

    _causal_sweep(i, n_chain, step, keys_per_iter, wide=True)
    lq = lam_ref[...]
    lam = (jnp.exp(jnp.sum(lq[0:1] * lq[1:2], axis=-1, keepdims=True))
           - jnp.exp(jnp.sum(lq[2:3] * lq[3:4], axis=-1, keepdims=True)) + lambda_init)
    g = g_ref[...]
    for c in range(n_chain):
        o = []
        for e in range(2):
            acc = acc_ref[(2 * c + e) * blk:(2 * c + e + 1) * blk, :]
            o.append(acc[:, :HEAD_DIM] / acc[:, HEAD_DIM:])
        out = o[0] - lam * o[1]
        o_ref[c * blk:(c + 1) * blk, :] = (_rms(out, g) * (1.0 - lambda_init)).astype(o_ref.dtype)


def diff_attention(pa, lam_vecs, g, lambda_init, side=(), blk=256, n_chain=4, keys_per_iter=4):
    b, s, _ = pa.shape
    c0 = 3 * N_SB
    bq = blk * n_chain
    n_i = s // bq
    side_in, side_out, side_shape = _side_cast_specs(
        side, b * N_DIFF * n_i, lambda b_, h, i: (b_ * N_DIFF + h) * n_i + i)
    out = pl.pallas_call(
        functools.partial(_diff_kernel, blk=blk, n_chain=n_chain, keys_per_iter=keys_per_iter,
                          lambda_init=lambda_init, n_side=len(side)),
        grid=(b, N_DIFF, n_i),
        in_specs=[pl.BlockSpec((None, bq, HEAD_DIM), lambda b_, h, i: (b_, i, c0 + h)),
                  pl.BlockSpec((None, s, HEAD_DIM), lambda b_, h, i: (b_, 0, c0 + N_DIFF + h)),
                  pl.BlockSpec((None, s, HEAD_DIM), lambda b_, h, i: (b_, 0, c0 + 2 * N_DIFF + h)),
                  pl.BlockSpec((4, DIFF_QK), lambda b_, h, i: (0, 0)),
                  pl.BlockSpec((1, HEAD_DIM), lambda b_, h, i: (0, 0))] + side_in,
        out_specs=[pl.BlockSpec((None, bq, HEAD_DIM), lambda b_, h, i: (b_, i, h))] + side_out,
        out_shape=[jax.ShapeDtypeStruct((b, s, W_DIFF), BF16)] + side_shape,
        scratch_shapes=[pltpu.VMEM((2 * bq, HEAD_DIM), BF16),
                        pltpu.VMEM((s, 2 * HEAD_DIM), BF16),
                        pltpu.VMEM((2 * bq, LANES), F32),
                        pltpu.VMEM((2 * bq, 2 * HEAD_DIM), F32)],
        compiler_params=_params("parallel", "parallel", "arbitrary"),
        name="diff_attention",
    )(pa, pa, pa, lam_vecs, g.reshape(1, HEAD_DIM), *[w for w, _ in side])
    return out[0], out[1:]


def _dil_kernel(q_ref, k_ref, v_ref, g_ref, o_ref, qd_ref, kd_ref, vd_ref, q4_ref, k4_ref, v4_ref,
                bias_ref, op_ref, lse_ref, *, seq, group, unroll):
    blk = DIL_SPAN
    scale = HEAD_DIM ** -0.5
    n_blocks = seq // blk
    for buf in range(2):
        kd_ref[buf, 0:blk, :] = jnp.zeros((blk, HEAD_DIM), BF16)
        vd_ref[buf, 0:blk, 0:HEAD_DIM] = jnp.zeros((blk, HEAD_DIM), BF16)
        vd_ref[buf, :, HEAD_DIM:] = jnp.ones((seq + blk, HEAD_DIM), BF16)
    ii = lax.broadcasted_iota(jnp.int32, (blk, 2 * blk), 0)
    jj = lax.broadcasted_iota(jnp.int32, (blk, 2 * blk), 1)
    dist = blk + ii - jj
    in_window = (dist >= 0) & (dist <= DIL_SPAN)
    bias_ref[0] = jnp.where(in_window & (jj >= blk), 0.0, NEG_INF)
    bias_ref[1] = jnp.where(in_window, 0.0, NEG_INF)
    quarter = seq // 4

    gathers, attends = [], []
    for p, r in enumerate(DILATIONS):
        per_seq = n_blocks // r
        shift = per_seq.bit_length() - 1
        buf = p % 2

        def block_of(n, per_seq=per_seq, shift=shift):
            return lax.shift_right_logical(n, shift), n & (per_seq - 1)

        def natural_rows(n, r=r, block_of=block_of):
            c, i = block_of(n)
            src = c + r * blk * i
            if r == 1:
                return pl.ds(pl.multiple_of(src, blk), blk)
            return pl.ds(src, blk, stride=r)

        def gather(n, r=r, buf=buf, block_of=block_of, natural_rows=natural_rows):
            here = pl.ds(pl.multiple_of(blk * n, blk), blk)
            if r == 16:
                c, i = block_of(n)
                rows = pl.ds((c & 3) * quarter + lax.shift_right_logical(c, 2) + 4 * blk * i, blk, stride=4)
                q, k, v = q4_ref[rows, :], k4_ref[rows, :], v4_ref[rows, :]
            else:
                rows = natural_rows(n)
                q, k, v = q_ref[rows, :], k_ref[rows, :], v_ref[rows, :]
            if r == 4:
                q4_ref[here, :], k4_ref[here, :], v4_ref[here, :] = q, k, v
            dst = pl.ds(pl.multiple_of(blk * (n + 1), blk), blk)
            kd_ref[buf, dst, :] = k.astype(BF16)
            vd_ref[buf, dst, 0:HEAD_DIM] = v.astype(BF16)
            qd_ref[buf, here, :] = (q * scale).astype(BF16)

        def attend(n, p=p, buf=buf, block_of=block_of, natural_rows=natural_rows):
            _, i = block_of(n)
            rows = natural_rows(n)
            q = qd_ref[buf, pl.ds(pl.multiple_of(blk * n, blk), blk), :]
            win = pl.ds(pl.multiple_of(blk * n, blk), 2 * blk)
            s = _dot_nt(q, kd_ref[buf, win, :])
            yield
            s = s + bias_ref[jnp.where(i == 0, 0, 1)]
            m = jnp.max(s, axis=-1, keepdims=True)
            pv = _dot(jnp.exp((s - m).astype(BF16)), vd_ref[buf, win, :])
            yield
            l = pv[:, HEAD_DIM:]
            op_ref[p, rows, :] = pv[:, :HEAD_DIM] / l
            lse_ref[p, rows, :] = m + jnp.log(l)

        gathers.append(gather)
        attends.append(attend)

    def one_phase(fn, n):
        fn(n)
        yield

    def first_gather(n, _):
        gathers[0](n)
        return 0

    lax.fori_loop(0, n_blocks, first_gather, 0, unroll=unroll)
    for p in range(len(DILATIONS)):
        prefetch = gathers[p + 1] if p + 1 < len(DILATIONS) else None

        def attend_group(t, _, p=p, prefetch=prefetch):
            steps = []
            for u in range(group):
                steps.append(attends[p](t * group + u))
                if prefetch is not None:
                    steps.append(one_phase(prefetch, t * group + u))
            _interleave(steps)
            return 0

        lax.fori_loop(0, n_blocks // group, attend_group, 0)

    g = g_ref[...]

    def merge(n, _):
        rows = pl.ds(pl.multiple_of(n * blk, blk), blk)
        lses = [lse_ref[p, rows, :] for p in range(len(DILATIONS))]
        top = functools.reduce(jnp.maximum, lses)
        ws = [jnp.exp(x - top) for x in lses]
        den = functools.reduce(lambda a, b_: a + b_, ws)
        num = functools.reduce(lambda a, b_: a + b_,
                               [op_ref[p, rows, :] * ws[p] for p in range(len(DILATIONS))])
        o_ref[rows, :] = _rms(num * (1.0 / den), g).astype(o_ref.dtype)
        return 0

    lax.fori_loop(0, n_blocks, merge, 0, unroll=2 * unroll)


def dil_attention(pd, g, group=32, unroll=4):
    b, s, _ = pd.shape
    n_pat = len(DILATIONS)
    assert DILATIONS == (1, 4, 16) and (s // DIL_SPAN) % group == 0
    return pl.pallas_call(
        functools.partial(_dil_kernel, seq=s, group=group, unroll=unroll),
        grid=(b, N_DIL),
        in_specs=[pl.BlockSpec((None, s, HEAD_DIM), lambda b_, h: (b_, 0, h)),
                  pl.BlockSpec((None, s, HEAD_DIM), lambda b_, h: (b_, 0, N_DIL + h)),
                  pl.BlockSpec((None, s, HEAD_DIM), lambda b_, h: (b_, 0, 2 * N_DIL + h)),
                  pl.BlockSpec((1, HEAD_DIM), lambda b_, h: (0, 0))],
        out_specs=pl.BlockSpec((None, s, HEAD_DIM), lambda b_, h: (b_, 0, h)),
        out_shape=jax.ShapeDtypeStruct((b, s, W_DIL), BF16),
        scratch_shapes=[pltpu.VMEM((2, s, HEAD_DIM), BF16),
                        pltpu.VMEM((2, s + DIL_SPAN, HEAD_DIM), BF16),
                        pltpu.VMEM((2, s + DIL_SPAN, 2 * HEAD_DIM), BF16),
                        pltpu.VMEM((s, HEAD_DIM), F32),
                        pltpu.VMEM((s, HEAD_DIM), F32),
                        pltpu.VMEM((s, HEAD_DIM), F32),
                        pltpu.VMEM((2, DIL_SPAN, 2 * DIL_SPAN), F32),
                        pltpu.VMEM((n_pat, s, HEAD_DIM), F32),
                        pltpu.VMEM((n_pat, s, HEAD_DIM), F32)],
        compiler_params=_params("parallel", "arbitrary"),
        name="dil_attention",
    )(pd, pd, pd, g.reshape(1, HEAD_DIM))


def kernel(x, positions, norm_mix_g, w_in, lambda_q1, lambda_k1, lambda_q2, lambda_k2, g_sb_out, g_diff_out, g_dil_out, w_out, norm_ffn_g, w_gate, w_up, w_down, norm_final_g):
    batch, seq, d = x.shape
    m = batch * seq
    depth = w_in.shape[0]
    xf = x.reshape(m, d)
    tabs, (w_in_bf,) = rope_tables(positions.reshape(m, 1), [(w_in, 0)])
    for layer in range(depth):
        pa, pd = in_proj(xf, norm_mix_g[layer], w_in_bf, tabs)
        pa = pa.reshape(batch, seq, W_A)
        pd = pd.reshape(batch, seq, 3 * W_DIL)
        lambda_init = 0.8 - 0.6 * math.exp(-0.3 * layer)
        lam_vecs = jnp.stack([lambda_q1[layer], lambda_k1[layer],
                              lambda_q2[layer], lambda_k2[layer]]).astype(F32)
        o_sb = sb_attention(pa, g_sb_out[layer]).reshape(m, W_SB)
        side = [(w, layer) for w in (w_out, w_gate, w_up, w_down)]
        side += [(w_in, layer + 1)] if layer + 1 < depth else []
        o_df, converted = diff_attention(pa, lam_vecs, g_diff_out[layer], lambda_init, side)
        o_df = o_df.reshape(m, W_DIFF)
        w_out_bf, w_gate_bf, w_up_bf, w_down_bf = converted[:4]
        w_in_bf = converted[4] if layer + 1 < depth else None
        o_dl = dil_attention(pd, g_dil_out[layer]).reshape(m, W_DIL)
        xf = mlp_block(o_sb, o_df, o_dl, w_out_bf, xf, norm_ffn_g[layer], w_gate_bf, w_up_bf, w_down_bf,
                       final_g=norm_final_g if layer == depth - 1 else None)
    return xf.reshape(batch, seq, d)
```
